```python
import math
import jax, jax.numpy as jnp
from jax import lax
import numpy as np

D_MODEL = 2048
BATCH = 8
SEQ = 2048
DEPTH = 2

N_MIXERS = 2
N_ATTN_LAYERS = (DEPTH + 1) // 2
N_SSM_LAYERS = DEPTH // 2

N_Q_HEADS = 32
N_KV_HEADS = 4
HEAD_DIM = 64
Q_PER_KV = N_Q_HEADS // N_KV_HEADS
Q_DIM = N_Q_HEADS * HEAD_DIM
KV_DIM = N_KV_HEADS * HEAD_DIM
QKV_DIM = Q_DIM + 2 * KV_DIM
WINDOW = 128
BLOCK_Q = WINDOW
NUM_BUCKETS = 32
MAX_DISTANCE = 128

SSM_WIDTH = D_MODEL
SSM_GROUP_CH = 16
SSM_GROUPS = SSM_WIDTH // SSM_GROUP_CH
SSM_STATE = 64
DT_MIN = 0.001
DT_MAX = 0.1

N_EXPERTS = 32
TOP_K = 4
D_EXPERT = D_MODEL
SWIGLU_ALPHA = 1.702
SWIGLU_LIMIT = 7.0
MOE_ROW_BLOCK = 256

NORM_EPS = 1e-5

kernel_name = 'hybrid_swa_s5_moe_adaln'


def _rms_norm(x, gain):
    x32 = x.astype(jnp.float32)
    y = x32 * lax.rsqrt(jnp.mean(x32 * x32, axis=-1, keepdims=True) + NORM_EPS)
    return (y * gain.astype(jnp.float32)).astype(x.dtype)


def _t5_bucket(dist):
    n = np.maximum(dist, 0)
    max_exact = NUM_BUCKETS // 2
    large = max_exact + (np.log(np.maximum(n, 1) / max_exact) / np.log(MAX_DISTANCE / max_exact)
                         * (NUM_BUCKETS - max_exact)).astype(np.int32)
    large = np.minimum(large, NUM_BUCKETS - 1)
    return np.where(n < max_exact, n, large).astype(np.int32)


def _band_bias_and_mask(n_blocks, rel_bias):
    ql = np.arange(BLOCK_Q)[:, None]
    kl = np.arange(2 * BLOCK_Q)[None, :]
    dist = ql + BLOCK_Q - kl
    k_abs = np.arange(n_blocks)[:, None, None] * BLOCK_Q - BLOCK_Q + kl[None]
    valid = (dist >= 0)[None] & (dist < WINDOW)[None] & (k_abs >= 0)
    bias = jnp.take(rel_bias.astype(jnp.float32), jnp.asarray(_t5_bucket(dist)), axis=0)
    bias = jnp.transpose(bias, (2, 0, 1)).reshape(N_KV_HEADS, Q_PER_KV, BLOCK_Q, 2 * BLOCK_Q)
    return bias, jnp.asarray(valid)


def _sliding_window_attention(h, w_qkv, b_qkv, q_gain, k_gain, sinks, w_o, b_o, rel_bias):
    bsz, seq, _ = h.shape
    nblk = seq // BLOCK_Q
    qkv = h @ w_qkv + b_qkv
    q = qkv[..., :Q_DIM].reshape(bsz, seq, N_KV_HEADS, Q_PER_KV, HEAD_DIM)
    k = qkv[..., Q_DIM:Q_DIM + KV_DIM].reshape(bsz, seq, N_KV_HEADS, HEAD_DIM)
    v = qkv[..., Q_DIM + KV_DIM:].reshape(bsz, seq, N_KV_HEADS, HEAD_DIM)
    q = _rms_norm(q, q_gain)
    k = _rms_norm(k, k_gain)
    pad = jnp.zeros((bsz, BLOCK_Q, N_KV_HEADS, HEAD_DIM), k.dtype)

    def band(t):
        prev = jnp.concatenate([pad, t[:, :-BLOCK_Q]], axis=1).reshape(bsz, nblk, BLOCK_Q, N_KV_HEADS, HEAD_DIM)
        cur = t.reshape(bsz, nblk, BLOCK_Q, N_KV_HEADS, HEAD_DIM)
        return jnp.concatenate([prev, cur], axis=2)

    kb, vb = band(k), band(v)
    qb = q.reshape(bsz, nblk, BLOCK_Q, N_KV_HEADS, Q_PER_KV, HEAD_DIM)
    bias, valid = _band_bias_and_mask(nblk, rel_bias)
    logits = jnp.einsum('bnqhgd,bnkhd->bnhgqk', qb, kb).astype(jnp.float32) * (1.0 / math.sqrt(HEAD_DIM))
    logits = jnp.where(valid[None, :, None, None], logits + bias[None, None], -jnp.inf)
    sink = sinks.astype(jnp.float32).reshape(N_KV_HEADS, Q_PER_KV, 1, 1)
    m = jnp.maximum(jnp.max(logits, axis=-1, keepdims=True), sink)
    p = jnp.exp(logits - m)
    probs = (p / (jnp.sum(p, axis=-1, keepdims=True) + jnp.exp(sink - m))).astype(v.dtype)
    o = jnp.einsum('bnhgqk,bnkhd->bnqhgd', probs, vb).reshape(bsz, seq, Q_DIM)
    return o @ w_o + b_o


def _s5_mixer(h, lam_re, lam_im, log_dt, b_re, b_im, c_re, c_im, d_skip, w_glu_a, b_glu_a, w_glu_b, b_glu_b):
    bsz, seq, _ = h.shape
    f32 = jnp.float32
    u = h.astype(f32).reshape(bsz, seq, SSM_GROUPS, SSM_GROUP_CH)
    lr, li = lam_re.astype(f32), lam_im.astype(f32)
    dt = jnp.exp(log_dt.astype(f32))[:, None]
    mag = jnp.exp(lr * dt)
    ab_re, ab_im = mag * jnp.cos(li * dt), mag * jnp.sin(li * dt)
    den = lr * lr + li * li
    nr, ni = ab_re - 1.0, ab_im
    f_re = (nr * lr + ni * li) / den
    f_im = (ni * lr - nr * li) / den
    br, bi = b_re.astype(f32), b_im.astype(f32)
    bb_re = f_re[..., None] * br - f_im[..., None] * bi
    bb_im = f_re[..., None] * bi + f_im[..., None] * br
    x_re = jnp.einsum('bsgc,gpc->bsgp', u, bb_re)
    x_im = jnp.einsum('bsgc,gpc->bsgp', u, bb_im)
    a_re = jnp.broadcast_to(ab_re[None, None], (1, seq, SSM_GROUPS, SSM_STATE))
    a_im = jnp.broadcast_to(ab_im[None, None], (1, seq, SSM_GROUPS, SSM_STATE))

    def combine(e1, e2):
        a1r, a1i, b1r, b1i = e1
        a2r, a2i, b2r, b2i = e2
        return (a2r * a1r - a2i * a1i, a2r * a1i + a2i * a1r,
                a2r * b1r - a2i * b1i + b2r, a2r * b1i + a2i * b1r + b2i)

    _, _, s_re, s_im = lax.associative_scan(combine, (a_re, a_im, x_re, x_im), axis=1)
    y = (jnp.einsum('bsgp,gcp->bsgc', s_re, c_re.astype(f32))
         - jnp.einsum('bsgp,gcp->bsgc', s_im, c_im.astype(f32))
         + d_skip.astype(f32).reshape(SSM_GROUPS, SSM_GROUP_CH) * u)
    y = jax.nn.gelu(y.reshape(bsz, seq, SSM_WIDTH)).astype(h.dtype)
    return (y @ w_glu_a + b_glu_a) * jax.nn.sigmoid(y @ w_glu_b + b_glu_b)


def _clamped_swiglu(gu):
    glu = jnp.minimum(gu[..., :D_EXPERT], SWIGLU_LIMIT)
    lin = jnp.clip(gu[..., D_EXPERT:], -SWIGLU_LIMIT, SWIGLU_LIMIT)
    return glu * jax.nn.sigmoid(SWIGLU_ALPHA * glu) * (lin + 1.0)


def _moe(h, layer, w_router, b_router, w_gate_up, b_gate_up, w_down, b_down):
    bsz, seq, dm = h.shape
    n_tok = bsz * seq
    n_assign = n_tok * TOP_K
    hf = h.reshape(n_tok, dm)
    logits = (hf @ w_router[layer] + b_router[layer]).astype(jnp.float32)
    top_logit, top_idx = lax.top_k(logits, TOP_K)
    gates = jax.nn.softmax(top_logit, axis=-1)
    e_flat = top_idx.reshape(n_assign).astype(jnp.int32)
    tok_flat = jnp.broadcast_to(jnp.arange(n_tok, dtype=jnp.int32)[:, None], (n_tok, TOP_K)).reshape(n_assign)
    g_flat = gates.reshape(n_assign)
    order = jnp.argsort(e_flat)
    e_sorted = e_flat[order]
    counts = jnp.bincount(e_flat, length=N_EXPERTS).astype(jnp.int32)
    padded = (counts + MOE_ROW_BLOCK - 1) // MOE_ROW_BLOCK * MOE_ROW_BLOCK
    pad_end = jnp.cumsum(padded)
    pad_start = pad_end - padded
    grp_start = jnp.cumsum(counts) - counts
    dest = pad_start[e_sorted] + jnp.arange(n_assign, dtype=jnp.int32) - grp_start[e_sorted]
    n_blocks = (n_assign + N_EXPERTS * (MOE_ROW_BLOCK - 1)) // MOE_ROW_BLOCK
    n_rows = n_blocks * MOE_ROW_BLOCK
    row_tok = jnp.full((n_rows,), n_tok, jnp.int32).at[dest].set(tok_flat[order])
    row_gate = jnp.zeros((n_rows,), jnp.float32).at[dest].set(g_flat[order])
    blk_expert = jnp.minimum(jnp.searchsorted(pad_end, jnp.arange(n_blocks, dtype=jnp.int32) * MOE_ROW_BLOCK,
                                              side='right'), N_EXPERTS - 1).astype(jnp.int32)
    h_pad = jnp.concatenate([hf, jnp.zeros((1, dm), hf.dtype)], axis=0)

    def expert_block(acc, blk):
        rows, g, e = blk
        xb = h_pad[rows]
        gu = xb @ w_gate_up[layer, e] + b_gate_up[layer, e]
        y = _clamped_swiglu(gu) @ w_down[layer, e] + b_down[layer, e]
        return acc.at[rows].add(y.astype(jnp.float32) * g[:, None]), None

    acc, _ = lax.scan(expert_block, jnp.zeros((n_tok + 1, dm), jnp.float32),
                      (row_tok.reshape(n_blocks, MOE_ROW_BLOCK), row_gate.reshape(n_blocks, MOE_ROW_BLOCK), blk_expert))
    return acc[:n_tok].reshape(bsz, seq, dm).astype(h.dtype)


def setup_inputs(seed: int = 0) -> dict:
    key = jax.random.key(seed)
    ks = iter(jax.random.split(key, 40))

    def nrm(shape, std):
        return jax.random.normal(next(ks), shape, jnp.float32) * std

    D = D_MODEL
    NA, NS = N_ATTN_LAYERS, N_SSM_LAYERS
    n_idx = jnp.arange(SSM_STATE, dtype=jnp.float32)
    return {
        'x': nrm((BATCH, SEQ, D), 1.0),
        'c': nrm((BATCH, D), 1.0),
        'rel_bias': nrm((NUM_BUCKETS, N_Q_HEADS), 0.5),
        'norm_gain': 1.0 + nrm((DEPTH, 2, D), 0.02),
        'ada_w': nrm((DEPTH, D, 6 * D), 0.5 * D ** -0.5),
        'ada_b': nrm((DEPTH, 6 * D), 0.02),
        'attn_w_qkv': nrm((NA, D, QKV_DIM), D ** -0.5),
        'attn_b_qkv': nrm((NA, QKV_DIM), 0.02),
        'attn_q_gain': 1.0 + nrm((NA, HEAD_DIM), 0.02),
        'attn_k_gain': 1.0 + nrm((NA, HEAD_DIM), 0.02),
        'attn_sinks': nrm((NA, N_Q_HEADS), 1.0),
        'attn_w_o': nrm((NA, Q_DIM, D), Q_DIM ** -0.5),
        'attn_b_o': nrm((NA, D), 0.02),
        'ssm_lam_re': -0.5 * jnp.exp(nrm((NS, SSM_GROUPS, SSM_STATE), 0.05)),
        'ssm_lam_im': math.pi * n_idx + nrm((NS, SSM_GROUPS, SSM_STATE), 0.01),
        'ssm_log_dt': jax.random.uniform(next(ks), (NS, SSM_GROUPS), jnp.float32,
                                         minval=math.log(DT_MIN), maxval=math.log(DT_MAX)),
        'ssm_b_re': nrm((NS, SSM_GROUPS, SSM_STATE, SSM_GROUP_CH), (2 * SSM_GROUP_CH) ** -0.5),
        'ssm_b_im': nrm((NS, SSM_GROUPS, SSM_STATE, SSM_GROUP_CH), (2 * SSM_GROUP_CH) ** -0.5),
        'ssm_c_re': nrm((NS, SSM_GROUPS, SSM_GROUP_CH, SSM_STATE), 2.0 * SSM_STATE ** -0.5),
        'ssm_c_im': nrm((NS, SSM_GROUPS, SSM_GROUP_CH, SSM_STATE), 2.0 * SSM_STATE ** -0.5),
        'ssm_d': nrm((NS, SSM_WIDTH), 1.0),
        'ssm_w_glu_a': nrm((NS, SSM_WIDTH, D), SSM_WIDTH ** -0.5),
        'ssm_b_glu_a': nrm((NS, D), 0.02),
        'ssm_w_glu_b': nrm((NS, SSM_WIDTH, D), SSM_WIDTH ** -0.5),
        'ssm_b_glu_b': nrm((NS, D), 0.02),
        'moe_w_router': nrm((DEPTH, D, N_EXPERTS), D ** -0.5),
        'moe_b_router': nrm((DEPTH, N_EXPERTS), 0.01),
        'moe_w_gate_up': nrm((DEPTH, N_EXPERTS, D, 2 * D_EXPERT), D ** -0.5),
        'moe_b_gate_up': nrm((DEPTH, N_EXPERTS, 2 * D_EXPERT), 0.02),
        'moe_w_down': nrm((DEPTH, N_EXPERTS, D_EXPERT, D), D_EXPERT ** -0.5),
        'moe_b_down': nrm((DEPTH, N_EXPERTS, D), 0.02),
    }


def reference(x, c, rel_bias, norm_gain, ada_w, ada_b,
              attn_w_qkv, attn_b_qkv, attn_q_gain, attn_k_gain, attn_sinks, attn_w_o, attn_b_o,
              ssm_lam_re, ssm_lam_im, ssm_log_dt, ssm_b_re, ssm_b_im, ssm_c_re, ssm_c_im, ssm_d,
              ssm_w_glu_a, ssm_b_glu_a, ssm_w_glu_b, ssm_b_glu_b,
              moe_w_router, moe_b_router, moe_w_gate_up, moe_b_gate_up, moe_w_down, moe_b_down):
    cond = jax.nn.silu(c)
    for layer in range(DEPTH):
        mod = cond @ ada_w[layer] + ada_b[layer]
        sh1, sc1, g1, sh2, sc2, g2 = [m[:, None, :] for m in jnp.split(mod, 6, axis=-1)]
        h = _rms_norm(x, norm_gain[layer, 0]) * (1.0 + sc1) + sh1
        i = layer // N_MIXERS
        if layer % N_MIXERS == 0:
            mix = _sliding_window_attention(h, attn_w_qkv[i], attn_b_qkv[i], attn_q_gain[i], attn_k_gain[i],
                                            attn_sinks[i], attn_w_o[i], attn_b_o[i], rel_bias)
        else:
            mix = _s5_mixer(h, ssm_lam_re[i], ssm_lam_im[i], ssm_log_dt[i], ssm_b_re[i], ssm_b_im[i],
                            ssm_c_re[i], ssm_c_im[i], ssm_d[i], ssm_w_glu_a[i], ssm_b_glu_a[i],
                            ssm_w_glu_b[i], ssm_b_glu_b[i])
        x = x + g1 * mix
        h = _rms_norm(x, norm_gain[layer, 1]) * (1.0 + sc2) + sh2
        x = x + g2 * _moe(h, layer, moe_w_router, moe_b_router, moe_w_gate_up, moe_b_gate_up, moe_w_down, moe_b_down)
    return x
```

```python
import functools
import math

import numpy as np
import jax
import jax.numpy as jnp
from jax import lax
from jax.experimental import pallas as pl
from jax.experimental.pallas import tpu as pltpu

f32 = jnp.float32
bf16 = jnp.bfloat16
i32 = jnp.int32

N_Q_HEADS = 32
N_KV_HEADS = 4
HEAD_DIM = 64
Q_PER_KV = N_Q_HEADS // N_KV_HEADS
WINDOW = 128
NUM_BUCKETS = 32
MAX_DISTANCE = 128
SSM_GROUP_CH = 16
SSM_STATE = 64
N_EXPERTS = 32
TOP_K = 4
SWIGLU_ALPHA = 1.702
SWIGLU_LIMIT = 7.0
NORM_EPS = 1e-5

LANES = 128
SUBLANES = 8
VMEM_LIMIT = 56 * 1024 * 1024

MOE_SUB = 256
MOE_SB_SUBS = 8
MOE_SB_ROWS = MOE_SUB * MOE_SB_SUBS
MOE_CHUNK = 256


def _cparams(n_axes):
    return pltpu.CompilerParams(dimension_semantics=("arbitrary",) * n_axes, vmem_limit_bytes=VMEM_LIMIT)


def _mod_kernel(c_ref, w_ref, b_ref, o_ref):
    c = c_ref[...]
    cond = c * jax.nn.sigmoid(c)
    o_ref[...] = jnp.dot(cond.astype(bf16), w_ref[...].astype(bf16), preferred_element_type=f32) + b_ref[...]


def _modulation(c, ada_w, ada_b, layer):
    bsz, d = c.shape
    n = ada_w.shape[-1]
    tn = 1024
    return pl.pallas_call(
        _mod_kernel,
        grid=(n // tn,),
        in_specs=[
            pl.BlockSpec((bsz, d), lambda j: (0, 0)),
            pl.BlockSpec((None, d, tn), lambda j: (layer, 0, j)),
            pl.BlockSpec((None, 1, tn), lambda j: (layer, 0, j)),
        ],
        out_specs=pl.BlockSpec((bsz, tn), lambda j: (0, j)),
        out_shape=jax.ShapeDtypeStruct((bsz, n), f32),
        compiler_params=_cparams(1),
        name="adaln_mod",
    )(c, ada_w, ada_b.reshape(ada_b.shape[0], 1, n))


def _norm_mod(x_ref, gain_ref, sc_ref, sh_ref):
    x = x_ref[0]
    ms = jnp.mean(x * x, axis=-1, keepdims=True)
    y = x * lax.rsqrt(ms + NORM_EPS) * gain_ref[...]
    return y * (1.0 + sc_ref[0]) + sh_ref[0]


def _norm_kernel(x_ref, gain_ref, sc_ref, sh_ref, h_ref):
    h_ref[0] = _norm_mod(x_ref, gain_ref, sc_ref, sh_ref).astype(h_ref.dtype)


def _split_bf16(v):
    hi = v.astype(bf16)
    lo = (v - hi.astype(f32)).astype(bf16)
    return hi, lo


def _norm_router_kernel(x_ref, gain_ref, sc_ref, sh_ref, wr_ref, br_ref, h_ref, idx_ref, gate_ref):
    h = _norm_mod(x_ref, gain_ref, sc_ref, sh_ref)
    h_ref[0] = h
    h_hi, h_lo = _split_bf16(h)
    w_hi, w_lo = _split_bf16(wr_ref[...])
    dot = functools.partial(jnp.dot, preferred_element_type=f32)
    logits = dot(h_hi, w_hi) + (dot(h_hi, w_lo) + dot(h_lo, w_hi)) + br_ref[...]
    rows, n_exp = logits.shape
    col = lax.broadcasted_iota(i32, (rows, n_exp), 1)
    work = logits
    tops, idxs = [], []
    for _ in range(TOP_K):
        m = jnp.max(work, axis=-1, keepdims=True)
        idx = jnp.min(jnp.where(work == m, col, n_exp), axis=-1, keepdims=True)
        work = jnp.where(col == idx, -jnp.inf, work)
        tops.append(m)
        idxs.append(idx)
    es = [jnp.exp(t - tops[0]) for t in tops]
    denom = es[0] + es[1] + es[2] + es[3]
    lane = lax.broadcasted_iota(i32, (rows, LANES), 1)
    idx_out = jnp.zeros((rows, LANES), i32)
    gate_out = jnp.zeros((rows, LANES), f32)
    for k in range(TOP_K):
        idx_out = jnp.where(lane == k, idxs[k], idx_out)
        gate_out = jnp.where(lane == k, es[k] / denom, gate_out)
    idx_ref[0] = idx_out
    gate_ref[0] = gate_out


def _norm(x, gain, mod3, sc_chunk, sh_chunk, out_dtype, router=None):
    bsz, seq, d = x.shape
    ts = 512
    in_specs = [
        pl.BlockSpec((1, ts, d), lambda b, s: (b, s, 0)),
        pl.BlockSpec((1, d), lambda b, s: (0, 0)),
        pl.BlockSpec((1, 1, d), lambda b, s: (b, 0, sc_chunk)),
        pl.BlockSpec((1, 1, d), lambda b, s: (b, 0, sh_chunk)),
    ]
    h_spec = pl.BlockSpec((1, ts, d), lambda b, s: (b, s, 0))
    args = [x, gain.reshape(1, d), mod3, mod3]
    if router is None:
        return pl.pallas_call(
            _norm_kernel, grid=(bsz, seq // ts), in_specs=in_specs, out_specs=h_spec,
            out_shape=jax.ShapeDtypeStruct((bsz, seq, d), out_dtype),
            compiler_params=_cparams(2), name="norm_mod",
        )(*args)
    w_router, b_router, layer = router
    n_exp = w_router.shape[-1]
    in_specs += [
        pl.BlockSpec((None, d, n_exp), lambda b, s: (layer, 0, 0)),
        pl.BlockSpec((None, 1, n_exp), lambda b, s: (layer, 0, 0)),
    ]
    lane_spec = pl.BlockSpec((1, ts, LANES), lambda b, s: (b, s, 0))
    return pl.pallas_call(
        _norm_router_kernel, grid=(bsz, seq // ts), in_specs=in_specs,
        out_specs=[h_spec, lane_spec, lane_spec],
        out_shape=[jax.ShapeDtypeStruct((bsz, seq, d), f32),
                   jax.ShapeDtypeStruct((bsz, seq, LANES), i32),
                   jax.ShapeDtypeStruct((bsz, seq, LANES), f32)],
        compiler_params=_cparams(2), name="norm_router",
    )(*args, w_router, b_router.reshape(b_router.shape[0], 1, n_exp))


def _dense_kernel(*refs, n_w, n_extra, epilogue):
    x_ref = refs[0]
    w_refs = refs[1:1 + n_w]
    b_refs = refs[1 + n_w:1 + 2 * n_w]
    e_refs = refs[1 + 2 * n_w:1 + 2 * n_w + n_extra]
    o_ref = refs[1 + 2 * n_w + n_extra]
    wbf_refs = refs[2 + 2 * n_w + n_extra:]

    @pl.when(pl.program_id(1) == 0)
    def _():
        for w_ref, wbf in zip(w_refs, wbf_refs):
            wbf[...] = w_ref[...].astype(bf16)

    x = x_ref[...]
    accs = [jnp.dot(x, wbf[...], preferred_element_type=f32) + b_ref[...] for wbf, b_ref in zip(wbf_refs, b_refs)]
    o_ref[...] = epilogue(accs, [e[...] for e in e_refs]).astype(o_ref.dtype)


def _dense(x, ws, bs, layer, epilogue, extras, out_dtype, tm, tn, name):
    m, k = x.shape
    n = ws[0].shape[-1]
    in_specs = [pl.BlockSpec((tm, k), lambda j, i: (i, 0))]
    in_specs += [pl.BlockSpec((None, k, tn), lambda j, i: (layer, 0, j)) for _ in ws]
    in_specs += [pl.BlockSpec((None, 1, tn), lambda j, i: (layer, 0, j)) for _ in bs]
    in_specs += [spec for _, spec in extras]
    args = [x] + list(ws) + [b.reshape(b.shape[0], 1, n) for b in bs] + [a for a, _ in extras]
    return pl.pallas_call(
        functools.partial(_dense_kernel, n_w=len(ws), n_extra=len(extras), epilogue=epilogue),
        grid=(n // tn, m // tm),
        in_specs=in_specs,
        out_specs=pl.BlockSpec((tm, tn), lambda j, i: (i, j)),
        out_shape=jax.ShapeDtypeStruct((m, n), out_dtype),
        scratch_shapes=[pltpu.VMEM((k, tn), bf16) for _ in ws],
        compiler_params=_cparams(2), name=name,
    )(*args)


def _gated_residual_extras(x2d, mod3, gate_chunk, d, tm, tn, seq):
    tiles_per_batch = seq // tm
    return [
        (x2d, pl.BlockSpec((tm, tn), lambda j, i: (i, j))),
        (mod3, pl.BlockSpec((None, 1, tn), lambda j, i: (i // tiles_per_batch, 0, gate_chunk * (d // tn) + j))),
    ]


def _attn_kernel(sinks_ref, q_ref, kp_ref, kc_ref, vp_ref, vc_ref, qg_ref, kg_ref, bias_ref, o_ref):
    n = pl.program_id(1)
    lane = lax.broadcasted_iota(i32, (1, LANES), 1)
    lo = lane < HEAD_DIM
    bq = q_ref.shape[1]

    def halfnorm(v, gain):
        sq = v * v
        s_lo = jnp.sum(jnp.where(lo, sq, 0.0), axis=-1, keepdims=True)
        s_hi = jnp.sum(jnp.where(lo, 0.0, sq), axis=-1, keepdims=True)
        ms = jnp.where(lo, s_lo, s_hi) * (1.0 / HEAD_DIM)
        return v * lax.rsqrt(ms + NORM_EPS) * gain

    k_all = jnp.concatenate([kp_ref[0], kc_ref[0]], axis=0)
    v_all = jnp.concatenate([vp_ref[0], vc_ref[0]], axis=0)
    kcol = lax.broadcasted_iota(i32, (1, 2 * bq), 1)
    key_ok = jnp.logical_or(kcol >= bq, n > 0)
    qgain = qg_ref[...]
    kgain = kg_ref[...]
    pairs = Q_PER_KV // 2
    for c in range(N_KV_HEADS // 2):
        kn = halfnorm(k_all[:, c * LANES:(c + 1) * LANES], kgain)
        kr = pltpu.roll(kn, HEAD_DIM, 1)
        vn = v_all[:, c * LANES:(c + 1) * LANES]
        vr = pltpu.roll(vn, HEAD_DIM, 1)
        for half in range(2):
            g = 2 * c + half
            k_src, k_rot = (kn, kr) if half == 0 else (kr, kn)
            v_src, v_rot = (vn, vr) if half == 0 else (vr, vn)
            k_par = [jnp.where(lo, k_src, 0.0).astype(bf16), jnp.where(lo, 0.0, k_rot).astype(bf16)]
            v_par = [jnp.where(lo, v_src, 0.0).astype(bf16), jnp.where(lo, 0.0, v_rot).astype(bf16)]
            qs = [halfnorm(q_ref[0, :, (g * pairs + p) * LANES:(g * pairs + p + 1) * LANES], qgain)
                  for p in range(pairs)]
            qg = (jnp.concatenate(qs, axis=0) * (1.0 / math.sqrt(HEAD_DIM))).astype(bf16)
            acc = None
            for par in range(2):
                s = lax.dot_general(qg, k_par[par], (((1,), (1,)), ((), ())), preferred_element_type=f32)
                s = s + bias_ref[g, par]
                s = jnp.where(key_ok, s, -jnp.inf)
                sink = jnp.concatenate(
                    [jnp.full((bq, 1), sinks_ref[g * Q_PER_KV + 2 * p + par], f32) for p in range(pairs)], axis=0)
                m = jnp.maximum(jnp.max(s, axis=-1, keepdims=True), sink)
                e = jnp.exp(s - m)
                den = jnp.sum(e, axis=-1, keepdims=True) + jnp.exp(sink - m)
                probs = (e * (1.0 / den)).astype(bf16)
                o = jnp.dot(probs, v_par[par], preferred_element_type=f32)
                acc = o if acc is None else acc + o
            for p in range(pairs):
                col = (g * pairs + p) * LANES
                o_ref[0, :, col:col + LANES] = acc[p * bq:(p + 1) * bq].astype(o_ref.dtype)


def _t5_bucket(dist):
    nn = np.maximum(dist, 0)
    max_exact = NUM_BUCKETS // 2
    large = max_exact + (np.log(np.maximum(nn, 1) / max_exact) / np.log(MAX_DISTANCE / max_exact)
                         * (NUM_BUCKETS - max_exact)).astype(np.int32)
    large = np.minimum(large, NUM_BUCKETS - 1)
    return np.where(nn < max_exact, nn, large).astype(np.int32)


def _attention(qkv, q_gain, k_gain, sinks, rel_bias):
    bsz, seq, _ = qkv.shape
    bq = WINDOW
    q_dim = N_Q_HEADS * HEAD_DIM
    kv_dim = N_KV_HEADS * HEAD_DIM
    pairs = Q_PER_KV // 2
    ql = np.arange(bq)[:, None]
    kl = np.arange(2 * bq)[None, :]
    dist = ql + bq - kl
    in_window = (dist >= 0) & (dist < WINDOW)
    bias = jnp.take(rel_bias.astype(f32), jnp.asarray(_t5_bucket(dist)), axis=0)
    bias = jnp.where(jnp.asarray(in_window)[:, :, None], bias, -jnp.inf)
    bias = jnp.transpose(bias, (2, 0, 1)).reshape(N_KV_HEADS, pairs, 2, bq, 2 * bq)
    bias = jnp.transpose(bias, (0, 2, 1, 3, 4)).reshape(N_KV_HEADS, 2, pairs * bq, 2 * bq)
    gain2 = lambda gn: jnp.concatenate([gn, gn]).reshape(1, LANES).astype(f32)
    k_blk = q_dim // kv_dim
    grid_spec = pltpu.PrefetchScalarGridSpec(
        num_scalar_prefetch=1,
        grid=(bsz, seq // bq),
        in_specs=[
            pl.BlockSpec((1, bq, q_dim), lambda b, n, s: (b, n, 0)),
            pl.BlockSpec((1, bq, kv_dim), lambda b, n, s: (b, jnp.maximum(n - 1, 0), k_blk)),
            pl.BlockSpec((1, bq, kv_dim), lambda b, n, s: (b, n, k_blk)),
            pl.BlockSpec((1, bq, kv_dim), lambda b, n, s: (b, jnp.maximum(n - 1, 0), k_blk + 1)),
            pl.BlockSpec((1, bq, kv_dim), lambda b, n, s: (b, n, k_blk + 1)),
            pl.BlockSpec((1, LANES), lambda b, n, s: (0, 0)),
            pl.BlockSpec((1, LANES), lambda b, n, s: (0, 0)),
            pl.BlockSpec((N_KV_HEADS, 2, pairs * bq, 2 * bq), lambda b, n, s: (0, 0, 0, 0)),
        ],
        out_specs=pl.BlockSpec((1, bq, q_dim), lambda b, n, s: (b, n, 0)),
    )
    return pl.pallas_call(
        _attn_kernel, grid_spec=grid_spec,
        out_shape=jax.ShapeDtypeStruct((bsz, seq, q_dim), bf16),
        compiler_params=_cparams(2), name="swa_attention",
    )(sinks.astype(f32), qkv, qkv, qkv, qkv, qkv, gain2(q_gain), gain2(k_gain), bias)


def _gelu_tanh(y):
    return 0.5 * y * (1.0 + jnp.tanh(math.sqrt(2.0 / math.pi) * (y + 0.044715 * (y * y * y))))


def _ssm_kernel(u_ref, bm_ref, cm_ref, are_ref, aim_ref, d_ref, y_ref, us, xs, st, ys):
    bsz, ts, _ = u_ref.shape
    n_state = are_ref.shape[-1]

    @pl.when(pl.program_id(1) == 0)
    def _():
        st[...] = jnp.zeros_like(st)

    for b in range(bsz):
        us[pl.ds(b, ts, stride=bsz), :] = u_ref[b]
    u = us[...]
    xs[...] = jnp.dot(u.astype(bf16), bm_ref[0], preferred_element_type=f32)
    a_re = jnp.broadcast_to(are_ref[0], (bsz, n_state))
    a_im = jnp.broadcast_to(aim_ref[0], (bsz, n_state))

    def step(t, carry):
        s_re, s_im = carry
        r0 = pl.multiple_of(t * bsz, bsz)
        x_re = xs[pl.ds(r0, bsz), 0:n_state]
        x_im = xs[pl.ds(r0, bsz), n_state:2 * n_state]
        n_re = a_re * s_re - a_im * s_im + x_re
        n_im = a_re * s_im + a_im * s_re + x_im
        xs[pl.ds(r0, bsz), 0:n_state] = n_re
        xs[pl.ds(r0, bsz), n_state:2 * n_state] = n_im
        return n_re, n_im

    s_re, s_im = lax.fori_loop(0, ts, step, (st[0], st[1]), unroll=8)
    st[0] = s_re
    st[1] = s_im
    y = jnp.dot(xs[...].astype(bf16), cm_ref[0], preferred_element_type=f32) + d_ref[0] * u
    ys[...] = _gelu_tanh(y)
    for b in range(bsz):
        y_ref[b] = ys[pl.ds(b, ts, stride=bsz), :].astype(y_ref.dtype)


def _ssm(h, lam_re, lam_im, log_dt, b_re, b_im, c_re, c_im, d_skip):
    bsz, seq, width = h.shape
    assert bsz == SUBLANES
    n_grp, n_st = lam_re.shape
    gpt = LANES // SSM_GROUP_CH
    n_tiles = width // LANES
    ns = gpt * n_st
    dt = jnp.exp(log_dt.astype(f32))[:, None]
    lr, li = lam_re.astype(f32), lam_im.astype(f32)
    mag = jnp.exp(lr * dt)
    ab_re, ab_im = mag * jnp.cos(li * dt), mag * jnp.sin(li * dt)
    den = lr * lr + li * li
    nr, ni = ab_re - 1.0, ab_im
    f_re = (nr * lr + ni * li) / den
    f_im = (ni * lr - nr * li) / den
    br, bi = b_re.astype(f32), b_im.astype(f32)
    bb_re = f_re[..., None] * br - f_im[..., None] * bi
    bb_im = f_re[..., None] * bi + f_im[..., None] * br
    eye = jnp.eye(gpt, dtype=f32)

    def blockdiag_in(bb):
        t = bb.reshape(n_tiles, gpt, n_st, SSM_GROUP_CH)
        return jnp.einsum("tgpc,gh->tgchp", t, eye).reshape(n_tiles, LANES, ns)

    def blockdiag_out(cc):
        t = cc.reshape(n_tiles, gpt, SSM_GROUP_CH, n_st)
        return jnp.einsum("tgcp,gh->tgphc", t, eye).reshape(n_tiles, ns, LANES)

    bm = jnp.concatenate([blockdiag_in(bb_re), blockdiag_in(bb_im)], axis=-1).astype(bf16)
    cm = jnp.concatenate([blockdiag_out(c_re.astype(f32)), -blockdiag_out(c_im.astype(f32))], axis=1).astype(bf16)
    a_re = ab_re.reshape(n_tiles, 1, ns)
    a_im = ab_im.reshape(n_tiles, 1, ns)
    dsk = d_skip.astype(f32).reshape(n_tiles, 1, LANES)
    ts = 256
    return pl.pallas_call(
        _ssm_kernel,
        grid=(n_tiles, seq // ts),
        in_specs=[
            pl.BlockSpec((bsz, ts, LANES), lambda g, t: (0, t, g)),
            pl.BlockSpec((1, LANES, 2 * ns), lambda g, t: (g, 0, 0)),
            pl.BlockSpec((1, 2 * ns, LANES), lambda g, t: (g, 0, 0)),
            pl.BlockSpec((1, 1, ns), lambda g, t: (g, 0, 0)),
            pl.BlockSpec((1, 1, ns), lambda g, t: (g, 0, 0)),
            pl.BlockSpec((1, 1, LANES), lambda g, t: (g, 0, 0)),
        ],
        out_specs=pl.BlockSpec((bsz, ts, LANES), lambda g, t: (0, t, g)),
        out_shape=jax.ShapeDtypeStruct((bsz, seq, width), bf16),
        scratch_shapes=[
            pltpu.VMEM((ts * bsz, LANES), f32),
            pltpu.VMEM((ts * bsz, 2 * ns), f32),
            pltpu.VMEM((2, bsz, ns), f32),
            pltpu.VMEM((ts * bsz, LANES), f32),
        ],
        compiler_params=_cparams(2), name="s5_ssm",
    )(h, bm, cm, a_re, a_im, dsk)


def _gather_kernel(valid_ref, idx_ref, h_hbm, o_ref, buf, sem):
    i = pl.program_id(0)
    rows = o_ref.shape[0]

    @pl.when(valid_ref[i] > 0)
    def _():
        def issue(r, carry):
            pltpu.make_async_copy(h_hbm.at[pl.ds(idx_ref[0, 0, r], 1)], buf.at[pl.ds(r, 1)], sem).start()
            return carry

        lax.fori_loop(0, rows, issue, 0, unroll=8)
        pltpu.make_async_copy(h_hbm.at[pl.ds(0, rows)], buf, sem).wait()
        o_ref[...] = buf[...].astype(o_ref.dtype)

    @pl.when(valid_ref[i] == 0)
    def _():
        o_ref[...] = jnp.zeros_like(o_ref)


def _moe_gather(h2d, row_tok, chunk_valid):
    n_rows = row_tok.shape[0]
    d = h2d.shape[1]
    n_chunks = n_rows // MOE_SUB
    grid_spec = pltpu.PrefetchScalarGridSpec(
        num_scalar_prefetch=1,
        grid=(n_chunks,),
        in_specs=[
            pl.BlockSpec((1, 1, MOE_SUB), lambda i, v: (i, 0, 0), memory_space=pltpu.SMEM),
            pl.BlockSpec(memory_space=pl.ANY),
        ],
        out_specs=pl.BlockSpec((MOE_SUB, d), lambda i, v: (i, 0)),
        scratch_shapes=[pltpu.VMEM((MOE_SUB, d), f32), pltpu.SemaphoreType.DMA(())],
    )
    return pl.pallas_call(
        _gather_kernel, grid_spec=grid_spec,
        out_shape=jax.ShapeDtypeStruct((n_rows, d), bf16),
        compiler_params=_cparams(1), name="moe_gather",
    )(chunk_valid, row_tok.reshape(n_chunks, 1, MOE_SUB), h2d)


def _swiglu(g, l):
    glu = jnp.minimum(g, SWIGLU_LIMIT)
    lin = jnp.clip(l, -SWIGLU_LIMIT, SWIGLU_LIMIT)
    return glu * jax.nn.sigmoid(SWIGLU_ALPHA * glu) * (lin + 1.0)


def _moe_kernel(sbe_ref, sbn_ref, sbsrc_ref, x_ref, wg_ref, wl_ref, bg_ref, bl_ref, wd_ref, bd_ref, o_ref,
                a_scr, wg_bf, wl_bf, wd_bf, *, n_chunks):
    s = pl.program_id(0)
    c = pl.program_id(1)
    nsub = sbn_ref[s]

    @pl.when(jnp.logical_and(c < n_chunks, nsub > 0))
    def _():
        wg_bf[...] = wg_ref[...].astype(bf16)
        wl_bf[...] = wl_ref[...].astype(bf16)

        def body(i, carry):
            r = pl.multiple_of(i * MOE_SUB, MOE_SUB)
            xs = x_ref[pl.ds(r, MOE_SUB), :]
            g = jnp.dot(xs, wg_bf[...], preferred_element_type=f32) + bg_ref[...]
            l = jnp.dot(xs, wl_bf[...], preferred_element_type=f32) + bl_ref[...]
            a_scr[c, pl.ds(r, MOE_SUB), :] = _swiglu(g, l).astype(bf16)
            return carry

        lax.fori_loop(0, nsub, body, 0)

    @pl.when(c >= n_chunks)
    def _():
        @pl.when(nsub > 0)
        def _():
            wd_bf[...] = wd_ref[...].astype(bf16)

        def body(i, carry):
            r = pl.multiple_of(i * MOE_SUB, MOE_SUB)
            a = jnp.concatenate([a_scr[cc, pl.ds(r, MOE_SUB), :] for cc in range(n_chunks)], axis=1)
            o_ref[pl.ds(r, MOE_SUB), :] = jnp.dot(a, wd_bf[...], preferred_element_type=f32) + bd_ref[...]
            return carry

        lax.fori_loop(0, nsub, body, 0)

        def zero(i, carry):
            r = pl.multiple_of(i * MOE_SUB, MOE_SUB)
            o_ref[pl.ds(r, MOE_SUB), :] = jnp.zeros((MOE_SUB, o_ref.shape[1]), f32)
            return carry

        lax.fori_loop(nsub, MOE_SB_SUBS, zero, 0)


def _moe_experts(x_sorted, sb_expert, sb_nsub, sb_src, w_gate_up, b_gate_up, w_down, b_down, layer):
    n_rows, d = x_sorted.shape
    n_sb = n_rows // MOE_SB_ROWS
    d_exp = w_down.shape[2]
    n_chunks = d_exp // MOE_CHUNK
    n_out_chunks = d // MOE_CHUNK
    assert n_out_chunks == n_chunks
    last = n_chunks - 1

    def valid(s, n):
        return n[s] > 0

    def wg_map(s, c, e, n, src):
        return (layer, e[s], 0, jnp.where(valid(s, n), jnp.minimum(c, last), last))

    def wl_map(s, c, e, n, src):
        return (layer, e[s], 0, n_chunks + jnp.where(valid(s, n), jnp.minimum(c, last), last))

    def wd_map(s, c, e, n, src):
        return (layer, e[s], 0, jnp.where(valid(s, n), jnp.maximum(c - n_chunks, 0), last))

    def out_map(s, c, e, n, src):
        return (s, jnp.maximum(c - n_chunks, 0))

    grid_spec = pltpu.PrefetchScalarGridSpec(
        num_scalar_prefetch=3,
        grid=(n_sb, n_chunks + n_out_chunks),
        in_specs=[
            pl.BlockSpec((MOE_SB_ROWS, d), lambda s, c, e, n, src: (src[s], 0)),
            pl.BlockSpec((None, None, d, MOE_CHUNK), wg_map),
            pl.BlockSpec((None, None, d, MOE_CHUNK), wl_map),
            pl.BlockSpec((None, None, 1, MOE_CHUNK), wg_map),
            pl.BlockSpec((None, None, 1, MOE_CHUNK), wl_map),
            pl.BlockSpec((None, None, d_exp, MOE_CHUNK), wd_map),
            pl.BlockSpec((None, None, 1, MOE_CHUNK), wd_map),
        ],
        out_specs=pl.BlockSpec((MOE_SB_ROWS, MOE_CHUNK), out_map),
        scratch_shapes=[
            pltpu.VMEM((n_chunks, MOE_SB_ROWS, MOE_CHUNK), bf16),
            pltpu.VMEM((d, MOE_CHUNK), bf16),
            pltpu.VMEM((d, MOE_CHUNK), bf16),
            pltpu.VMEM((d_exp, MOE_CHUNK), bf16),
        ],
    )
    n_l, n_e = b_gate_up.shape[:2]
    return pl.pallas_call(
        functools.partial(_moe_kernel, n_chunks=n_chunks), grid_spec=grid_spec,
        out_shape=jax.ShapeDtypeStruct((n_rows, d), f32),
        compiler_params=_cparams(2), name="moe_experts",
    )(sb_expert, sb_nsub, sb_src, x_sorted, w_gate_up, w_gate_up,
      b_gate_up.reshape(n_l, n_e, 1, -1), b_gate_up.reshape(n_l, n_e, 1, -1),
      w_down, b_down.reshape(n_l, n_e, 1, -1))


def _combine_kernel(dest_ref, y_hbm, gate_ref, x_ref, g2_ref, o_ref, ybuf, sem):
    tm = x_ref.shape[0]
    for k in range(TOP_K):
        def issue(t, carry, k=k):
            pltpu.make_async_copy(y_hbm.at[pl.ds(dest_ref[0, 0, k * tm + t], 1)], ybuf.at[k, pl.ds(t, 1)], sem).start()
            return carry

        lax.fori_loop(0, tm, issue, 0, unroll=8)
    for k in range(TOP_K):
        pltpu.make_async_copy(y_hbm.at[pl.ds(0, tm)], ybuf.at[k], sem).wait()
    gates = gate_ref[...]
    acc = gates[:, 0:1] * ybuf[0]
    for k in range(1, TOP_K):
        acc = acc + gates[:, k:k + 1] * ybuf[k]
    o_ref[...] = x_ref[...] + g2_ref[...] * acc


def _moe_combine(y_sorted, dest, gates, x2d, mod3, gate_chunk, seq):
    n_tok, d = x2d.shape
    tm = 128
    n_tiles = n_tok // tm
    dest_km = jnp.transpose(dest.reshape(n_tiles, tm, TOP_K), (0, 2, 1)).reshape(n_tiles, 1, TOP_K * tm)
    tiles_per_batch = seq // tm
    return pl.pallas_call(
        _combine_kernel,
        grid=(n_tiles,),
        in_specs=[
            pl.BlockSpec((1, 1, TOP_K * tm), lambda i: (i, 0, 0), memory_space=pltpu.SMEM),
            pl.BlockSpec(memory_space=pl.ANY),
            pl.BlockSpec((tm, LANES), lambda i: (i, 0)),
            pl.BlockSpec((tm, d), lambda i: (i, 0)),
            pl.BlockSpec((None, 1, d), lambda i: (i // tiles_per_batch, 0, gate_chunk)),
        ],
        out_specs=pl.BlockSpec((tm, d), lambda i: (i, 0)),
        out_shape=jax.ShapeDtypeStruct((n_tok, d), f32),
        scratch_shapes=[pltpu.VMEM((TOP_K, tm, d), f32), pltpu.SemaphoreType.DMA(())],
        compiler_params=_cparams(1), name="moe_combine",
    )(dest_km, y_sorted, gates, x2d, mod3)


def _routing_tables(top_idx):
    n_tok = top_idx.shape[0]
    n_assign = n_tok * TOP_K
    n_sb_max = (n_assign // MOE_SUB + N_EXPERTS) // MOE_SB_SUBS + N_EXPERTS
    e_flat = top_idx.reshape(n_assign)
    onehot = (e_flat[:, None] == jnp.arange(N_EXPERTS, dtype=i32)[None, :]).astype(i32)
    csum = jnp.cumsum(onehot, axis=0)
    rank = jnp.sum(onehot * csum, axis=1) - 1
    counts = csum[-1]
    n_sub_e = (counts + MOE_SUB - 1) // MOE_SUB
    n_sb_e = (n_sub_e + MOE_SB_SUBS - 1) // MOE_SB_SUBS
    sb_end = jnp.cumsum(n_sb_e)
    sb_start = sb_end - n_sb_e
    dest = sb_start[e_flat] * MOE_SB_ROWS + rank
    tok_flat = jnp.arange(n_assign, dtype=i32) // TOP_K
    row_tok = jnp.zeros((n_sb_max * MOE_SB_ROWS,), i32).at[dest].set(tok_flat)
    s = jnp.arange(n_sb_max, dtype=i32)
    n_sb_total = sb_end[-1]
    sb_valid = s < n_sb_total
    sb_src = jnp.where(sb_valid, s, n_sb_total - 1).astype(i32)
    sb_e = jnp.minimum(jnp.searchsorted(sb_end, sb_src, side="right"), N_EXPERTS - 1).astype(i32)
    rows_left = counts[sb_e] - (sb_src - sb_start[sb_e]) * MOE_SB_ROWS
    sb_nsub = jnp.where(sb_valid, (jnp.clip(rows_left, 0, MOE_SB_ROWS) + MOE_SUB - 1) // MOE_SUB, 0).astype(i32)
    chunk_valid = (jnp.arange(MOE_SB_SUBS, dtype=i32)[None, :] < sb_nsub[:, None]).astype(i32).reshape(-1)
    return dest.reshape(n_tok, TOP_K).astype(i32), row_tok, chunk_valid, sb_e, sb_nsub, sb_src


def kernel(x, c, rel_bias, norm_gain, ada_w, ada_b, attn_w_qkv, attn_b_qkv, attn_q_gain, attn_k_gain, attn_sinks, attn_w_o, attn_b_o, ssm_lam_re, ssm_lam_im, ssm_log_dt, ssm_b_re, ssm_b_im, ssm_c_re, ssm_c_im, ssm_d, ssm_w_glu_a, ssm_b_glu_a, ssm_w_glu_b, ssm_b_glu_b, moe_w_router, moe_b_router, moe_w_gate_up, moe_b_gate_up, moe_w_down, moe_b_down):
    bsz, seq, d = x.shape
    n_tok = bsz * seq
    depth = norm_gain.shape[0]
    tm, tn = 1024, 512
    SH1, SC1, G1, SH2, SC2, G2 = range(6)

    def gated(accs, extras):
        x_res, gate = extras
        return x_res + gate * accs[0]

    def gated_glu(accs, extras):
        x_res, gate = extras
        return x_res + gate * (accs[0] * jax.nn.sigmoid(accs[1]))

    for layer in range(depth):
        mod3 = _modulation(c, ada_w, ada_b, layer).reshape(bsz, 1, 6 * d)
        x2d = x.reshape(n_tok, d)
        res_extras = _gated_residual_extras(x2d, mod3, G1, d, tm, tn, seq)
        i = layer // 2
        if layer % 2 == 0:
            h = _norm(x, norm_gain[layer, 0], mod3, SC1, SH1, bf16)
            qkv = _dense(h.reshape(n_tok, d), [attn_w_qkv], [attn_b_qkv], i, lambda accs, extras: accs[0], [],
                         f32, tm, 512, "qkv_proj")
            o = _attention(qkv.reshape(bsz, seq, -1), attn_q_gain[i], attn_k_gain[i], attn_sinks[i], rel_bias)
            x2d = _dense(o.reshape(n_tok, -1), [attn_w_o], [attn_b_o], i, gated, res_extras, f32, tm, tn, "attn_out")
        else:
            h = _norm(x, norm_gain[layer, 0], mod3, SC1, SH1, f32)
            y = _ssm(h, ssm_lam_re[i], ssm_lam_im[i], ssm_log_dt[i], ssm_b_re[i], ssm_b_im[i],
                     ssm_c_re[i], ssm_c_im[i], ssm_d[i])
            x2d = _dense(y.reshape(n_tok, d), [ssm_w_glu_a, ssm_w_glu_b], [ssm_b_glu_a, ssm_b_glu_b], i,
                         gated_glu, res_extras, f32, tm, tn, "ssm_glu")
        x = x2d.reshape(bsz, seq, d)
        h, top_idx, gates = _norm(x, norm_gain[layer, 1], mod3, SC2, SH2, f32,
                                  router=(moe_w_router, moe_b_router, layer))
        top_idx = top_idx.reshape(n_tok, LANES)[:, :TOP_K]
        dest, row_tok, chunk_valid, sb_e, sb_nsub, sb_src = _routing_tables(top_idx)
        x_sorted = _moe_gather(h.reshape(n_tok, d), row_tok, chunk_valid)
        y_sorted = _moe_experts(x_sorted, sb_e, sb_nsub, sb_src, moe_w_gate_up, moe_b_gate_up,
                                moe_w_down, moe_b_down, layer)
        x = _moe_combine(y_sorted, dest, gates.reshape(n_tok, LANES), x2d, mod3, G2, seq).reshape(bsz, seq, d)
    return x
```

```python
import functools
import math

import numpy as np
import jax
import jax.numpy as jnp
from jax import lax
from jax.experimental import pallas as pl
from jax.experimental.pallas import tpu as pltpu

f32 = jnp.float32
bf16 = jnp.bfloat16
i32 = jnp.int32

N_Q_HEADS = 32
N_KV_HEADS = 4
HEAD_DIM = 64
Q_PER_KV = N_Q_HEADS // N_KV_HEADS
WINDOW = 128
NUM_BUCKETS = 32
MAX_DISTANCE = 128
SSM_GROUP_CH = 16
SSM_STATE = 64
N_EXPERTS = 32
TOP_K = 4
SWIGLU_ALPHA = 1.702
SWIGLU_LIMIT = 7.0
NORM_EPS = 1e-5

LANES = 128
SUBLANES = 8
VMEM_LIMIT = 56 * 1024 * 1024

MOE_SUB = 256
MOE_SB_SUBS = 8
MOE_SB_ROWS = MOE_SUB * MOE_SB_SUBS
MOE_GROUP = 4
MOE_CHUNK = 256


def _cparams(n_axes):
    return pltpu.CompilerParams(dimension_semantics=("arbitrary",) * n_axes, vmem_limit_bytes=VMEM_LIMIT)


def _mod_kernel(c_ref, w_ref, b_ref, o_ref):
    c = c_ref[...]
    cond = c * jax.nn.sigmoid(c)
    o_ref[...] = jnp.dot(cond.astype(bf16), w_ref[...].astype(bf16), preferred_element_type=f32) + b_ref[...]


def _modulation(c, ada_w, ada_b, layer):
    bsz, d = c.shape
    n = ada_w.shape[-1]
    tn = 1024
    return pl.pallas_call(
        _mod_kernel,
        grid=(n // tn,),
        in_specs=[
            pl.BlockSpec((bsz, d), lambda j: (0, 0)),
            pl.BlockSpec((None, d, tn), lambda j: (layer, 0, j)),
            pl.BlockSpec((None, 1, tn), lambda j: (layer, 0, j)),
        ],
        out_specs=pl.BlockSpec((bsz, tn), lambda j: (0, j)),
        out_shape=jax.ShapeDtypeStruct((bsz, n), f32),
        compiler_params=_cparams(1),
        name="adaln_mod",
    )(c, ada_w, ada_b.reshape(ada_b.shape[0], 1, n))


def _norm_mod(x_ref, gain_ref, sc_ref, sh_ref):
    x = x_ref[0]
    ms = jnp.mean(x * x, axis=-1, keepdims=True)
    y = x * lax.rsqrt(ms + NORM_EPS) * gain_ref[...]
    return y * (1.0 + sc_ref[0]) + sh_ref[0]


def _norm_kernel(x_ref, gain_ref, sc_ref, sh_ref, h_ref):
    h_ref[0] = _norm_mod(x_ref, gain_ref, sc_ref, sh_ref).astype(h_ref.dtype)


def _split_bf16(v):
    hi = v.astype(bf16)
    lo = (v - hi.astype(f32)).astype(bf16)
    return hi, lo


def _norm_router_kernel(x_ref, gain_ref, sc_ref, sh_ref, wr_ref, br_ref, h_ref, idx_ref, gate_ref):
    h = _norm_mod(x_ref, gain_ref, sc_ref, sh_ref)
    ts, d = h.shape
    n_col = d // LANES
    for j in range(n_col):
        h_ref[pl.ds(j, ts, stride=n_col), :] = h[:, j * LANES:(j + 1) * LANES]
    h_hi, h_lo = _split_bf16(h)
    w_hi, w_lo = _split_bf16(wr_ref[...])
    dot = functools.partial(jnp.dot, preferred_element_type=f32)
    logits = dot(h_hi, w_hi) + (dot(h_hi, w_lo) + dot(h_lo, w_hi)) + br_ref[...]
    rows, n_exp = logits.shape
    col = lax.broadcasted_iota(i32, (rows, n_exp), 1)
    work = logits
    tops, idxs = [], []
    for _ in range(TOP_K):
        m = jnp.max(work, axis=-1, keepdims=True)
        idx = jnp.min(jnp.where(work == m, col, n_exp), axis=-1, keepdims=True)
        work = jnp.where(col == idx, -jnp.inf, work)
        tops.append(m)
        idxs.append(idx)
    es = [jnp.exp(t - tops[0]) for t in tops]
    denom = es[0] + es[1] + es[2] + es[3]
    lane = lax.broadcasted_iota(i32, (rows, LANES), 1)
    idx_out = jnp.zeros((rows, LANES), i32)
    gate_out = jnp.zeros((rows, LANES), f32)
    for k in range(TOP_K):
        idx_out = jnp.where(lane == k, idxs[k], idx_out)
        gate_out = jnp.where(lane == k, es[k] / denom, gate_out)
    idx_ref[0] = idx_out
    gate_ref[0] = gate_out


def _norm(x, gain, mod3, sc_chunk, sh_chunk, out_dtype, router=None):
    bsz, seq, d = x.shape
    ts = 512
    in_specs = [
        pl.BlockSpec((1, ts, d), lambda b, s: (b, s, 0)),
        pl.BlockSpec((1, d), lambda b, s: (0, 0)),
        pl.BlockSpec((1, 1, d), lambda b, s: (b, 0, sc_chunk)),
        pl.BlockSpec((1, 1, d), lambda b, s: (b, 0, sh_chunk)),
    ]
    h_spec = pl.BlockSpec((1, ts, d), lambda b, s: (b, s, 0))
    args = [x, gain.reshape(1, d), mod3, mod3]
    if router is None:
        return pl.pallas_call(
            _norm_kernel, grid=(bsz, seq // ts), in_specs=in_specs, out_specs=h_spec,
            out_shape=jax.ShapeDtypeStruct((bsz, seq, d), out_dtype),
            compiler_params=_cparams(2), name="norm_mod",
        )(*args)
    w_router, b_router, layer = router
    n_exp = w_router.shape[-1]
    in_specs += [
        pl.BlockSpec((None, d, n_exp), lambda b, s: (layer, 0, 0)),
        pl.BlockSpec((None, 1, n_exp), lambda b, s: (layer, 0, 0)),
    ]
    lane_spec = pl.BlockSpec((1, ts, LANES), lambda b, s: (b, s, 0))
    n_col = d // LANES
    h3_spec = pl.BlockSpec((ts * n_col, LANES), lambda b, s: (b * (seq // ts) + s, 0))
    return pl.pallas_call(
        _norm_router_kernel, grid=(bsz, seq // ts), in_specs=in_specs,
        out_specs=[h3_spec, lane_spec, lane_spec],
        out_shape=[jax.ShapeDtypeStruct((bsz * seq * n_col, LANES), f32),
                   jax.ShapeDtypeStruct((bsz, seq, LANES), i32),
                   jax.ShapeDtypeStruct((bsz, seq, LANES), f32)],
        compiler_params=_cparams(2), name="norm_router",
    )(*args, w_router, b_router.reshape(b_router.shape[0], 1, n_exp))


def _dense_kernel(*refs, n_w, n_extra, epilogue):
    x_ref = refs[0]
    w_refs = refs[1:1 + n_w]
    b_refs = refs[1 + n_w:1 + 2 * n_w]
    e_refs = refs[1 + 2 * n_w:1 + 2 * n_w + n_extra]
    o_ref = refs[1 + 2 * n_w + n_extra]
    wbf_refs = refs[2 + 2 * n_w + n_extra:]

    @pl.when(pl.program_id(1) == 0)
    def _():
        for w_ref, wbf in zip(w_refs, wbf_refs):
            wbf[...] = w_ref[...].astype(bf16)

    x = x_ref[...]
    accs = [jnp.dot(x, wbf[...], preferred_element_type=f32) + b_ref[...] for wbf, b_ref in zip(wbf_refs, b_refs)]
    o_ref[...] = epilogue(accs, [e[...] for e in e_refs]).astype(o_ref.dtype)


def _dense(x, ws, bs, layer, epilogue, extras, out_dtype, tm, tn, name):
    m, k = x.shape
    n = ws[0].shape[-1]
    in_specs = [pl.BlockSpec((tm, k), lambda j, i: (i, 0))]
    in_specs += [pl.BlockSpec((None, k, tn), lambda j, i: (layer, 0, j)) for _ in ws]
    in_specs += [pl.BlockSpec((None, 1, tn), lambda j, i: (layer, 0, j)) for _ in bs]
    in_specs += [spec for _, spec in extras]
    args = [x] + list(ws) + [b.reshape(b.shape[0], 1, n) for b in bs] + [a for a, _ in extras]
    return pl.pallas_call(
        functools.partial(_dense_kernel, n_w=len(ws), n_extra=len(extras), epilogue=epilogue),
        grid=(n // tn, m // tm),
        in_specs=in_specs,
        out_specs=pl.BlockSpec((tm, tn), lambda j, i: (i, j)),
        out_shape=jax.ShapeDtypeStruct((m, n), out_dtype),
        scratch_shapes=[pltpu.VMEM((k, tn), bf16) for _ in ws],
        compiler_params=_cparams(2), name=name,
    )(*args)


def _gated_residual_extras(x2d, mod3, gate_chunk, d, tm, tn, seq):
    tiles_per_batch = seq // tm
    return [
        (x2d, pl.BlockSpec((tm, tn), lambda j, i: (i, j))),
        (mod3, pl.BlockSpec((None, 1, tn), lambda j, i: (i // tiles_per_batch, 0, gate_chunk * (d // tn) + j))),
    ]


def _attn_kernel(sinks_ref, q_ref, kp_ref, kc_ref, vp_ref, vc_ref, qg_ref, kg_ref, bias_ref, o_ref):
    n = pl.program_id(1)
    lane = lax.broadcasted_iota(i32, (1, LANES), 1)
    lo = lane < HEAD_DIM
    bq = q_ref.shape[1]

    def halfnorm(v, gain):
        sq = v * v
        s_lo = jnp.sum(jnp.where(lo, sq, 0.0), axis=-1, keepdims=True)
        s_hi = jnp.sum(jnp.where(lo, 0.0, sq), axis=-1, keepdims=True)
        ms = jnp.where(lo, s_lo, s_hi) * (1.0 / HEAD_DIM)
        return v * lax.rsqrt(ms + NORM_EPS) * gain

    k_all = jnp.concatenate([kp_ref[0], kc_ref[0]], axis=0)
    v_all = jnp.concatenate([vp_ref[0], vc_ref[0]], axis=0)
    kcol = lax.broadcasted_iota(i32, (1, 2 * bq), 1)
    key_ok = jnp.logical_or(kcol >= bq, n > 0)
    qgain = qg_ref[...]
    kgain = kg_ref[...]
    pairs = Q_PER_KV // 2
    for c in range(N_KV_HEADS // 2):
        kn = halfnorm(k_all[:, c * LANES:(c + 1) * LANES], kgain)
        kr = pltpu.roll(kn, HEAD_DIM, 1)
        vn = v_all[:, c * LANES:(c + 1) * LANES]
        vr = pltpu.roll(vn, HEAD_DIM, 1)
        for half in range(2):
            g = 2 * c + half
            k_src, k_rot = (kn, kr) if half == 0 else (kr, kn)
            v_src, v_rot = (vn, vr) if half == 0 else (vr, vn)
            k_par = [jnp.where(lo, k_src, 0.0).astype(bf16), jnp.where(lo, 0.0, k_rot).astype(bf16)]
            v_par = [jnp.where(lo, v_src, 0.0).astype(bf16), jnp.where(lo, 0.0, v_rot).astype(bf16)]
            qs = [halfnorm(q_ref[0, :, (g * pairs + p) * LANES:(g * pairs + p + 1) * LANES], qgain)
                  for p in range(pairs)]
            qg = (jnp.concatenate(qs, axis=0) * (1.0 / math.sqrt(HEAD_DIM))).astype(bf16)
            acc = None
            for par in range(2):
                s = lax.dot_general(qg, k_par[par], (((1,), (1,)), ((), ())), preferred_element_type=f32)
                s = s + bias_ref[g, par]
                s = jnp.where(key_ok, s, -jnp.inf)
                sink = jnp.concatenate(
                    [jnp.full((bq, 1), sinks_ref[g * Q_PER_KV + 2 * p + par], f32) for p in range(pairs)], axis=0)
                m = jnp.maximum(jnp.max(s, axis=-1, keepdims=True), sink)
                e = jnp.exp(s - m)
                den = jnp.sum(e, axis=-1, keepdims=True) + jnp.exp(sink - m)
                probs = (e * (1.0 / den)).astype(bf16)
                o = jnp.dot(probs, v_par[par], preferred_element_type=f32)
                acc = o if acc is None else acc + o
            for p in range(pairs):
                col = (g * pairs + p) * LANES
                o_ref[0, :, col:col + LANES] = acc[p * bq:(p + 1) * bq].astype(o_ref.dtype)


def _t5_bucket(dist):
    nn = np.maximum(dist, 0)
    max_exact = NUM_BUCKETS // 2
    large = max_exact + (np.log(np.maximum(nn, 1) / max_exact) / np.log(MAX_DISTANCE / max_exact)
                         * (NUM_BUCKETS - max_exact)).astype(np.int32)
    large = np.minimum(large, NUM_BUCKETS - 1)
    return np.where(nn < max_exact, nn, large).astype(np.int32)


def _attention(qkv, q_gain, k_gain, sinks, rel_bias):
    bsz, seq, _ = qkv.shape
    bq = WINDOW
    q_dim = N_Q_HEADS * HEAD_DIM
    kv_dim = N_KV_HEADS * HEAD_DIM
    pairs = Q_PER_KV // 2
    ql = np.arange(bq)[:, None]
    kl = np.arange(2 * bq)[None, :]
    dist = ql + bq - kl
    in_window = (dist >= 0) & (dist < WINDOW)
    bias = jnp.take(rel_bias.astype(f32), jnp.asarray(_t5_bucket(dist)), axis=0)
    bias = jnp.where(jnp.asarray(in_window)[:, :, None], bias, -jnp.inf)
    bias = jnp.transpose(bias, (2, 0, 1)).reshape(N_KV_HEADS, pairs, 2, bq, 2 * bq)
    bias = jnp.transpose(bias, (0, 2, 1, 3, 4)).reshape(N_KV_HEADS, 2, pairs * bq, 2 * bq)
    gain2 = lambda gn: jnp.concatenate([gn, gn]).reshape(1, LANES).astype(f32)
    k_blk = q_dim // kv_dim
    grid_spec = pltpu.PrefetchScalarGridSpec(
        num_scalar_prefetch=1,
        grid=(bsz, seq // bq),
        in_specs=[
            pl.BlockSpec((1, bq, q_dim), lambda b, n, s: (b, n, 0)),
            pl.BlockSpec((1, bq, kv_dim), lambda b, n, s: (b, jnp.maximum(n - 1, 0), k_blk)),
            pl.BlockSpec((1, bq, kv_dim), lambda b, n, s: (b, n, k_blk)),
            pl.BlockSpec((1, bq, kv_dim), lambda b, n, s: (b, jnp.maximum(n - 1, 0), k_blk + 1)),
            pl.BlockSpec((1, bq, kv_dim), lambda b, n, s: (b, n, k_blk + 1)),
            pl.BlockSpec((1, LANES), lambda b, n, s: (0, 0)),
            pl.BlockSpec((1, LANES), lambda b, n, s: (0, 0)),
            pl.BlockSpec((N_KV_HEADS, 2, pairs * bq, 2 * bq), lambda b, n, s: (0, 0, 0, 0)),
        ],
        out_specs=pl.BlockSpec((1, bq, q_dim), lambda b, n, s: (b, n, 0)),
    )
    return pl.pallas_call(
        _attn_kernel, grid_spec=grid_spec,
        out_shape=jax.ShapeDtypeStruct((bsz, seq, q_dim), bf16),
        compiler_params=_cparams(2), name="swa_attention",
    )(sinks.astype(f32), qkv, qkv, qkv, qkv, qkv, gain2(q_gain), gain2(k_gain), bias)


def _gelu_tanh(y):
    return 0.5 * y * (1.0 + jnp.tanh(math.sqrt(2.0 / math.pi) * (y + 0.044715 * (y * y * y))))


def _ssm_kernel(u_ref, bm_ref, cm_ref, are_ref, aim_ref, d_ref, y_ref, us, xs, st, ys):
    bsz, ts, _ = u_ref.shape
    n_state = are_ref.shape[-1]

    @pl.when(pl.program_id(1) == 0)
    def _():
        st[...] = jnp.zeros_like(st)

    for b in range(bsz):
        us[pl.ds(b, ts, stride=bsz), :] = u_ref[b]
    u = us[...]
    xs[...] = jnp.dot(u.astype(bf16), bm_ref[0], preferred_element_type=f32)
    a_re = jnp.broadcast_to(are_ref[0], (bsz, n_state))
    a_im = jnp.broadcast_to(aim_ref[0], (bsz, n_state))

    def step(t, carry):
        s_re, s_im = carry
        r0 = pl.multiple_of(t * bsz, bsz)
        x_re = xs[pl.ds(r0, bsz), 0:n_state]
        x_im = xs[pl.ds(r0, bsz), n_state:2 * n_state]
        n_re = a_re * s_re - a_im * s_im + x_re
        n_im = a_re * s_im + a_im * s_re + x_im
        xs[pl.ds(r0, bsz), 0:n_state] = n_re
        xs[pl.ds(r0, bsz), n_state:2 * n_state] = n_im
        return n_re, n_im

    s_re, s_im = lax.fori_loop(0, ts, step, (st[0], st[1]), unroll=8)
    st[0] = s_re
    st[1] = s_im
    y = jnp.dot(xs[...].astype(bf16), cm_ref[0], preferred_element_type=f32) + d_ref[0] * u
    ys[...] = _gelu_tanh(y)
    for b in range(bsz):
        y_ref[b] = ys[pl.ds(b, ts, stride=bsz), :].astype(y_ref.dtype)


def _ssm(h, lam_re, lam_im, log_dt, b_re, b_im, c_re, c_im, d_skip):
    bsz, seq, width = h.shape
    assert bsz == SUBLANES
    n_grp, n_st = lam_re.shape
    gpt = LANES // SSM_GROUP_CH
    n_tiles = width // LANES
    ns = gpt * n_st
    dt = jnp.exp(log_dt.astype(f32))[:, None]
    lr, li = lam_re.astype(f32), lam_im.astype(f32)
    mag = jnp.exp(lr * dt)
    ab_re, ab_im = mag * jnp.cos(li * dt), mag * jnp.sin(li * dt)
    den = lr * lr + li * li
    nr, ni = ab_re - 1.0, ab_im
    f_re = (nr * lr + ni * li) / den
    f_im = (ni * lr - nr * li) / den
    br, bi = b_re.astype(f32), b_im.astype(f32)
    bb_re = f_re[..., None] * br - f_im[..., None] * bi
    bb_im = f_re[..., None] * bi + f_im[..., None] * br
    eye = jnp.eye(gpt, dtype=f32)

    def blockdiag_in(bb):
        t = bb.reshape(n_tiles, gpt, n_st, SSM_GROUP_CH)
        return jnp.einsum("tgpc,gh->tgchp", t, eye).reshape(n_tiles, LANES, ns)

    def blockdiag_out(cc):
        t = cc.reshape(n_tiles, gpt, SSM_GROUP_CH, n_st)
        return jnp.einsum("tgcp,gh->tgphc", t, eye).reshape(n_tiles, ns, LANES)

    bm = jnp.concatenate([blockdiag_in(bb_re), blockdiag_in(bb_im)], axis=-1).astype(bf16)
    cm = jnp.concatenate([blockdiag_out(c_re.astype(f32)), -blockdiag_out(c_im.astype(f32))], axis=1).astype(bf16)
    a_re = ab_re.reshape(n_tiles, 1, ns)
    a_im = ab_im.reshape(n_tiles, 1, ns)
    dsk = d_skip.astype(f32).reshape(n_tiles, 1, LANES)
    ts = 256
    return pl.pallas_call(
        _ssm_kernel,
        grid=(n_tiles, seq // ts),
        in_specs=[
            pl.BlockSpec((bsz, ts, LANES), lambda g, t: (0, t, g)),
            pl.BlockSpec((1, LANES, 2 * ns), lambda g, t: (g, 0, 0)),
            pl.BlockSpec((1, 2 * ns, LANES), lambda g, t: (g, 0, 0)),
            pl.BlockSpec((1, 1, ns), lambda g, t: (g, 0, 0)),
            pl.BlockSpec((1, 1, ns), lambda g, t: (g, 0, 0)),
            pl.BlockSpec((1, 1, LANES), lambda g, t: (g, 0, 0)),
        ],
        out_specs=pl.BlockSpec((bsz, ts, LANES), lambda g, t: (0, t, g)),
        out_shape=jax.ShapeDtypeStruct((bsz, seq, width), bf16),
        scratch_shapes=[
            pltpu.VMEM((ts * bsz, LANES), f32),
            pltpu.VMEM((ts * bsz, 2 * ns), f32),
            pltpu.VMEM((2, bsz, ns), f32),
            pltpu.VMEM((ts * bsz, LANES), f32),
        ],
        compiler_params=_cparams(2), name="s5_ssm",
    )(h, bm, cm, a_re, a_im, dsk)


GATHER_PITCH = 24


def _gather_kernel(valid_ref, idx_ref, idx_next_ref, h_hbm, o_ref, buf, sem):
    i = pl.program_id(0)
    n = pl.num_programs(0)
    rows, d = o_ref.shape
    n_col = d // LANES
    slot = lax.rem(i, 2)

    def issue(idx_blk, sl):
        def body(r, carry):
            src = pl.multiple_of(idx_blk[0, 0, r] * n_col, SUBLANES)
            dst = pl.multiple_of((sl * rows + r) * GATHER_PITCH, SUBLANES)
            pltpu.make_async_copy(h_hbm.at[pl.ds(src, n_col)], buf.at[pl.ds(dst, n_col)], sem.at[sl]).start()
            return carry

        lax.fori_loop(0, rows, body, 0, unroll=8)

    @pl.when(jnp.logical_and(i == 0, valid_ref[0] > 0))
    def _():
        issue(idx_ref, 0)

    @pl.when(jnp.logical_and(i + 1 < n, valid_ref[jnp.minimum(i + 1, n - 1)] > 0))
    def _():
        issue(idx_next_ref, 1 - slot)

    @pl.when(valid_ref[i] > 0)
    def _():
        base = pl.multiple_of(slot * (rows * GATHER_PITCH), SUBLANES)
        pltpu.make_async_copy(h_hbm.at[pl.ds(0, rows * n_col)], buf.at[pl.ds(base, rows * n_col)],
                              sem.at[slot]).wait()
        for j in range(n_col):
            piece = buf[pl.ds(base + j, rows, stride=GATHER_PITCH), :]
            o_ref[:, j * LANES:(j + 1) * LANES] = piece.astype(o_ref.dtype)

    @pl.when(valid_ref[i] == 0)
    def _():
        o_ref[...] = jnp.zeros_like(o_ref)


def _moe_gather(h3, n_col, row_tok, chunk_valid):
    n_rows = row_tok.shape[0]
    d = n_col * LANES
    n_chunks = n_rows // MOE_SUB
    idx = row_tok.reshape(n_chunks, 1, MOE_SUB)
    grid_spec = pltpu.PrefetchScalarGridSpec(
        num_scalar_prefetch=1,
        grid=(n_chunks,),
        in_specs=[
            pl.BlockSpec((1, 1, MOE_SUB), lambda i, v: (i, 0, 0), memory_space=pltpu.SMEM),
            pl.BlockSpec((1, 1, MOE_SUB), lambda i, v: (jnp.minimum(i + 1, n_chunks - 1), 0, 0),
                         memory_space=pltpu.SMEM),
            pl.BlockSpec(memory_space=pl.ANY),
        ],
        out_specs=pl.BlockSpec((MOE_SUB, d), lambda i, v: (i, 0)),
        scratch_shapes=[pltpu.VMEM((2 * MOE_SUB * GATHER_PITCH, LANES), f32), pltpu.SemaphoreType.DMA((2,))],
    )
    return pl.pallas_call(
        _gather_kernel, grid_spec=grid_spec,
        out_shape=jax.ShapeDtypeStruct((n_rows, d), bf16),
        compiler_params=_cparams(1), name="moe_gather",
    )(chunk_valid, idx, idx, h3)


def _swiglu(g, l):
    glu = jnp.minimum(g, SWIGLU_LIMIT)
    lin = jnp.clip(l, -SWIGLU_LIMIT, SWIGLU_LIMIT)
    return glu * jax.nn.sigmoid(SWIGLU_ALPHA * glu) * (lin + 1.0)


def _for_row_groups(nsub, fn):
    n_big = nsub // MOE_GROUP
    big_rows = MOE_GROUP * MOE_SUB

    def big(i, carry):
        fn(pl.multiple_of(i * big_rows, big_rows), big_rows)
        return carry

    def small(i, carry):
        fn(pl.multiple_of(i * MOE_SUB, MOE_SUB), MOE_SUB)
        return carry

    lax.fori_loop(0, n_big, big, 0)
    lax.fori_loop(n_big * MOE_GROUP, nsub, small, 0)


def _moe_kernel(sbe_ref, sbn_ref, sbsrc_ref, x_ref, wg_ref, wl_ref, bg_ref, bl_ref, wd_ref, bd_ref, o_ref,
                a_scr, wg_bf, wl_bf, wd_bf, *, n_chunks):
    s = pl.program_id(0)
    c = pl.program_id(1)
    nsub = sbn_ref[s]

    @pl.when(jnp.logical_and(c < n_chunks, nsub > 0))
    def _():
        wg_bf[...] = wg_ref[...].astype(bf16)
        wl_bf[...] = wl_ref[...].astype(bf16)

        def up(r, rows):
            xs = x_ref[pl.ds(r, rows), :]
            g = jnp.dot(xs, wg_bf[...], preferred_element_type=f32) + bg_ref[...]
            l = jnp.dot(xs, wl_bf[...], preferred_element_type=f32) + bl_ref[...]
            a_scr[c, pl.ds(r, rows), :] = _swiglu(g, l).astype(bf16)

        _for_row_groups(nsub, up)

    @pl.when(c >= n_chunks)
    def _():
        @pl.when(nsub > 0)
        def _():
            wd_bf[...] = wd_ref[...].astype(bf16)

        def down(r, rows):
            a = jnp.concatenate([a_scr[cc, pl.ds(r, rows), :] for cc in range(n_chunks)], axis=1)
            o_ref[pl.ds(r, rows), :] = jnp.dot(a, wd_bf[...], preferred_element_type=f32) + bd_ref[...]

        _for_row_groups(nsub, down)

        def zero(i, carry):
            r = pl.multiple_of(i * MOE_SUB, MOE_SUB)
            o_ref[pl.ds(r, MOE_SUB), :] = jnp.zeros((MOE_SUB, o_ref.shape[1]), f32)
            return carry

        lax.fori_loop(nsub, MOE_SB_SUBS, zero, 0)


def _moe_experts(x_sorted, sb_expert, sb_nsub, sb_src, w_gate_up, b_gate_up, w_down, b_down, layer):
    n_rows, d = x_sorted.shape
    n_sb = n_rows // MOE_SB_ROWS
    d_exp = w_down.shape[2]
    n_chunks = d_exp // MOE_CHUNK
    n_out_chunks = d // MOE_CHUNK
    assert n_out_chunks == n_chunks
    last = n_chunks - 1

    def valid(s, n):
        return n[s] > 0

    def wg_map(s, c, e, n, src):
        return (layer, e[s], 0, jnp.where(valid(s, n), jnp.minimum(c, last), last))

    def wl_map(s, c, e, n, src):
        return (layer, e[s], 0, n_chunks + jnp.where(valid(s, n), jnp.minimum(c, last), last))

    def wd_map(s, c, e, n, src):
        return (layer, e[s], 0, jnp.where(valid(s, n), jnp.maximum(c - n_chunks, 0), last))

    def out_map(s, c, e, n, src):
        return (s, jnp.maximum(c - n_chunks, 0))

    grid_spec = pltpu.PrefetchScalarGridSpec(
        num_scalar_prefetch=3,
        grid=(n_sb, n_chunks + n_out_chunks),
        in_specs=[
            pl.BlockSpec((MOE_SB_ROWS, d), lambda s, c, e, n, src: (src[s], 0)),
            pl.BlockSpec((None, None, d, MOE_CHUNK), wg_map),
            pl.BlockSpec((None, None, d, MOE_CHUNK), wl_map),
            pl.BlockSpec((None, None, 1, MOE_CHUNK), wg_map),
            pl.BlockSpec((None, None, 1, MOE_CHUNK), wl_map),
            pl.BlockSpec((None, None, d_exp, MOE_CHUNK), wd_map),
            pl.BlockSpec((None, None, 1, MOE_CHUNK), wd_map),
        ],
        out_specs=pl.BlockSpec((MOE_SB_ROWS, MOE_CHUNK), out_map),
        scratch_shapes=[
            pltpu.VMEM((n_chunks, MOE_SB_ROWS, MOE_CHUNK), bf16),
            pltpu.VMEM((d, MOE_CHUNK), bf16),
            pltpu.VMEM((d, MOE_CHUNK), bf16),
            pltpu.VMEM((d_exp, MOE_CHUNK), bf16),
        ],
    )
    n_l, n_e = b_gate_up.shape[:2]
    return pl.pallas_call(
        functools.partial(_moe_kernel, n_chunks=n_chunks), grid_spec=grid_spec,
        out_shape=jax.ShapeDtypeStruct((n_rows, d), f32),
        compiler_params=_cparams(2), name="moe_experts",
    )(sb_expert, sb_nsub, sb_src, x_sorted, w_gate_up, w_gate_up,
      b_gate_up.reshape(n_l, n_e, 1, -1), b_gate_up.reshape(n_l, n_e, 1, -1),
      w_down, b_down.reshape(n_l, n_e, 1, -1))


def _combine_kernel(dest_ref, dest_next_ref, y_hbm, gate_ref, x_ref, g2_ref, o_ref, ybuf, sem):
    i = pl.program_id(0)
    n = pl.num_programs(0)
    tm = x_ref.shape[0]
    slot = lax.rem(i, 2)

    def issue(dest_blk, sl):
        for k in range(TOP_K):
            def body(t, carry, k=k):
                pltpu.make_async_copy(y_hbm.at[pl.ds(dest_blk[0, 0, k * tm + t], 1)],
                                      ybuf.at[sl * TOP_K + k, pl.ds(t, 1)], sem.at[sl]).start()
                return carry

            lax.fori_loop(0, tm, body, 0, unroll=8)

    @pl.when(i == 0)
    def _():
        issue(dest_ref, 0)

    @pl.when(i + 1 < n)
    def _():
        issue(dest_next_ref, 1 - slot)

    for k in range(TOP_K):
        pltpu.make_async_copy(y_hbm.at[pl.ds(0, tm)], ybuf.at[slot * TOP_K + k], sem.at[slot]).wait()
    gates = gate_ref[...]
    acc = gates[:, 0:1] * ybuf[slot * TOP_K]
    for k in range(1, TOP_K):
        acc = acc + gates[:, k:k + 1] * ybuf[slot * TOP_K + k]
    o_ref[...] = x_ref[...] + g2_ref[...] * acc


def _moe_combine(y_sorted, dest, gates, x2d, mod3, gate_chunk, seq):
    n_tok, d = x2d.shape
    tm = 128
    n_tiles = n_tok // tm
    dest_km = jnp.transpose(dest.reshape(n_tiles, tm, TOP_K), (0, 2, 1)).reshape(n_tiles, 1, TOP_K * tm)
    tiles_per_batch = seq // tm
    return pl.pallas_call(
        _combine_kernel,
        grid=(n_tiles,),
        in_specs=[
            pl.BlockSpec((1, 1, TOP_K * tm), lambda i: (i, 0, 0), memory_space=pltpu.SMEM),
            pl.BlockSpec((1, 1, TOP_K * tm), lambda i: (jnp.minimum(i + 1, n_tiles - 1), 0, 0),
                         memory_space=pltpu.SMEM),
            pl.BlockSpec(memory_space=pl.ANY),
            pl.BlockSpec((tm, LANES), lambda i: (i, 0)),
            pl.BlockSpec((tm, d), lambda i: (i, 0)),
            pl.BlockSpec((None, 1, d), lambda i: (i // tiles_per_batch, 0, gate_chunk)),
        ],
        out_specs=pl.BlockSpec((tm, d), lambda i: (i, 0)),
        out_shape=jax.ShapeDtypeStruct((n_tok, d), f32),
        scratch_shapes=[pltpu.VMEM((2 * TOP_K, tm, d), f32), pltpu.SemaphoreType.DMA((2,))],
        compiler_params=_cparams(1), name="moe_combine",
    )(dest_km, dest_km, y_sorted, gates, x2d, mod3)


def _routing_tables(top_idx):
    n_tok = top_idx.shape[0]
    n_assign = n_tok * TOP_K
    n_sb_max = (n_assign // MOE_SUB + N_EXPERTS) // MOE_SB_SUBS + N_EXPERTS
    e_flat = top_idx.reshape(n_assign)
    onehot = (e_flat[:, None] == jnp.arange(N_EXPERTS, dtype=i32)[None, :]).astype(i32)
    csum = jnp.cumsum(onehot, axis=0)
    rank = jnp.sum(onehot * csum, axis=1) - 1
    counts = csum[-1]
    n_sub_e = (counts + MOE_SUB - 1) // MOE_SUB
    n_sb_e = (n_sub_e + MOE_SB_SUBS - 1) // MOE_SB_SUBS
    sb_end = jnp.cumsum(n_sb_e)
    sb_start = sb_end - n_sb_e
    base = n_sub_e // jnp.maximum(n_sb_e, 1)
    rem = n_sub_e - base * n_sb_e
    per_assign = lambda table: jnp.sum(onehot * table[None, :], axis=1)
    a_start, a_base, a_rem = per_assign(sb_start), per_assign(base), per_assign(rem)
    q = rank // MOE_SUB
    thr = a_rem * (a_base + 1)
    sb_local = jnp.where(q < thr, q // (a_base + 1), a_rem + (q - thr) // jnp.maximum(a_base, 1))
    sub_in = jnp.where(q < thr, q % (a_base + 1), (q - thr) % jnp.maximum(a_base, 1))
    dest = (a_start + sb_local) * MOE_SB_ROWS + sub_in * MOE_SUB + rank % MOE_SUB
    tok_flat = jnp.arange(n_assign, dtype=i32) // TOP_K
    row_tok = jnp.zeros((n_sb_max * MOE_SB_ROWS,), i32).at[dest].set(tok_flat)
    s = jnp.arange(n_sb_max, dtype=i32)
    n_sb_total = sb_end[-1]
    sb_valid = s < n_sb_total
    sb_src = jnp.where(sb_valid, s, n_sb_total - 1).astype(i32)
    sb_e = jnp.minimum(jnp.searchsorted(sb_end, sb_src, side="right"), N_EXPERTS - 1).astype(i32)
    sb_local_s = sb_src - sb_start[sb_e]
    sb_nsub = jnp.where(sb_valid, base[sb_e] + (sb_local_s < rem[sb_e]).astype(i32), 0).astype(i32)
    chunk_valid = (jnp.arange(MOE_SB_SUBS, dtype=i32)[None, :] < sb_nsub[:, None]).astype(i32).reshape(-1)
    return dest.reshape(n_tok, TOP_K).astype(i32), row_tok, chunk_valid, sb_e, sb_nsub, sb_src


def kernel(x, c, rel_bias, norm_gain, ada_w, ada_b, attn_w_qkv, attn_b_qkv, attn_q_gain, attn_k_gain, attn_sinks, attn_w_o, attn_b_o, ssm_lam_re, ssm_lam_im, ssm_log_dt, ssm_b_re, ssm_b_im, ssm_c_re, ssm_c_im, ssm_d, ssm_w_glu_a, ssm_b_glu_a, ssm_w_glu_b, ssm_b_glu_b, moe_w_router, moe_b_router, moe_w_gate_up, moe_b_gate_up, moe_w_down, moe_b_down):
    bsz, seq, d = x.shape
    n_tok = bsz * seq
    depth = norm_gain.shape[0]
    tm, tn = 1024, 512
    SH1, SC1, G1, SH2, SC2, G2 = range(6)

    def gated(accs, extras):
        x_res, gate = extras
        return x_res + gate * accs[0]

    def gated_glu(accs, extras):
        x_res, gate = extras
        return x_res + gate * (accs[0] * jax.nn.sigmoid(accs[1]))

    for layer in range(depth):
        mod3 = _modulation(c, ada_w, ada_b, layer).reshape(bsz, 1, 6 * d)
        x2d = x.reshape(n_tok, d)
        res_extras = _gated_residual_extras(x2d, mod3, G1, d, tm, tn, seq)
        i = layer // 2
        if layer % 2 == 0:
            h = _norm(x, norm_gain[layer, 0], mod3, SC1, SH1, bf16)
            qkv = _dense(h.reshape(n_tok, d), [attn_w_qkv], [attn_b_qkv], i, lambda accs, extras: accs[0], [],
                         f32, tm, 512, "qkv_proj")
            o = _attention(qkv.reshape(bsz, seq, -1), attn_q_gain[i], attn_k_gain[i], attn_sinks[i], rel_bias)
            x2d = _dense(o.reshape(n_tok, -1), [attn_w_o], [attn_b_o], i, gated, res_extras, f32, tm, tn, "attn_out")
        else:
            h = _norm(x, norm_gain[layer, 0], mod3, SC1, SH1, f32)
            y = _ssm(h, ssm_lam_re[i], ssm_lam_im[i], ssm_log_dt[i], ssm_b_re[i], ssm_b_im[i],
                     ssm_c_re[i], ssm_c_im[i], ssm_d[i])
            x2d = _dense(y.reshape(n_tok, d), [ssm_w_glu_a, ssm_w_glu_b], [ssm_b_glu_a, ssm_b_glu_b], i,
                         gated_glu, res_extras, f32, tm, tn, "ssm_glu")
        x = x2d.reshape(bsz, seq, d)
        h, top_idx, gates = _norm(x, norm_gain[layer, 1], mod3, SC2, SH2, f32,
                                  router=(moe_w_router, moe_b_router, layer))
        top_idx = top_idx.reshape(n_tok, LANES)[:, :TOP_K]
        dest, row_tok, chunk_valid, sb_e, sb_nsub, sb_src = _routing_tables(top_idx)
        x_sorted = _moe_gather(h, d // LANES, row_tok, chunk_valid)
        y_sorted = _moe_experts(x_sorted, sb_e, sb_nsub, sb_src, moe_w_gate_up, moe_b_gate_up,
                                moe_w_down, moe_b_down, layer)
        x = _moe_combine(y_sorted, dest, gates.reshape(n_tok, LANES), x2d, mod3, G2, seq).reshape(bsz, seq, d)
    return x
```

```python
import functools
import math

import numpy as np
import jax
import jax.numpy as jnp
from jax import lax
from jax.experimental import pallas as pl
from jax.experimental.pallas import tpu as pltpu

f32 = jnp.float32
bf16 = jnp.bfloat16
i32 = jnp.int32

N_Q_HEADS = 32
N_KV_HEADS = 4
HEAD_DIM = 64
Q_PER_KV = N_Q_HEADS // N_KV_HEADS
WINDOW = 128
NUM_BUCKETS = 32
MAX_DISTANCE = 128
SSM_GROUP_CH = 16
SSM_STATE = 64
N_EXPERTS = 32
TOP_K = 4
SWIGLU_ALPHA = 1.702
SWIGLU_LIMIT = 7.0
NORM_EPS = 1e-5

LANES = 128
SUBLANES = 8
VMEM_LIMIT = 56 * 1024 * 1024

MOE_SUB = 256
MOE_SB_SUBS = 8
MOE_SB_ROWS = MOE_SUB * MOE_SB_SUBS
MOE_GROUP = 4
MOE_CHUNK = 256


def _cparams(n_axes):
    return pltpu.CompilerParams(dimension_semantics=("arbitrary",) * n_axes, vmem_limit_bytes=VMEM_LIMIT)


def _mod_kernel(c_ref, w_ref, b_ref, o_ref):
    c = c_ref[...]
    cond = c * jax.nn.sigmoid(c)
    o_ref[...] = jnp.dot(cond.astype(bf16), w_ref[...].astype(bf16), preferred_element_type=f32) + b_ref[...]


def _modulation(c, ada_w, ada_b, layer):
    bsz, d = c.shape
    n = ada_w.shape[-1]
    tn = 1024
    return pl.pallas_call(
        _mod_kernel,
        grid=(n // tn,),
        in_specs=[
            pl.BlockSpec((bsz, d), lambda j: (0, 0)),
            pl.BlockSpec((None, d, tn), lambda j: (layer, 0, j)),
            pl.BlockSpec((None, 1, tn), lambda j: (layer, 0, j)),
        ],
        out_specs=pl.BlockSpec((bsz, tn), lambda j: (0, j)),
        out_shape=jax.ShapeDtypeStruct((bsz, n), f32),
        compiler_params=_cparams(1),
        name="adaln_mod",
    )(c, ada_w, ada_b.reshape(ada_b.shape[0], 1, n))


def _norm_mod(x_ref, gain_ref, sc_ref, sh_ref):
    x = x_ref[0]
    ms = jnp.mean(x * x, axis=-1, keepdims=True)
    y = x * lax.rsqrt(ms + NORM_EPS) * gain_ref[...]
    return y * (1.0 + sc_ref[0]) + sh_ref[0]


def _norm_kernel(x_ref, gain_ref, sc_ref, sh_ref, h_ref):
    h_ref[0] = _norm_mod(x_ref, gain_ref, sc_ref, sh_ref).astype(h_ref.dtype)


def _split_bf16(v):
    hi = v.astype(bf16)
    lo = (v - hi.astype(f32)).astype(bf16)
    return hi, lo


def _norm_router_kernel(x_ref, gain_ref, sc_ref, sh_ref, wr_ref, br_ref, h_ref, idx_ref, gate_ref):
    h = _norm_mod(x_ref, gain_ref, sc_ref, sh_ref)
    ts, d = h.shape
    n_col = d // LANES
    for j in range(n_col):
        h_ref[pl.ds(j, ts, stride=n_col), :] = h[:, j * LANES:(j + 1) * LANES]
    h_hi, h_lo = _split_bf16(h)
    w_hi, w_lo = _split_bf16(wr_ref[...])
    dot = functools.partial(jnp.dot, preferred_element_type=f32)
    logits = dot(h_hi, w_hi) + (dot(h_hi, w_lo) + dot(h_lo, w_hi)) + br_ref[...]
    rows, n_exp = logits.shape
    col = lax.broadcasted_iota(i32, (rows, n_exp), 1)
    work = logits
    tops, idxs = [], []
    for _ in range(TOP_K):
        m = jnp.max(work, axis=-1, keepdims=True)
        idx = jnp.min(jnp.where(work == m, col, n_exp), axis=-1, keepdims=True)
        work = jnp.where(col == idx, -jnp.inf, work)
        tops.append(m)
        idxs.append(idx)
    es = [jnp.exp(t - tops[0]) for t in tops]
    denom = es[0] + es[1] + es[2] + es[3]
    lane = lax.broadcasted_iota(i32, (rows, LANES), 1)
    idx_out = jnp.zeros((rows, LANES), i32)
    gate_out = jnp.zeros((rows, LANES), f32)
    for k in range(TOP_K):
        idx_out = jnp.where(lane == k, idxs[k], idx_out)
        gate_out = jnp.where(lane == k, es[k] / denom, gate_out)
    idx_ref[0] = idx_out
    gate_ref[0] = gate_out


def _norm(x, gain, mod3, sc_chunk, sh_chunk, out_dtype, router=None):
    bsz, seq, d = x.shape
    ts = 512
    in_specs = [
        pl.BlockSpec((1, ts, d), lambda b, s: (b, s, 0)),
        pl.BlockSpec((1, d), lambda b, s: (0, 0)),
        pl.BlockSpec((1, 1, d), lambda b, s: (b, 0, sc_chunk)),
        pl.BlockSpec((1, 1, d), lambda b, s: (b, 0, sh_chunk)),
    ]
    h_spec = pl.BlockSpec((1, ts, d), lambda b, s: (b, s, 0))
    args = [x, gain.reshape(1, d), mod3, mod3]
    if router is None:
        return pl.pallas_call(
            _norm_kernel, grid=(bsz, seq // ts), in_specs=in_specs, out_specs=h_spec,
            out_shape=jax.ShapeDtypeStruct((bsz, seq, d), out_dtype),
            compiler_params=_cparams(2), name="norm_mod",
        )(*args)
    w_router, b_router, layer = router
    n_exp = w_router.shape[-1]
    in_specs += [
        pl.BlockSpec((None, d, n_exp), lambda b, s: (layer, 0, 0)),
        pl.BlockSpec((None, 1, n_exp), lambda b, s: (layer, 0, 0)),
    ]
    lane_spec = pl.BlockSpec((1, ts, LANES), lambda b, s: (b, s, 0))
    n_col = d // LANES
    h3_spec = pl.BlockSpec((ts * n_col, LANES), lambda b, s: (b * (seq // ts) + s, 0))
    return pl.pallas_call(
        _norm_router_kernel, grid=(bsz, seq // ts), in_specs=in_specs,
        out_specs=[h3_spec, lane_spec, lane_spec],
        out_shape=[jax.ShapeDtypeStruct((bsz * seq * n_col, LANES), f32),
                   jax.ShapeDtypeStruct((bsz, seq, LANES), i32),
                   jax.ShapeDtypeStruct((bsz, seq, LANES), f32)],
        compiler_params=_cparams(2), name="norm_router",
    )(*args, w_router, b_router.reshape(b_router.shape[0], 1, n_exp))


def _dense_kernel(*refs, n_w, n_extra, epilogue):
    x_ref = refs[0]
    w_refs = refs[1:1 + n_w]
    b_refs = refs[1 + n_w:1 + 2 * n_w]
    e_refs = refs[1 + 2 * n_w:1 + 2 * n_w + n_extra]
    o_ref = refs[1 + 2 * n_w + n_extra]
    wbf_refs = refs[2 + 2 * n_w + n_extra:]

    @pl.when(pl.program_id(1) == 0)
    def _():
        for w_ref, wbf in zip(w_refs, wbf_refs):
            wbf[...] = w_ref[...].astype(bf16)

    x = x_ref[...]
    accs = [jnp.dot(x, wbf[...], preferred_element_type=f32) + b_ref[...] for wbf, b_ref in zip(wbf_refs, b_refs)]
    o_ref[...] = epilogue(accs, [e[...] for e in e_refs]).astype(o_ref.dtype)


def _dense(x, ws, bs, layer, epilogue, extras, out_dtype, tm, tn, name):
    m, k = x.shape
    n = ws[0].shape[-1]
    in_specs = [pl.BlockSpec((tm, k), lambda j, i: (i, 0))]
    in_specs += [pl.BlockSpec((None, k, tn), lambda j, i: (layer, 0, j)) for _ in ws]
    in_specs += [pl.BlockSpec((None, 1, tn), lambda j, i: (layer, 0, j)) for _ in bs]
    in_specs += [spec for _, spec in extras]
    args = [x] + list(ws) + [b.reshape(b.shape[0], 1, n) for b in bs] + [a for a, _ in extras]
    return pl.pallas_call(
        functools.partial(_dense_kernel, n_w=len(ws), n_extra=len(extras), epilogue=epilogue),
        grid=(n // tn, m // tm),
        in_specs=in_specs,
        out_specs=pl.BlockSpec((tm, tn), lambda j, i: (i, j)),
        out_shape=jax.ShapeDtypeStruct((m, n), out_dtype),
        scratch_shapes=[pltpu.VMEM((k, tn), bf16) for _ in ws],
        compiler_params=_cparams(2), name=name,
    )(*args)


def _gated_residual_extras(x2d, mod3, gate_chunk, d, tm, tn, seq):
    tiles_per_batch = seq // tm
    return [
        (x2d, pl.BlockSpec((tm, tn), lambda j, i: (i, j))),
        (mod3, pl.BlockSpec((None, 1, tn), lambda j, i: (i // tiles_per_batch, 0, gate_chunk * (d // tn) + j))),
    ]


def _attn_kernel(sinks_ref, q_ref, kp_ref, kc_ref, vp_ref, vc_ref, qg_ref, kg_ref, bias_ref, o_ref):
    n = pl.program_id(1)
    lane = lax.broadcasted_iota(i32, (1, LANES), 1)
    lo = lane < HEAD_DIM
    bq = q_ref.shape[1]

    def halfnorm(v, gain):
        sq = v * v
        s_lo = jnp.sum(jnp.where(lo, sq, 0.0), axis=-1, keepdims=True)
        s_hi = jnp.sum(jnp.where(lo, 0.0, sq), axis=-1, keepdims=True)
        ms = jnp.where(lo, s_lo, s_hi) * (1.0 / HEAD_DIM)
        return v * lax.rsqrt(ms + NORM_EPS) * gain

    k_all = jnp.concatenate([kp_ref[0], kc_ref[0]], axis=0)
    v_all = jnp.concatenate([vp_ref[0], vc_ref[0]], axis=0)
    kcol = lax.broadcasted_iota(i32, (1, 2 * bq), 1)
    key_ok = jnp.logical_or(kcol >= bq, n > 0)
    qgain = qg_ref[...]
    kgain = kg_ref[...]
    pairs = Q_PER_KV // 2
    for c in range(N_KV_HEADS // 2):
        kn = halfnorm(k_all[:, c * LANES:(c + 1) * LANES], kgain)
        kr = pltpu.roll(kn, HEAD_DIM, 1)
        vn = v_all[:, c * LANES:(c + 1) * LANES]
        vr = pltpu.roll(vn, HEAD_DIM, 1)
        for half in range(2):
            g = 2 * c + half
            k_src, k_rot = (kn, kr) if half == 0 else (kr, kn)
            v_src, v_rot = (vn, vr) if half == 0 else (vr, vn)
            k_par = [jnp.where(lo, k_src, 0.0).astype(bf16), jnp.where(lo, 0.0, k_rot).astype(bf16)]
            v_par = [jnp.where(lo, v_src, 0.0).astype(bf16), jnp.where(lo, 0.0, v_rot).astype(bf16)]
            qs = [halfnorm(q_ref[0, :, (g * pairs + p) * LANES:(g * pairs + p + 1) * LANES], qgain)
                  for p in range(pairs)]
            qg = (jnp.concatenate(qs, axis=0) * (1.0 / math.sqrt(HEAD_DIM))).astype(bf16)
            acc = None
            for par in range(2):
                s = lax.dot_general(qg, k_par[par], (((1,), (1,)), ((), ())), preferred_element_type=f32)
                s = s + bias_ref[g, par]
                s = jnp.where(key_ok, s, -jnp.inf)
                sink = jnp.concatenate(
                    [jnp.full((bq, 1), sinks_ref[g * Q_PER_KV + 2 * p + par], f32) for p in range(pairs)], axis=0)
                m = jnp.maximum(jnp.max(s, axis=-1, keepdims=True), sink)
                e = jnp.exp(s - m)
                den = jnp.sum(e, axis=-1, keepdims=True) + jnp.exp(sink - m)
                probs = (e * (1.0 / den)).astype(bf16)
                o = jnp.dot(probs, v_par[par], preferred_element_type=f32)
                acc = o if acc is None else acc + o
            for p in range(pairs):
                col = (g * pairs + p) * LANES
                o_ref[0, :, col:col + LANES] = acc[p * bq:(p + 1) * bq].astype(o_ref.dtype)


def _t5_bucket(dist):
    nn = np.maximum(dist, 0)
    max_exact = NUM_BUCKETS // 2
    large = max_exact + (np.log(np.maximum(nn, 1) / max_exact) / np.log(MAX_DISTANCE / max_exact)
                         * (NUM_BUCKETS - max_exact)).astype(np.int32)
    large = np.minimum(large, NUM_BUCKETS - 1)
    return np.where(nn < max_exact, nn, large).astype(np.int32)


def _attention(qkv, q_gain, k_gain, sinks, rel_bias):
    bsz, seq, _ = qkv.shape
    bq = WINDOW
    q_dim = N_Q_HEADS * HEAD_DIM
    kv_dim = N_KV_HEADS * HEAD_DIM
    pairs = Q_PER_KV // 2
    ql = np.arange(bq)[:, None]
    kl = np.arange(2 * bq)[None, :]
    dist = ql + bq - kl
    in_window = (dist >= 0) & (dist < WINDOW)
    bias = jnp.take(rel_bias.astype(f32), jnp.asarray(_t5_bucket(dist)), axis=0)
    bias = jnp.where(jnp.asarray(in_window)[:, :, None], bias, -jnp.inf)
    bias = jnp.transpose(bias, (2, 0, 1)).reshape(N_KV_HEADS, pairs, 2, bq, 2 * bq)
    bias = jnp.transpose(bias, (0, 2, 1, 3, 4)).reshape(N_KV_HEADS, 2, pairs * bq, 2 * bq)
    gain2 = lambda gn: jnp.concatenate([gn, gn]).reshape(1, LANES).astype(f32)
    k_blk = q_dim // kv_dim
    grid_spec = pltpu.PrefetchScalarGridSpec(
        num_scalar_prefetch=1,
        grid=(bsz, seq // bq),
        in_specs=[
            pl.BlockSpec((1, bq, q_dim), lambda b, n, s: (b, n, 0)),
            pl.BlockSpec((1, bq, kv_dim), lambda b, n, s: (b, jnp.maximum(n - 1, 0), k_blk)),
            pl.BlockSpec((1, bq, kv_dim), lambda b, n, s: (b, n, k_blk)),
            pl.BlockSpec((1, bq, kv_dim), lambda b, n, s: (b, jnp.maximum(n - 1, 0), k_blk + 1)),
            pl.BlockSpec((1, bq, kv_dim), lambda b, n, s: (b, n, k_blk + 1)),
            pl.BlockSpec((1, LANES), lambda b, n, s: (0, 0)),
            pl.BlockSpec((1, LANES), lambda b, n, s: (0, 0)),
            pl.BlockSpec((N_KV_HEADS, 2, pairs * bq, 2 * bq), lambda b, n, s: (0, 0, 0, 0)),
        ],
        out_specs=pl.BlockSpec((1, bq, q_dim), lambda b, n, s: (b, n, 0)),
    )
    return pl.pallas_call(
        _attn_kernel, grid_spec=grid_spec,
        out_shape=jax.ShapeDtypeStruct((bsz, seq, q_dim), bf16),
        compiler_params=_cparams(2), name="swa_attention",
    )(sinks.astype(f32), qkv, qkv, qkv, qkv, qkv, gain2(q_gain), gain2(k_gain), bias)


def _gelu_tanh(y):
    return 0.5 * y * (1.0 + jnp.tanh(math.sqrt(2.0 / math.pi) * (y + 0.044715 * (y * y * y))))


def _ssm_kernel(u_ref, bm_ref, cm_ref, are_ref, aim_ref, d_ref, y_ref, us, xs, st, ys):
    bsz, ts, _ = u_ref.shape
    n_state = are_ref.shape[-1]

    @pl.when(pl.program_id(1) == 0)
    def _():
        st[...] = jnp.zeros_like(st)

    for b in range(bsz):
        us[pl.ds(b, ts, stride=bsz), :] = u_ref[b]
    u = us[...]
    xs[...] = jnp.dot(u.astype(bf16), bm_ref[0], preferred_element_type=f32)
    a_re = jnp.broadcast_to(are_ref[0], (bsz, n_state))
    a_im = jnp.broadcast_to(aim_ref[0], (bsz, n_state))

    def step(t, carry):
        s_re, s_im = carry
        r0 = pl.multiple_of(t * bsz, bsz)
        x_re = xs[pl.ds(r0, bsz), 0:n_state]
        x_im = xs[pl.ds(r0, bsz), n_state:2 * n_state]
        n_re = a_re * s_re - a_im * s_im + x_re
        n_im = a_re * s_im + a_im * s_re + x_im
        xs[pl.ds(r0, bsz), 0:n_state] = n_re
        xs[pl.ds(r0, bsz), n_state:2 * n_state] = n_im
        return n_re, n_im

    s_re, s_im = lax.fori_loop(0, ts, step, (st[0], st[1]), unroll=8)
    st[0] = s_re
    st[1] = s_im
    y = jnp.dot(xs[...].astype(bf16), cm_ref[0], preferred_element_type=f32) + d_ref[0] * u
    ys[...] = _gelu_tanh(y)
    for b in range(bsz):
        y_ref[b] = ys[pl.ds(b, ts, stride=bsz), :].astype(y_ref.dtype)


def _ssm(h, lam_re, lam_im, log_dt, b_re, b_im, c_re, c_im, d_skip):
    bsz, seq, width = h.shape
    assert bsz == SUBLANES
    n_grp, n_st = lam_re.shape
    gpt = LANES // SSM_GROUP_CH
    n_tiles = width // LANES
    ns = gpt * n_st
    dt = jnp.exp(log_dt.astype(f32))[:, None]
    lr, li = lam_re.astype(f32), lam_im.astype(f32)
    mag = jnp.exp(lr * dt)
    ab_re, ab_im = mag * jnp.cos(li * dt), mag * jnp.sin(li * dt)
    den = lr * lr + li * li
    nr, ni = ab_re - 1.0, ab_im
    f_re = (nr * lr + ni * li) / den
    f_im = (ni * lr - nr * li) / den
    br, bi = b_re.astype(f32), b_im.astype(f32)
    bb_re = f_re[..., None] * br - f_im[..., None] * bi
    bb_im = f_re[..., None] * bi + f_im[..., None] * br
    eye = jnp.eye(gpt, dtype=f32)

    def blockdiag_in(bb):
        t = bb.reshape(n_tiles, gpt, n_st, SSM_GROUP_CH)
        return jnp.einsum("tgpc,gh->tgchp", t, eye).reshape(n_tiles, LANES, ns)

    def blockdiag_out(cc):
        t = cc.reshape(n_tiles, gpt, SSM_GROUP_CH, n_st)
        return jnp.einsum("tgcp,gh->tgphc", t, eye).reshape(n_tiles, ns, LANES)

    bm = jnp.concatenate([blockdiag_in(bb_re), blockdiag_in(bb_im)], axis=-1).astype(bf16)
    cm = jnp.concatenate([blockdiag_out(c_re.astype(f32)), -blockdiag_out(c_im.astype(f32))], axis=1).astype(bf16)
    a_re = ab_re.reshape(n_tiles, 1, ns)
    a_im = ab_im.reshape(n_tiles, 1, ns)
    dsk = d_skip.astype(f32).reshape(n_tiles, 1, LANES)
    ts = 256
    return pl.pallas_call(
        _ssm_kernel,
        grid=(n_tiles, seq // ts),
        in_specs=[
            pl.BlockSpec((bsz, ts, LANES), lambda g, t: (0, t, g)),
            pl.BlockSpec((1, LANES, 2 * ns), lambda g, t: (g, 0, 0)),
            pl.BlockSpec((1, 2 * ns, LANES), lambda g, t: (g, 0, 0)),
            pl.BlockSpec((1, 1, ns), lambda g, t: (g, 0, 0)),
            pl.BlockSpec((1, 1, ns), lambda g, t: (g, 0, 0)),
            pl.BlockSpec((1, 1, LANES), lambda g, t: (g, 0, 0)),
        ],
        out_specs=pl.BlockSpec((bsz, ts, LANES), lambda g, t: (0, t, g)),
        out_shape=jax.ShapeDtypeStruct((bsz, seq, width), bf16),
        scratch_shapes=[
            pltpu.VMEM((ts * bsz, LANES), f32),
            pltpu.VMEM((ts * bsz, 2 * ns), f32),
            pltpu.VMEM((2, bsz, ns), f32),
            pltpu.VMEM((ts * bsz, LANES), f32),
        ],
        compiler_params=_cparams(2), name="s5_ssm",
    )(h, bm, cm, a_re, a_im, dsk)


GATHER_PITCH = 24


def _gather_kernel(valid_ref, idx_ref, idx_next_ref, h_hbm, o_ref, buf, sem):
    i = pl.program_id(0)
    n = pl.num_programs(0)
    rows, d = o_ref.shape
    n_col = d // LANES
    slot = lax.rem(i, 2)

    def issue(idx_blk, sl):
        def body(r, carry):
            src = pl.multiple_of(idx_blk[0, 0, r] * n_col, SUBLANES)
            dst = pl.multiple_of((sl * rows + r) * GATHER_PITCH, SUBLANES)
            pltpu.make_async_copy(h_hbm.at[pl.ds(src, n_col)], buf.at[pl.ds(dst, n_col)], sem.at[sl]).start()
            return carry

        lax.fori_loop(0, rows, body, 0, unroll=8)

    @pl.when(jnp.logical_and(i == 0, valid_ref[0] > 0))
    def _():
        issue(idx_ref, 0)

    @pl.when(jnp.logical_and(i + 1 < n, valid_ref[jnp.minimum(i + 1, n - 1)] > 0))
    def _():
        issue(idx_next_ref, 1 - slot)

    @pl.when(valid_ref[i] > 0)
    def _():
        base = pl.multiple_of(slot * (rows * GATHER_PITCH), SUBLANES)
        pltpu.make_async_copy(h_hbm.at[pl.ds(0, rows * n_col)], buf.at[pl.ds(base, rows * n_col)],
                              sem.at[slot]).wait()
        for j in range(n_col):
            piece = buf[pl.ds(base + j, rows, stride=GATHER_PITCH), :]
            o_ref[:, j * LANES:(j + 1) * LANES] = piece.astype(o_ref.dtype)

    @pl.when(valid_ref[i] == 0)
    def _():
        o_ref[...] = jnp.zeros_like(o_ref)


def _moe_gather(h3, n_col, row_tok, chunk_valid):
    n_rows = row_tok.shape[0]
    d = n_col * LANES
    n_chunks = n_rows // MOE_SUB
    idx = row_tok.reshape(n_chunks, 1, MOE_SUB)
    grid_spec = pltpu.PrefetchScalarGridSpec(
        num_scalar_prefetch=1,
        grid=(n_chunks,),
        in_specs=[
            pl.BlockSpec((1, 1, MOE_SUB), lambda i, v: (i, 0, 0), memory_space=pltpu.SMEM),
            pl.BlockSpec((1, 1, MOE_SUB), lambda i, v: (jnp.minimum(i + 1, n_chunks - 1), 0, 0),
                         memory_space=pltpu.SMEM),
            pl.BlockSpec(memory_space=pl.ANY),
        ],
        out_specs=pl.BlockSpec((MOE_SUB, d), lambda i, v: (i, 0)),
        scratch_shapes=[pltpu.VMEM((2 * MOE_SUB * GATHER_PITCH, LANES), f32), pltpu.SemaphoreType.DMA((2,))],
    )
    return pl.pallas_call(
        _gather_kernel, grid_spec=grid_spec,
        out_shape=jax.ShapeDtypeStruct((n_rows, d), bf16),
        compiler_params=_cparams(1), name="moe_gather",
    )(chunk_valid, idx, idx, h3)


def _swiglu(g, l):
    glu = jnp.minimum(g, SWIGLU_LIMIT)
    lin = jnp.clip(l, -SWIGLU_LIMIT, SWIGLU_LIMIT)
    return glu * jax.nn.sigmoid(SWIGLU_ALPHA * glu) * (lin + 1.0)


def _for_row_groups(nsub, cast, fn):
    n_big = nsub // MOE_GROUP
    big_rows = MOE_GROUP * MOE_SUB

    @pl.when(n_big > 0)
    def _():
        fn(0, big_rows, cast())

    @pl.when(n_big == 0)
    def _():
        fn(0, MOE_SUB, cast())

    def big(i, carry):
        fn(pl.multiple_of(i * big_rows, big_rows), big_rows, None)
        return carry

    def small(i, carry):
        fn(pl.multiple_of(i * MOE_SUB, MOE_SUB), MOE_SUB, None)
        return carry

    lax.fori_loop(1, n_big, big, 0)
    lax.fori_loop(jnp.maximum(n_big * MOE_GROUP, 1), nsub, small, 0)


def _moe_kernel(sbe_ref, sbn_ref, sbsrc_ref, x_ref, wg_ref, wl_ref, bg_ref, bl_ref, wd_ref, bd_ref, o_ref,
                a_scr, wg_bf, wl_bf, wd_bf, *, n_chunks):
    s = pl.program_id(0)
    c = pl.program_id(1)
    nsub = sbn_ref[s]

    @pl.when(jnp.logical_and(c < n_chunks, nsub > 0))
    def _():
        def cast():
            wg, wl = wg_ref[...].astype(bf16), wl_ref[...].astype(bf16)
            wg_bf[...] = wg
            wl_bf[...] = wl
            return wg, wl

        def up(r, rows, weights):
            wg, wl = weights if weights is not None else (wg_bf[...], wl_bf[...])
            xs = x_ref[pl.ds(r, rows), :]
            g = jnp.dot(xs, wg, preferred_element_type=f32) + bg_ref[...]
            l = jnp.dot(xs, wl, preferred_element_type=f32) + bl_ref[...]
            a_scr[c, pl.ds(r, rows), :] = _swiglu(g, l).astype(bf16)

        _for_row_groups(nsub, cast, up)

    @pl.when(jnp.logical_and(c >= n_chunks, nsub > 0))
    def _():
        def cast():
            wd = wd_ref[...].astype(bf16)
            wd_bf[...] = wd
            return wd

        def down(r, rows, weights):
            wd = weights if weights is not None else wd_bf[...]
            a = jnp.concatenate([a_scr[cc, pl.ds(r, rows), :] for cc in range(n_chunks)], axis=1)
            o_ref[pl.ds(r, rows), :] = jnp.dot(a, wd, preferred_element_type=f32) + bd_ref[...]

        _for_row_groups(nsub, cast, down)

    @pl.when(c >= n_chunks)
    def _():
        def zero(i, carry):
            r = pl.multiple_of(i * MOE_SUB, MOE_SUB)
            o_ref[pl.ds(r, MOE_SUB), :] = jnp.zeros((MOE_SUB, o_ref.shape[1]), f32)
            return carry

        lax.fori_loop(nsub, MOE_SB_SUBS, zero, 0)


def _moe_experts(x_sorted, sb_expert, sb_nsub, sb_src, w_gate_up, b_gate_up, w_down, b_down, layer):
    n_rows, d = x_sorted.shape
    n_sb = n_rows // MOE_SB_ROWS
    d_exp = w_down.shape[2]
    n_chunks = d_exp // MOE_CHUNK
    n_out_chunks = d // MOE_CHUNK
    assert n_out_chunks == n_chunks
    last = n_chunks - 1

    def valid(s, n):
        return n[s] > 0

    def wg_map(s, c, e, n, src):
        return (layer, e[s], 0, jnp.where(valid(s, n), jnp.minimum(c, last), last))

    def wl_map(s, c, e, n, src):
        return (layer, e[s], 0, n_chunks + jnp.where(valid(s, n), jnp.minimum(c, last), last))

    def wd_map(s, c, e, n, src):
        return (layer, e[s], 0, jnp.where(valid(s, n), jnp.maximum(c - n_chunks, 0), last))

    def out_map(s, c, e, n, src):
        return (s, jnp.maximum(c - n_chunks, 0))

    grid_spec = pltpu.PrefetchScalarGridSpec(
        num_scalar_prefetch=3,
        grid=(n_sb, n_chunks + n_out_chunks),
        in_specs=[
            pl.BlockSpec((MOE_SB_ROWS, d), lambda s, c, e, n, src: (src[s], 0)),
            pl.BlockSpec((None, None, d, MOE_CHUNK), wg_map),
            pl.BlockSpec((None, None, d, MOE_CHUNK), wl_map),
            pl.BlockSpec((None, None, 1, MOE_CHUNK), wg_map),
            pl.BlockSpec((None, None, 1, MOE_CHUNK), wl_map),
            pl.BlockSpec((None, None, d_exp, MOE_CHUNK), wd_map),
            pl.BlockSpec((None, None, 1, MOE_CHUNK), wd_map),
        ],
        out_specs=pl.BlockSpec((MOE_SB_ROWS, MOE_CHUNK), out_map),
        scratch_shapes=[
            pltpu.VMEM((n_chunks, MOE_SB_ROWS, MOE_CHUNK), bf16),
            pltpu.VMEM((d, MOE_CHUNK), bf16),
            pltpu.VMEM((d, MOE_CHUNK), bf16),
            pltpu.VMEM((d_exp, MOE_CHUNK), bf16),
        ],
    )
    n_l, n_e = b_gate_up.shape[:2]
    return pl.pallas_call(
        functools.partial(_moe_kernel, n_chunks=n_chunks), grid_spec=grid_spec,
        out_shape=jax.ShapeDtypeStruct((n_rows, d), f32),
        compiler_params=_cparams(2), name="moe_experts",
    )(sb_expert, sb_nsub, sb_src, x_sorted, w_gate_up, w_gate_up,
      b_gate_up.reshape(n_l, n_e, 1, -1), b_gate_up.reshape(n_l, n_e, 1, -1),
      w_down, b_down.reshape(n_l, n_e, 1, -1))


def _combine_kernel(dest_ref, dest_next_ref, y_hbm, gate_ref, x_ref, g2_ref, o_ref, ybuf, sem):
    i = pl.program_id(0)
    n = pl.num_programs(0)
    tm = x_ref.shape[0]
    slot = lax.rem(i, 2)

    def issue(dest_blk, sl):
        for k in range(TOP_K):
            def body(t, carry, k=k):
                pltpu.make_async_copy(y_hbm.at[pl.ds(dest_blk[0, 0, k * tm + t], 1)],
                                      ybuf.at[sl * TOP_K + k, pl.ds(t, 1)], sem.at[sl]).start()
                return carry

            lax.fori_loop(0, tm, body, 0, unroll=8)

    @pl.when(i == 0)
    def _():
        issue(dest_ref, 0)

    @pl.when(i + 1 < n)
    def _():
        issue(dest_next_ref, 1 - slot)

    for k in range(TOP_K):
        pltpu.make_async_copy(y_hbm.at[pl.ds(0, tm)], ybuf.at[slot * TOP_K + k], sem.at[slot]).wait()
    gates = gate_ref[...]
    acc = gates[:, 0:1] * ybuf[slot * TOP_K]
    for k in range(1, TOP_K):
        acc = acc + gates[:, k:k + 1] * ybuf[slot * TOP_K + k]
    o_ref[...] = x_ref[...] + g2_ref[...] * acc


def _moe_combine(y_sorted, dest, gates, x2d, mod3, gate_chunk, seq):
    n_tok, d = x2d.shape
    tm = 128
    n_tiles = n_tok // tm
    dest_km = jnp.transpose(dest.reshape(n_tiles, tm, TOP_K), (0, 2, 1)).reshape(n_tiles, 1, TOP_K * tm)
    tiles_per_batch = seq // tm
    return pl.pallas_call(
        _combine_kernel,
        grid=(n_tiles,),
        in_specs=[
            pl.BlockSpec((1, 1, TOP_K * tm), lambda i: (i, 0, 0), memory_space=pltpu.SMEM),
            pl.BlockSpec((1, 1, TOP_K * tm), lambda i: (jnp.minimum(i + 1, n_tiles - 1), 0, 0),
                         memory_space=pltpu.SMEM),
            pl.BlockSpec(memory_space=pl.ANY),
            pl.BlockSpec((tm, LANES), lambda i: (i, 0)),
            pl.BlockSpec((tm, d), lambda i: (i, 0)),
            pl.BlockSpec((None, 1, d), lambda i: (i // tiles_per_batch, 0, gate_chunk)),
        ],
        out_specs=pl.BlockSpec((tm, d), lambda i: (i, 0)),
        out_shape=jax.ShapeDtypeStruct((n_tok, d), f32),
        scratch_shapes=[pltpu.VMEM((2 * TOP_K, tm, d), f32), pltpu.SemaphoreType.DMA((2,))],
        compiler_params=_cparams(1), name="moe_combine",
    )(dest_km, dest_km, y_sorted, gates, x2d, mod3)


def _routing_tables(top_idx):
    n_tok = top_idx.shape[0]
    n_assign = n_tok * TOP_K
    n_sb_max = (n_assign // MOE_SUB + N_EXPERTS + N_EXPERTS * (MOE_SB_SUBS - 1)) // MOE_SB_SUBS
    e_flat = top_idx.reshape(n_assign)
    onehot = (e_flat[:, None] == jnp.arange(N_EXPERTS, dtype=i32)[None, :]).astype(i32)
    csum = jnp.cumsum(onehot, axis=0)
    rank = jnp.sum(onehot * csum, axis=1) - 1
    counts = csum[-1]
    n_sub_e = (counts + MOE_SUB - 1) // MOE_SUB
    n_sb_e = (n_sub_e + MOE_SB_SUBS - 1) // MOE_SB_SUBS
    sb_end = jnp.cumsum(n_sb_e)
    sb_start = sb_end - n_sb_e
    base = n_sub_e // jnp.maximum(n_sb_e, 1)
    rem = n_sub_e - base * n_sb_e
    tables = jnp.stack([sb_start, base, rem], axis=1).astype(bf16)
    looked = jnp.dot(onehot.astype(bf16), tables, preferred_element_type=f32).astype(i32)
    a_start, a_base, a_rem = looked[:, 0], looked[:, 1], looked[:, 2]
    q = rank // MOE_SUB
    thr = a_rem * (a_base + 1)
    sb_local = jnp.where(q < thr, q // (a_base + 1), a_rem + (q - thr) // jnp.maximum(a_base, 1))
    sub_in = jnp.where(q < thr, q % (a_base + 1), (q - thr) % jnp.maximum(a_base, 1))
    dest = (a_start + sb_local) * MOE_SB_ROWS + sub_in * MOE_SUB + rank % MOE_SUB
    tok_flat = jnp.arange(n_assign, dtype=i32) // TOP_K
    row_tok = jnp.zeros((n_sb_max * MOE_SB_ROWS,), i32).at[dest].set(
        tok_flat, unique_indices=True, mode="promise_in_bounds")
    s = jnp.arange(n_sb_max, dtype=i32)
    n_sb_total = sb_end[-1]
    sb_valid = s < n_sb_total
    sb_src = jnp.where(sb_valid, s, n_sb_total - 1).astype(i32)
    sb_e = jnp.minimum(jnp.searchsorted(sb_end, sb_src, side="right"), N_EXPERTS - 1).astype(i32)
    sb_local_s = sb_src - sb_start[sb_e]
    sb_nsub = jnp.where(sb_valid, base[sb_e] + (sb_local_s < rem[sb_e]).astype(i32), 0).astype(i32)
    chunk_valid = (jnp.arange(MOE_SB_SUBS, dtype=i32)[None, :] < sb_nsub[:, None]).astype(i32).reshape(-1)
    return dest.reshape(n_tok, TOP_K).astype(i32), row_tok, chunk_valid, sb_e, sb_nsub, sb_src


def kernel(x, c, rel_bias, norm_gain, ada_w, ada_b, attn_w_qkv, attn_b_qkv, attn_q_gain, attn_k_gain, attn_sinks, attn_w_o, attn_b_o, ssm_lam_re, ssm_lam_im, ssm_log_dt, ssm_b_re, ssm_b_im, ssm_c_re, ssm_c_im, ssm_d, ssm_w_glu_a, ssm_b_glu_a, ssm_w_glu_b, ssm_b_glu_b, moe_w_router, moe_b_router, moe_w_gate_up, moe_b_gate_up, moe_w_down, moe_b_down):
    bsz, seq, d = x.shape
    n_tok = bsz * seq
    depth = norm_gain.shape[0]
    tm, tn = 1024, 512
    SH1, SC1, G1, SH2, SC2, G2 = range(6)

    def gated(accs, extras):
        x_res, gate = extras
        return x_res + gate * accs[0]

    def gated_glu(accs, extras):
        x_res, gate = extras
        return x_res + gate * (accs[0] * jax.nn.sigmoid(accs[1]))

    for layer in range(depth):
        mod3 = _modulation(c, ada_w, ada_b, layer).reshape(bsz, 1, 6 * d)
        x2d = x.reshape(n_tok, d)
        res_extras = _gated_residual_extras(x2d, mod3, G1, d, tm, tn, seq)
        i = layer // 2
        if layer % 2 == 0:
            h = _norm(x, norm_gain[layer, 0], mod3, SC1, SH1, bf16)
            qkv = _dense(h.reshape(n_tok, d), [attn_w_qkv], [attn_b_qkv], i, lambda accs, extras: accs[0], [],
                         f32, tm, 512, "qkv_proj")
            o = _attention(qkv.reshape(bsz, seq, -1), attn_q_gain[i], attn_k_gain[i], attn_sinks[i], rel_bias)
            x2d = _dense(o.reshape(n_tok, -1), [attn_w_o], [attn_b_o], i, gated, res_extras, f32, tm, tn, "attn_out")
        else:
            h = _norm(x, norm_gain[layer, 0], mod3, SC1, SH1, f32)
            y = _ssm(h, ssm_lam_re[i], ssm_lam_im[i], ssm_log_dt[i], ssm_b_re[i], ssm_b_im[i],
                     ssm_c_re[i], ssm_c_im[i], ssm_d[i])
            x2d = _dense(y.reshape(n_tok, d), [ssm_w_glu_a, ssm_w_glu_b], [ssm_b_glu_a, ssm_b_glu_b], i,
                         gated_glu, res_extras, f32, tm, tn, "ssm_glu")
        x = x2d.reshape(bsz, seq, d)
        h, top_idx, gates = _norm(x, norm_gain[layer, 1], mod3, SC2, SH2, f32,
                                  router=(moe_w_router, moe_b_router, layer))
        top_idx = top_idx.reshape(n_tok, LANES)[:, :TOP_K]
        dest, row_tok, chunk_valid, sb_e, sb_nsub, sb_src = _routing_tables(top_idx)
        x_sorted = _moe_gather(h, d // LANES, row_tok, chunk_valid)
        y_sorted = _moe_experts(x_sorted, sb_e, sb_nsub, sb_src, moe_w_gate_up, moe_b_gate_up,
                                moe_w_down, moe_b_down, layer)
        x = _moe_combine(y_sorted, dest, gates.reshape(n_tok, LANES), x2d, mod3, G2, seq).reshape(bsz, seq, d)
    return x
```

```python
import functools
import math

import numpy as np
import jax
import jax.numpy as jnp
from jax import lax
from jax.experimental import pallas as pl
from jax.experimental.pallas import tpu as pltpu

f32 = jnp.float32
bf16 = jnp.bfloat16
i32 = jnp.int32

N_Q_HEADS = 32
N_KV_HEADS = 4
HEAD_DIM = 64
Q_PER_KV = N_Q_HEADS // N_KV_HEADS
WINDOW = 128
NUM_BUCKETS = 32
MAX_DISTANCE = 128
SSM_GROUP_CH = 16
SSM_STATE = 64
N_EXPERTS = 32
TOP_K = 4
SWIGLU_ALPHA = 1.702
SWIGLU_LIMIT = 7.0
NORM_EPS = 1e-5

LANES = 128
SUBLANES = 8
VMEM_LIMIT = 56 * 1024 * 1024
N_DMA_PRIORITIES = 2

SSM_TILE_CH = 128
SSM_TIME_CHUNK = 256
MOE_SUB = 256
MOE_SB_SUBS = 8
MOE_SB_ROWS = MOE_SUB * MOE_SB_SUBS
MOE_GROUP = 4
MOE_CHUNK = 256


def _cparams(n_axes):
    return pltpu.CompilerParams(dimension_semantics=("arbitrary",) * n_axes, vmem_limit_bytes=VMEM_LIMIT)


def _mod_kernel(c_ref, w_ref, b_ref, o_ref):
    c = c_ref[...]
    cond = c * jax.nn.sigmoid(c)
    o_ref[...] = jnp.dot(cond.astype(bf16), w_ref[...].astype(bf16), preferred_element_type=f32) + b_ref[...]


def _modulation(c, ada_w, ada_b, layer):
    bsz, d = c.shape
    n = ada_w.shape[-1]
    tn = 1024
    return pl.pallas_call(
        _mod_kernel,
        grid=(n // tn,),
        in_specs=[
            pl.BlockSpec((bsz, d), lambda j: (0, 0)),
            pl.BlockSpec((None, d, tn), lambda j: (layer, 0, j)),
            pl.BlockSpec((None, 1, tn), lambda j: (layer, 0, j)),
        ],
        out_specs=pl.BlockSpec((bsz, tn), lambda j: (0, j)),
        out_shape=jax.ShapeDtypeStruct((bsz, n), f32),
        compiler_params=_cparams(1),
        name="adaln_mod",
    )(c, ada_w, ada_b.reshape(ada_b.shape[0], 1, n))


def _norm_mod(x_ref, gain_ref, sc_ref, sh_ref):
    x = x_ref[0]
    ms = jnp.mean(x * x, axis=-1, keepdims=True)
    y = x * lax.rsqrt(ms + NORM_EPS) * gain_ref[...]
    return y * (1.0 + sc_ref[0]) + sh_ref[0]


def _norm_kernel(x_ref, gain_ref, sc_ref, sh_ref, h_ref):
    h_ref[0] = _norm_mod(x_ref, gain_ref, sc_ref, sh_ref).astype(h_ref.dtype)


def _split_bf16(v):
    hi = v.astype(bf16)
    lo = (v - hi.astype(f32)).astype(bf16)
    return hi, lo


def _norm_router_kernel(x_ref, gain_ref, sc_ref, sh_ref, wr_ref, br_ref, h_ref, idx_ref, gate_ref):
    h = _norm_mod(x_ref, gain_ref, sc_ref, sh_ref)
    ts, d = h.shape
    n_col = d // LANES
    for j in range(n_col):
        h_ref[pl.ds(j, ts, stride=n_col), :] = h[:, j * LANES:(j + 1) * LANES]
    h_hi, h_lo = _split_bf16(h)
    w_hi, w_lo = _split_bf16(wr_ref[...])
    dot = functools.partial(jnp.dot, preferred_element_type=f32)
    logits = dot(h_hi, w_hi) + (dot(h_hi, w_lo) + dot(h_lo, w_hi)) + br_ref[...]
    rows, n_exp = logits.shape
    col = lax.broadcasted_iota(i32, (rows, n_exp), 1)
    work = logits
    tops, idxs = [], []
    for _ in range(TOP_K):
        m = jnp.max(work, axis=-1, keepdims=True)
        idx = jnp.min(jnp.where(work == m, col, n_exp), axis=-1, keepdims=True)
        work = jnp.where(col == idx, -jnp.inf, work)
        tops.append(m)
        idxs.append(idx)
    es = [jnp.exp(t - tops[0]) for t in tops]
    denom = es[0] + es[1] + es[2] + es[3]
    lane = lax.broadcasted_iota(i32, (rows, LANES), 1)
    idx_out = jnp.zeros((rows, LANES), i32)
    gate_out = jnp.zeros((rows, LANES), f32)
    for k in range(TOP_K):
        idx_out = jnp.where(lane == k, idxs[k], idx_out)
        gate_out = jnp.where(lane == k, es[k] / denom, gate_out)
    idx_ref[0] = idx_out
    gate_ref[0] = gate_out


def _norm(x, gain, mod3, sc_chunk, sh_chunk, out_dtype, router=None):
    bsz, seq, d = x.shape
    ts = 512
    in_specs = [
        pl.BlockSpec((1, ts, d), lambda b, s: (b, s, 0)),
        pl.BlockSpec((1, d), lambda b, s: (0, 0)),
        pl.BlockSpec((1, 1, d), lambda b, s: (b, 0, sc_chunk)),
        pl.BlockSpec((1, 1, d), lambda b, s: (b, 0, sh_chunk)),
    ]
    h_spec = pl.BlockSpec((1, ts, d), lambda b, s: (b, s, 0))
    args = [x, gain.reshape(1, d), mod3, mod3]
    if router is None:
        return pl.pallas_call(
            _norm_kernel, grid=(bsz, seq // ts), in_specs=in_specs, out_specs=h_spec,
            out_shape=jax.ShapeDtypeStruct((bsz, seq, d), out_dtype),
            compiler_params=_cparams(2), name="norm_mod",
        )(*args)
    w_router, b_router, layer = router
    n_exp = w_router.shape[-1]
    in_specs += [
        pl.BlockSpec((None, d, n_exp), lambda b, s: (layer, 0, 0)),
        pl.BlockSpec((None, 1, n_exp), lambda b, s: (layer, 0, 0)),
    ]
    lane_spec = pl.BlockSpec((1, ts, LANES), lambda b, s: (b, s, 0))
    n_col = d // LANES
    h3_spec = pl.BlockSpec((ts * n_col, LANES), lambda b, s: (b * (seq // ts) + s, 0))
    return pl.pallas_call(
        _norm_router_kernel, grid=(bsz, seq // ts), in_specs=in_specs,
        out_specs=[h3_spec, lane_spec, lane_spec],
        out_shape=[jax.ShapeDtypeStruct((bsz * seq * n_col, LANES), f32),
                   jax.ShapeDtypeStruct((bsz, seq, LANES), i32),
                   jax.ShapeDtypeStruct((bsz, seq, LANES), f32)],
        compiler_params=_cparams(2), name="norm_router",
    )(*args, w_router, b_router.reshape(b_router.shape[0], 1, n_exp))


def _dense_kernel(*refs, n_w, n_extra, epilogue):
    x_ref = refs[0]
    w_refs = refs[1:1 + n_w]
    b_refs = refs[1 + n_w:1 + 2 * n_w]
    e_refs = refs[1 + 2 * n_w:1 + 2 * n_w + n_extra]
    o_ref = refs[1 + 2 * n_w + n_extra]
    wbf_refs = refs[2 + 2 * n_w + n_extra:]

    @pl.when(pl.program_id(1) == 0)
    def _():
        for w_ref, wbf in zip(w_refs, wbf_refs):
            wbf[...] = w_ref[...].astype(bf16)

    x = x_ref[...]
    accs = [jnp.dot(x, wbf[...], preferred_element_type=f32) + b_ref[...] for wbf, b_ref in zip(wbf_refs, b_refs)]
    o_ref[...] = epilogue(accs, [e[...] for e in e_refs]).astype(o_ref.dtype)


def _dense(x, ws, bs, layer, epilogue, extras, out_dtype, tm, tn, name):
    m, k = x.shape
    n = ws[0].shape[-1]
    in_specs = [pl.BlockSpec((tm, k), lambda j, i: (i, 0))]
    in_specs += [pl.BlockSpec((None, k, tn), lambda j, i: (layer, 0, j)) for _ in ws]
    in_specs += [pl.BlockSpec((None, 1, tn), lambda j, i: (layer, 0, j)) for _ in bs]
    in_specs += [spec for _, spec in extras]
    args = [x] + list(ws) + [b.reshape(b.shape[0], 1, n) for b in bs] + [a for a, _ in extras]
    return pl.pallas_call(
        functools.partial(_dense_kernel, n_w=len(ws), n_extra=len(extras), epilogue=epilogue),
        grid=(n // tn, m // tm),
        in_specs=in_specs,
        out_specs=pl.BlockSpec((tm, tn), lambda j, i: (i, j)),
        out_shape=jax.ShapeDtypeStruct((m, n), out_dtype),
        scratch_shapes=[pltpu.VMEM((k, tn), bf16) for _ in ws],
        compiler_params=_cparams(2), name=name,
    )(*args)


def _gated_residual_extras(x2d, mod3, gate_chunk, d, tm, tn, seq):
    tiles_per_batch = seq // tm
    return [
        (x2d, pl.BlockSpec((tm, tn), lambda j, i: (i, j))),
        (mod3, pl.BlockSpec((None, 1, tn), lambda j, i: (i // tiles_per_batch, 0, gate_chunk * (d // tn) + j))),
    ]


def _attn_kernel(sinks_ref, q_ref, kp_ref, kc_ref, vp_ref, vc_ref, qg_ref, kg_ref, bias_ref, o_ref):
    n = pl.program_id(1)
    lane = lax.broadcasted_iota(i32, (1, LANES), 1)
    lo = lane < HEAD_DIM
    bq = q_ref.shape[1]

    def halfnorm(v, gain):
        sq = v * v
        s_lo = jnp.sum(jnp.where(lo, sq, 0.0), axis=-1, keepdims=True)
        s_hi = jnp.sum(jnp.where(lo, 0.0, sq), axis=-1, keepdims=True)
        ms = jnp.where(lo, s_lo, s_hi) * (1.0 / HEAD_DIM)
        return v * lax.rsqrt(ms + NORM_EPS) * gain

    k_all = jnp.concatenate([kp_ref[0], kc_ref[0]], axis=0)
    v_all = jnp.concatenate([vp_ref[0], vc_ref[0]], axis=0)
    kcol = lax.broadcasted_iota(i32, (1, 2 * bq), 1)
    key_ok = jnp.logical_or(kcol >= bq, n > 0)
    qgain = qg_ref[...]
    kgain = kg_ref[...]
    pairs = Q_PER_KV // 2
    for c in range(N_KV_HEADS // 2):
        kn = halfnorm(k_all[:, c * LANES:(c + 1) * LANES], kgain)
        kr = pltpu.roll(kn, HEAD_DIM, 1)
        vn = v_all[:, c * LANES:(c + 1) * LANES]
        vr = pltpu.roll(vn, HEAD_DIM, 1)
        for half in range(2):
            g = 2 * c + half
            k_src, k_rot = (kn, kr) if half == 0 else (kr, kn)
            v_src, v_rot = (vn, vr) if half == 0 else (vr, vn)
            k_par = [jnp.where(lo, k_src, 0.0).astype(bf16), jnp.where(lo, 0.0, k_rot).astype(bf16)]
            v_par = [jnp.where(lo, v_src, 0.0).astype(bf16), jnp.where(lo, 0.0, v_rot).astype(bf16)]
            qs = [halfnorm(q_ref[0, :, (g * pairs + p) * LANES:(g * pairs + p + 1) * LANES], qgain)
                  for p in range(pairs)]
            qg = (jnp.concatenate(qs, axis=0) * (1.0 / math.sqrt(HEAD_DIM))).astype(bf16)
            acc = None
            for par in range(2):
                s = lax.dot_general(qg, k_par[par], (((1,), (1,)), ((), ())), preferred_element_type=f32)
                s = s + bias_ref[g, par]
                s = jnp.where(key_ok, s, -jnp.inf)
                sink = jnp.concatenate(
                    [jnp.full((bq, 1), sinks_ref[g * Q_PER_KV + 2 * p + par], f32) for p in range(pairs)], axis=0)
                m = jnp.maximum(jnp.max(s, axis=-1, keepdims=True), sink)
                e = jnp.exp(s - m)
                den = jnp.sum(e, axis=-1, keepdims=True) + jnp.exp(sink - m)
                probs = (e * (1.0 / den)).astype(bf16)
                o = jnp.dot(probs, v_par[par], preferred_element_type=f32)
                acc = o if acc is None else acc + o
            for p in range(pairs):
                col = (g * pairs + p) * LANES
                o_ref[0, :, col:col + LANES] = acc[p * bq:(p + 1) * bq].astype(o_ref.dtype)


def _t5_bucket(dist):
    nn = np.maximum(dist, 0)
    max_exact = NUM_BUCKETS // 2
    large = max_exact + (np.log(np.maximum(nn, 1) / max_exact) / np.log(MAX_DISTANCE / max_exact)
                         * (NUM_BUCKETS - max_exact)).astype(np.int32)
    large = np.minimum(large, NUM_BUCKETS - 1)
    return np.where(nn < max_exact, nn, large).astype(np.int32)


def _attention(qkv, q_gain, k_gain, sinks, rel_bias):
    bsz, seq, _ = qkv.shape
    bq = WINDOW
    q_dim = N_Q_HEADS * HEAD_DIM
    kv_dim = N_KV_HEADS * HEAD_DIM
    pairs = Q_PER_KV // 2
    ql = np.arange(bq)[:, None]
    kl = np.arange(2 * bq)[None, :]
    dist = ql + bq - kl
    in_window = (dist >= 0) & (dist < WINDOW)
    bias = jnp.take(rel_bias.astype(f32), jnp.asarray(_t5_bucket(dist)), axis=0)
    bias = jnp.where(jnp.asarray(in_window)[:, :, None], bias, -jnp.inf)
    bias = jnp.transpose(bias, (2, 0, 1)).reshape(N_KV_HEADS, pairs, 2, bq, 2 * bq)
    bias = jnp.transpose(bias, (0, 2, 1, 3, 4)).reshape(N_KV_HEADS, 2, pairs * bq, 2 * bq)
    gain2 = lambda gn: jnp.concatenate([gn, gn]).reshape(1, LANES).astype(f32)
    k_blk = q_dim // kv_dim
    grid_spec = pltpu.PrefetchScalarGridSpec(
        num_scalar_prefetch=1,
        grid=(bsz, seq // bq),
        in_specs=[
            pl.BlockSpec((1, bq, q_dim), lambda b, n, s: (b, n, 0)),
            pl.BlockSpec((1, bq, kv_dim), lambda b, n, s: (b, jnp.maximum(n - 1, 0), k_blk)),
            pl.BlockSpec((1, bq, kv_dim), lambda b, n, s: (b, n, k_blk)),
            pl.BlockSpec((1, bq, kv_dim), lambda b, n, s: (b, jnp.maximum(n - 1, 0), k_blk + 1)),
            pl.BlockSpec((1, bq, kv_dim), lambda b, n, s: (b, n, k_blk + 1)),
            pl.BlockSpec((1, LANES), lambda b, n, s: (0, 0)),
            pl.BlockSpec((1, LANES), lambda b, n, s: (0, 0)),
            pl.BlockSpec((N_KV_HEADS, 2, pairs * bq, 2 * bq), lambda b, n, s: (0, 0, 0, 0)),
        ],
        out_specs=pl.BlockSpec((1, bq, q_dim), lambda b, n, s: (b, n, 0)),
    )
    return pl.pallas_call(
        _attn_kernel, grid_spec=grid_spec,
        out_shape=jax.ShapeDtypeStruct((bsz, seq, q_dim), bf16),
        compiler_params=_cparams(2), name="swa_attention",
    )(sinks.astype(f32), qkv, qkv, qkv, qkv, qkv, gain2(q_gain), gain2(k_gain), bias)


def _gelu_tanh(y):
    return 0.5 * y * (1.0 + jnp.tanh(math.sqrt(2.0 / math.pi) * (y + 0.044715 * (y * y * y))))


def _ssm_kernel(u_ref, bm_ref, cm_ref, are_ref, aim_ref, d_ref, y_ref, us, xs, st, ys):
    bsz, ts, _ = u_ref.shape
    n_state = are_ref.shape[-1]

    @pl.when(pl.program_id(1) == 0)
    def _():
        st[...] = jnp.zeros_like(st)

    n_slab = us.shape[0]
    for h in range(n_slab):
        for b in range(bsz):
            us[h, pl.ds(b, ts, stride=bsz), :] = u_ref[b, :, h * LANES:(h + 1) * LANES]
    u = jnp.concatenate([us[h] for h in range(n_slab)], axis=1)
    xs[...] = jnp.dot(u.astype(bf16), bm_ref[0], preferred_element_type=f32)
    a_re = jnp.broadcast_to(are_ref[0], (bsz, n_state))
    a_im = jnp.broadcast_to(aim_ref[0], (bsz, n_state))

    def step(t, carry):
        s_re, s_im = carry
        r0 = pl.multiple_of(t * bsz, bsz)
        x_re = xs[pl.ds(r0, bsz), 0:n_state]
        x_im = xs[pl.ds(r0, bsz), n_state:2 * n_state]
        n_re = a_re * s_re - a_im * s_im + x_re
        n_im = a_re * s_im + a_im * s_re + x_im
        xs[pl.ds(r0, bsz), 0:n_state] = n_re
        xs[pl.ds(r0, bsz), n_state:2 * n_state] = n_im
        return n_re, n_im

    s_re, s_im = lax.fori_loop(0, ts, step, (st[0], st[1]), unroll=8)
    st[0] = s_re
    st[1] = s_im
    y = jnp.dot(xs[...].astype(bf16), cm_ref[0], preferred_element_type=f32) + d_ref[0] * u
    y = _gelu_tanh(y)
    for h in range(n_slab):
        ys[h] = y[:, h * LANES:(h + 1) * LANES]
        for b in range(bsz):
            y_ref[b, :, h * LANES:(h + 1) * LANES] = ys[h, pl.ds(b, ts, stride=bsz), :].astype(y_ref.dtype)


def _ssm(h, lam_re, lam_im, log_dt, b_re, b_im, c_re, c_im, d_skip):
    bsz, seq, width = h.shape
    assert bsz == SUBLANES
    n_grp, n_st = lam_re.shape
    tile_ch = SSM_TILE_CH
    gpt = tile_ch // SSM_GROUP_CH
    n_tiles = width // tile_ch
    ns = gpt * n_st
    dt = jnp.exp(log_dt.astype(f32))[:, None]
    lr, li = lam_re.astype(f32), lam_im.astype(f32)
    mag = jnp.exp(lr * dt)
    ab_re, ab_im = mag * jnp.cos(li * dt), mag * jnp.sin(li * dt)
    den = lr * lr + li * li
    nr, ni = ab_re - 1.0, ab_im
    f_re = (nr * lr + ni * li) / den
    f_im = (ni * lr - nr * li) / den
    br, bi = b_re.astype(f32), b_im.astype(f32)
    bb_re = f_re[..., None] * br - f_im[..., None] * bi
    bb_im = f_re[..., None] * bi + f_im[..., None] * br
    eye = jnp.eye(gpt, dtype=f32)

    def blockdiag_in(bb):
        t = bb.reshape(n_tiles, gpt, n_st, SSM_GROUP_CH)
        return jnp.einsum("tgpc,gh->tgchp", t, eye).reshape(n_tiles, tile_ch, ns)

    def blockdiag_out(cc):
        t = cc.reshape(n_tiles, gpt, SSM_GROUP_CH, n_st)
        return jnp.einsum("tgcp,gh->tgphc", t, eye).reshape(n_tiles, ns, tile_ch)

    bm = jnp.concatenate([blockdiag_in(bb_re), blockdiag_in(bb_im)], axis=-1).astype(bf16)
    cm = jnp.concatenate([blockdiag_out(c_re.astype(f32)), -blockdiag_out(c_im.astype(f32))], axis=1).astype(bf16)
    a_re = ab_re.reshape(n_tiles, 1, ns)
    a_im = ab_im.reshape(n_tiles, 1, ns)
    dsk = d_skip.astype(f32).reshape(n_tiles, 1, tile_ch)
    ts = SSM_TIME_CHUNK
    return pl.pallas_call(
        _ssm_kernel,
        grid=(n_tiles, seq // ts),
        in_specs=[
            pl.BlockSpec((bsz, ts, tile_ch), lambda g, t: (0, t, g)),
            pl.BlockSpec((1, tile_ch, 2 * ns), lambda g, t: (g, 0, 0)),
            pl.BlockSpec((1, 2 * ns, tile_ch), lambda g, t: (g, 0, 0)),
            pl.BlockSpec((1, 1, ns), lambda g, t: (g, 0, 0)),
            pl.BlockSpec((1, 1, ns), lambda g, t: (g, 0, 0)),
            pl.BlockSpec((1, 1, tile_ch), lambda g, t: (g, 0, 0)),
        ],
        out_specs=pl.BlockSpec((bsz, ts, tile_ch), lambda g, t: (0, t, g)),
        out_shape=jax.ShapeDtypeStruct((bsz, seq, width), bf16),
        scratch_shapes=[
            pltpu.VMEM((tile_ch // LANES, ts * bsz, LANES), f32),
            pltpu.VMEM((ts * bsz, 2 * ns), f32),
            pltpu.VMEM((2, bsz, ns), f32),
            pltpu.VMEM((tile_ch // LANES, ts * bsz, LANES), f32),
        ],
        compiler_params=_cparams(2), name="s5_ssm",
    )(h, bm, cm, a_re, a_im, dsk)


GATHER_PITCH = 24


def _gather_kernel(valid_ref, idx_ref, idx_next_ref, h_hbm, o_ref, buf, sem):
    i = pl.program_id(0)
    n = pl.num_programs(0)
    rows, d = o_ref.shape
    n_col = d // LANES
    slot = lax.rem(i, 2)

    def issue(idx_blk, sl):
        def body(r2, carry):
            for prio in range(N_DMA_PRIORITIES):
                r = r2 * N_DMA_PRIORITIES + prio
                src = pl.multiple_of(idx_blk[0, 0, r] * n_col, SUBLANES)
                dst = pl.multiple_of((sl * rows + r) * GATHER_PITCH, SUBLANES)
                pltpu.make_async_copy(h_hbm.at[pl.ds(src, n_col)], buf.at[pl.ds(dst, n_col)],
                                      sem.at[sl]).start(priority=prio)
            return carry

        lax.fori_loop(0, rows // N_DMA_PRIORITIES, body, 0, unroll=4)

    @pl.when(jnp.logical_and(i == 0, valid_ref[0] > 0))
    def _():
        issue(idx_ref, 0)

    @pl.when(jnp.logical_and(i + 1 < n, valid_ref[jnp.minimum(i + 1, n - 1)] > 0))
    def _():
        issue(idx_next_ref, 1 - slot)

    @pl.when(valid_ref[i] > 0)
    def _():
        base = pl.multiple_of(slot * (rows * GATHER_PITCH), SUBLANES)
        pltpu.make_async_copy(h_hbm.at[pl.ds(0, rows * n_col)], buf.at[pl.ds(base, rows * n_col)],
                              sem.at[slot]).wait()
        for j in range(n_col):
            piece = buf[pl.ds(base + j, rows, stride=GATHER_PITCH), :]
            o_ref[:, j * LANES:(j + 1) * LANES] = piece.astype(o_ref.dtype)

    @pl.when(valid_ref[i] == 0)
    def _():
        o_ref[...] = jnp.zeros_like(o_ref)


def _moe_gather(h3, n_col, row_tok, chunk_valid):
    n_rows = row_tok.shape[0]
    d = n_col * LANES
    n_chunks = n_rows // MOE_SUB
    idx = row_tok.reshape(n_chunks, 1, MOE_SUB)
    grid_spec = pltpu.PrefetchScalarGridSpec(
        num_scalar_prefetch=1,
        grid=(n_chunks,),
        in_specs=[
            pl.BlockSpec((1, 1, MOE_SUB), lambda i, v: (i, 0, 0), memory_space=pltpu.SMEM),
            pl.BlockSpec((1, 1, MOE_SUB), lambda i, v: (jnp.minimum(i + 1, n_chunks - 1), 0, 0),
                         memory_space=pltpu.SMEM),
            pl.BlockSpec(memory_space=pl.ANY),
        ],
        out_specs=pl.BlockSpec((MOE_SUB, d), lambda i, v: (i, 0)),
        scratch_shapes=[pltpu.VMEM((2 * MOE_SUB * GATHER_PITCH, LANES), f32), pltpu.SemaphoreType.DMA((2,))],
    )
    return pl.pallas_call(
        _gather_kernel, grid_spec=grid_spec,
        out_shape=jax.ShapeDtypeStruct((n_rows, d), bf16),
        compiler_params=_cparams(1), name="moe_gather",
    )(chunk_valid, idx, idx, h3)


def _swiglu(g, l):
    glu = jnp.minimum(g, SWIGLU_LIMIT)
    lin = jnp.clip(l, -SWIGLU_LIMIT, SWIGLU_LIMIT)
    return glu * jax.nn.sigmoid(SWIGLU_ALPHA * glu) * (lin + 1.0)


def _for_row_groups(nsub, cast, fn):
    n_big = nsub // MOE_GROUP
    big_rows = MOE_GROUP * MOE_SUB

    @pl.when(n_big > 0)
    def _():
        fn(0, big_rows, cast())

    @pl.when(n_big == 0)
    def _():
        fn(0, MOE_SUB, cast())

    def big(i, carry):
        fn(pl.multiple_of(i * big_rows, big_rows), big_rows, None)
        return carry

    def small(i, carry):
        fn(pl.multiple_of(i * MOE_SUB, MOE_SUB), MOE_SUB, None)
        return carry

    lax.fori_loop(1, n_big, big, 0)
    lax.fori_loop(jnp.maximum(n_big * MOE_GROUP, 1), nsub, small, 0)


def _moe_kernel(sbe_ref, sbn_ref, sbsrc_ref, x_ref, wg_ref, wl_ref, bg_ref, bl_ref, wd_ref, bd_ref, o_ref,
                a_scr, wg_bf, wl_bf, wd_bf, *, n_chunks):
    s = pl.program_id(0)
    c = pl.program_id(1)
    nsub = sbn_ref[s]

    @pl.when(jnp.logical_and(c < n_chunks, nsub > 0))
    def _():
        def cast():
            wg, wl = wg_ref[...].astype(bf16), wl_ref[...].astype(bf16)
            wg_bf[...] = wg
            wl_bf[...] = wl
            return wg, wl

        def up(r, rows, weights):
            wg, wl = weights if weights is not None else (wg_bf[...], wl_bf[...])
            xs = x_ref[pl.ds(r, rows), :]
            g = jnp.dot(xs, wg, preferred_element_type=f32) + bg_ref[...]
            l = jnp.dot(xs, wl, preferred_element_type=f32) + bl_ref[...]
            a_scr[c, pl.ds(r, rows), :] = _swiglu(g, l).astype(bf16)

        _for_row_groups(nsub, cast, up)

    @pl.when(jnp.logical_and(c >= n_chunks, nsub > 0))
    def _():
        def cast():
            wd = wd_ref[...].astype(bf16)
            wd_bf[...] = wd
            return wd

        def down(r, rows, weights):
            wd = weights if weights is not None else wd_bf[...]
            a = jnp.concatenate([a_scr[cc, pl.ds(r, rows), :] for cc in range(n_chunks)], axis=1)
            o_ref[pl.ds(r, rows), :] = jnp.dot(a, wd, preferred_element_type=f32) + bd_ref[...]

        _for_row_groups(nsub, cast, down)

    @pl.when(c >= n_chunks)
    def _():
        def zero(i, carry):
            r = pl.multiple_of(i * MOE_SUB, MOE_SUB)
            o_ref[pl.ds(r, MOE_SUB), :] = jnp.zeros((MOE_SUB, o_ref.shape[1]), f32)
            return carry

        lax.fori_loop(nsub, MOE_SB_SUBS, zero, 0)


def _moe_experts(x_sorted, sb_expert, sb_nsub, sb_src, w_gate_up, b_gate_up, w_down, b_down, layer):
    n_rows, d = x_sorted.shape
    n_sb = n_rows // MOE_SB_ROWS
    d_exp = w_down.shape[2]
    n_chunks = d_exp // MOE_CHUNK
    n_out_chunks = d // MOE_CHUNK
    assert n_out_chunks == n_chunks
    last = n_chunks - 1

    def valid(s, n):
        return n[s] > 0

    def wg_map(s, c, e, n, src):
        return (layer, e[s], 0, jnp.where(valid(s, n), jnp.minimum(c, last), last))

    def wl_map(s, c, e, n, src):
        return (layer, e[s], 0, n_chunks + jnp.where(valid(s, n), jnp.minimum(c, last), last))

    def wd_map(s, c, e, n, src):
        return (layer, e[s], 0, jnp.where(valid(s, n), jnp.maximum(c - n_chunks, 0), last))

    def out_map(s, c, e, n, src):
        return (s, jnp.maximum(c - n_chunks, 0))

    grid_spec = pltpu.PrefetchScalarGridSpec(
        num_scalar_prefetch=3,
        grid=(n_sb, n_chunks + n_out_chunks),
        in_specs=[
            pl.BlockSpec((MOE_SB_ROWS, d), lambda s, c, e, n, src: (src[s], 0)),
            pl.BlockSpec((None, None, d, MOE_CHUNK), wg_map),
            pl.BlockSpec((None, None, d, MOE_CHUNK), wl_map),
            pl.BlockSpec((None, None, 1, MOE_CHUNK), wg_map),
            pl.BlockSpec((None, None, 1, MOE_CHUNK), wl_map),
            pl.BlockSpec((None, None, d_exp, MOE_CHUNK), wd_map),
            pl.BlockSpec((None, None, 1, MOE_CHUNK), wd_map),
        ],
        out_specs=pl.BlockSpec((MOE_SB_ROWS, MOE_CHUNK), out_map),
        scratch_shapes=[
            pltpu.VMEM((n_chunks, MOE_SB_ROWS, MOE_CHUNK), bf16),
            pltpu.VMEM((d, MOE_CHUNK), bf16),
            pltpu.VMEM((d, MOE_CHUNK), bf16),
            pltpu.VMEM((d_exp, MOE_CHUNK), bf16),
        ],
    )
    n_l, n_e = b_gate_up.shape[:2]
    return pl.pallas_call(
        functools.partial(_moe_kernel, n_chunks=n_chunks), grid_spec=grid_spec,
        out_shape=jax.ShapeDtypeStruct((n_rows, d), f32),
        compiler_params=_cparams(2), name="moe_experts",
    )(sb_expert, sb_nsub, sb_src, x_sorted, w_gate_up, w_gate_up,
      b_gate_up.reshape(n_l, n_e, 1, -1), b_gate_up.reshape(n_l, n_e, 1, -1),
      w_down, b_down.reshape(n_l, n_e, 1, -1))


def _combine_kernel(dest_ref, dest_next_ref, y_hbm, gate_ref, x_ref, g2_ref, o_ref, ybuf, sem):
    i = pl.program_id(0)
    n = pl.num_programs(0)
    tm = x_ref.shape[0]
    slot = lax.rem(i, 2)

    def issue(dest_blk, sl):
        for k in range(TOP_K):
            def body(t2, carry, k=k):
                for prio in range(N_DMA_PRIORITIES):
                    t = t2 * N_DMA_PRIORITIES + prio
                    pltpu.make_async_copy(y_hbm.at[pl.ds(dest_blk[0, 0, k * tm + t], 1)],
                                          ybuf.at[sl * TOP_K + k, pl.ds(t, 1)], sem.at[sl]).start(priority=prio)
                return carry

            lax.fori_loop(0, tm // N_DMA_PRIORITIES, body, 0, unroll=4)

    @pl.when(i == 0)
    def _():
        issue(dest_ref, 0)

    @pl.when(i + 1 < n)
    def _():
        issue(dest_next_ref, 1 - slot)

    for k in range(TOP_K):
        pltpu.make_async_copy(y_hbm.at[pl.ds(0, tm)], ybuf.at[slot * TOP_K + k], sem.at[slot]).wait()
    gates = gate_ref[...]
    acc = gates[:, 0:1] * ybuf[slot * TOP_K]
    for k in range(1, TOP_K):
        acc = acc + gates[:, k:k + 1] * ybuf[slot * TOP_K + k]
    o_ref[...] = x_ref[...] + g2_ref[...] * acc


def _moe_combine(y_sorted, dest, gates, x2d, mod3, gate_chunk, seq):
    n_tok, d = x2d.shape
    tm = 128
    n_tiles = n_tok // tm
    dest_km = jnp.transpose(dest.reshape(TOP_K, n_tiles, tm), (1, 0, 2)).reshape(n_tiles, 1, TOP_K * tm)
    tiles_per_batch = seq // tm
    return pl.pallas_call(
        _combine_kernel,
        grid=(n_tiles,),
        in_specs=[
            pl.BlockSpec((1, 1, TOP_K * tm), lambda i: (i, 0, 0), memory_space=pltpu.SMEM),
            pl.BlockSpec((1, 1, TOP_K * tm), lambda i: (jnp.minimum(i + 1, n_tiles - 1), 0, 0),
                         memory_space=pltpu.SMEM),
            pl.BlockSpec(memory_space=pl.ANY),
            pl.BlockSpec((tm, LANES), lambda i: (i, 0)),
            pl.BlockSpec((tm, d), lambda i: (i, 0)),
            pl.BlockSpec((None, 1, d), lambda i: (i // tiles_per_batch, 0, gate_chunk)),
        ],
        out_specs=pl.BlockSpec((tm, d), lambda i: (i, 0)),
        out_shape=jax.ShapeDtypeStruct((n_tok, d), f32),
        scratch_shapes=[pltpu.VMEM((2 * TOP_K, tm, d), f32), pltpu.SemaphoreType.DMA((2,))],
        compiler_params=_cparams(1), name="moe_combine",
    )(dest_km, dest_km, y_sorted, gates, x2d, mod3)


def _routing_tables(idx_km):
    n_tok = idx_km.shape[1]
    n_assign = n_tok * TOP_K
    n_sb_max = (n_assign // MOE_SUB + N_EXPERTS + N_EXPERTS * (MOE_SB_SUBS - 1)) // MOE_SB_SUBS
    experts = jnp.arange(N_EXPERTS, dtype=i32)
    onehot = idx_km[:, :, None] == experts[None, None, :]
    per_tok = jnp.sum(onehot.astype(i32), axis=0)
    before = jnp.cumsum(per_tok, axis=0) - per_tok
    rank = jnp.sum(jnp.where(onehot, before[None], 0), axis=2)
    counts = before[-1] + per_tok[-1]
    n_sub_e = (counts + MOE_SUB - 1) // MOE_SUB
    n_sb_e = (n_sub_e + MOE_SB_SUBS - 1) // MOE_SB_SUBS
    sb_end = jnp.cumsum(n_sb_e)
    sb_start = sb_end - n_sb_e
    base = n_sub_e // jnp.maximum(n_sb_e, 1)
    rem = n_sub_e - base * n_sb_e
    tables = jnp.stack([sb_start, base, rem], axis=1).astype(bf16)
    looked = jnp.dot(onehot.astype(bf16).reshape(n_assign, N_EXPERTS), tables,
                     preferred_element_type=f32).astype(i32).reshape(TOP_K, n_tok, 3)
    a_start, a_base, a_rem = looked[..., 0], looked[..., 1], looked[..., 2]
    q = rank // MOE_SUB
    thr = a_rem * (a_base + 1)
    sb_local = jnp.where(q < thr, q // (a_base + 1), a_rem + (q - thr) // jnp.maximum(a_base, 1))
    sub_in = jnp.where(q < thr, q % (a_base + 1), (q - thr) % jnp.maximum(a_base, 1))
    dest = ((a_start + sb_local) * MOE_SB_ROWS + sub_in * MOE_SUB + rank % MOE_SUB).astype(i32)
    tok = jnp.broadcast_to(jnp.arange(n_tok, dtype=i32)[None, :], (TOP_K, n_tok))
    row_tok = jnp.zeros((n_sb_max * MOE_SB_ROWS,), i32).at[dest.reshape(-1)].set(
        tok.reshape(-1), unique_indices=True, mode="promise_in_bounds")
    s = jnp.arange(n_sb_max, dtype=i32)
    n_sb_total = sb_end[-1]
    sb_valid = s < n_sb_total
    sb_src = jnp.where(sb_valid, s, n_sb_total - 1).astype(i32)
    sb_e = jnp.minimum(jnp.searchsorted(sb_end, sb_src, side="right"), N_EXPERTS - 1).astype(i32)
    sb_local_s = sb_src - sb_start[sb_e]
    sb_nsub = jnp.where(sb_valid, base[sb_e] + (sb_local_s < rem[sb_e]).astype(i32), 0).astype(i32)
    chunk_valid = (jnp.arange(MOE_SB_SUBS, dtype=i32)[None, :] < sb_nsub[:, None]).astype(i32).reshape(-1)
    return dest, row_tok, chunk_valid, sb_e, sb_nsub, sb_src


def kernel(x, c, rel_bias, norm_gain, ada_w, ada_b, attn_w_qkv, attn_b_qkv, attn_q_gain, attn_k_gain, attn_sinks, attn_w_o, attn_b_o, ssm_lam_re, ssm_lam_im, ssm_log_dt, ssm_b_re, ssm_b_im, ssm_c_re, ssm_c_im, ssm_d, ssm_w_glu_a, ssm_b_glu_a, ssm_w_glu_b, ssm_b_glu_b, moe_w_router, moe_b_router, moe_w_gate_up, moe_b_gate_up, moe_w_down, moe_b_down):
    bsz, seq, d = x.shape
    n_tok = bsz * seq
    depth = norm_gain.shape[0]
    tm, tn = 1024, 512
    SH1, SC1, G1, SH2, SC2, G2 = range(6)

    def gated(accs, extras):
        x_res, gate = extras
        return x_res + gate * accs[0]

    def gated_glu(accs, extras):
        x_res, gate = extras
        return x_res + gate * (accs[0] * jax.nn.sigmoid(accs[1]))

    for layer in range(depth):
        mod3 = _modulation(c, ada_w, ada_b, layer).reshape(bsz, 1, 6 * d)
        x2d = x.reshape(n_tok, d)
        res_extras = _gated_residual_extras(x2d, mod3, G1, d, tm, tn, seq)
        i = layer // 2
        if layer % 2 == 0:
            h = _norm(x, norm_gain[layer, 0], mod3, SC1, SH1, bf16)
            qkv = _dense(h.reshape(n_tok, d), [attn_w_qkv], [attn_b_qkv], i, lambda accs, extras: accs[0], [],
                         f32, tm, 512, "qkv_proj")
            o = _attention(qkv.reshape(bsz, seq, -1), attn_q_gain[i], attn_k_gain[i], attn_sinks[i], rel_bias)
            x2d = _dense(o.reshape(n_tok, -1), [attn_w_o], [attn_b_o], i, gated, res_extras, f32, tm, tn, "attn_out")
        else:
            h = _norm(x, norm_gain[layer, 0], mod3, SC1, SH1, f32)
            y = _ssm(h, ssm_lam_re[i], ssm_lam_im[i], ssm_log_dt[i], ssm_b_re[i], ssm_b_im[i],
                     ssm_c_re[i], ssm_c_im[i], ssm_d[i])
            x2d = _dense(y.reshape(n_tok, d), [ssm_w_glu_a, ssm_w_glu_b], [ssm_b_glu_a, ssm_b_glu_b], i,
                         gated_glu, res_extras, f32, tm, tn, "ssm_glu")
        x = x2d.reshape(bsz, seq, d)
        h, top_idx, gates = _norm(x, norm_gain[layer, 1], mod3, SC2, SH2, f32,
                                  router=(moe_w_router, moe_b_router, layer))
        idx_km = top_idx.reshape(n_tok, LANES)[:, :TOP_K].T
        dest, row_tok, chunk_valid, sb_e, sb_nsub, sb_src = _routing_tables(idx_km)
        x_sorted = _moe_gather(h, d // LANES, row_tok, chunk_valid)
        y_sorted = _moe_experts(x_sorted, sb_e, sb_nsub, sb_src, moe_w_gate_up, moe_b_gate_up,
                                moe_w_down, moe_b_down, layer)
        x = _moe_combine(y_sorted, dest, gates.reshape(n_tok, LANES), x2d, mod3, G2, seq).reshape(bsz, seq, d)
    return x
```

```python
import functools
import math

import numpy as np
import jax
import jax.numpy as jnp
from jax import lax
from jax.experimental import pallas as pl
from jax.experimental.pallas import tpu as pltpu

f32 = jnp.float32
bf16 = jnp.bfloat16
i32 = jnp.int32

N_Q_HEADS = 32
N_KV_HEADS = 4
HEAD_DIM = 64
Q_PER_KV = N_Q_HEADS // N_KV_HEADS
WINDOW = 128
NUM_BUCKETS = 32
MAX_DISTANCE = 128
SSM_GROUP_CH = 16
SSM_STATE = 64
N_EXPERTS = 32
TOP_K = 4
SWIGLU_ALPHA = 1.702
SWIGLU_LIMIT = 7.0
NORM_EPS = 1e-5

LANES = 128
SUBLANES = 8
VMEM_LIMIT = 56 * 1024 * 1024
N_DMA_PRIORITIES = 2

SSM_TILE_CH = 128
SSM_TIME_CHUNK = 256
MOE_SUB = 256
MOE_SB_SUBS = 8
MOE_SB_ROWS = MOE_SUB * MOE_SB_SUBS
MOE_GROUP = 4
MOE_CHUNK = 256


def _cparams(n_axes):
    return pltpu.CompilerParams(dimension_semantics=("arbitrary",) * n_axes, vmem_limit_bytes=VMEM_LIMIT)


def _mod_kernel(c_ref, w_ref, b_ref, o_ref):
    c = c_ref[...]
    cond = c * jax.nn.sigmoid(c)
    o_ref[...] = jnp.dot(cond.astype(bf16), w_ref[...].astype(bf16), preferred_element_type=f32) + b_ref[...]


def _modulation(c, ada_w, ada_b, layer):
    bsz, d = c.shape
    n = ada_w.shape[-1]
    tn = 1024
    return pl.pallas_call(
        _mod_kernel,
        grid=(n // tn,),
        in_specs=[
            pl.BlockSpec((bsz, d), lambda j: (0, 0)),
            pl.BlockSpec((None, d, tn), lambda j: (layer, 0, j)),
            pl.BlockSpec((None, 1, tn), lambda j: (layer, 0, j)),
        ],
        out_specs=pl.BlockSpec((bsz, tn), lambda j: (0, j)),
        out_shape=jax.ShapeDtypeStruct((bsz, n), f32),
        compiler_params=_cparams(1),
        name="adaln_mod",
    )(c, ada_w, ada_b.reshape(ada_b.shape[0], 1, n))


def _norm_mod(x_ref, gain_ref, sc_ref, sh_ref):
    x = x_ref[0]
    ms = jnp.mean(x * x, axis=-1, keepdims=True)
    y = x * lax.rsqrt(ms + NORM_EPS) * gain_ref[...]
    return y * (1.0 + sc_ref[0]) + sh_ref[0]


def _norm_kernel(x_ref, gain_ref, sc_ref, sh_ref, h_ref):
    h_ref[0] = _norm_mod(x_ref, gain_ref, sc_ref, sh_ref).astype(h_ref.dtype)


def _split_bf16(v):
    hi = v.astype(bf16)
    lo = (v - hi.astype(f32)).astype(bf16)
    return hi, lo


def _norm_router_kernel(x_ref, gain_ref, sc_ref, sh_ref, wr_ref, br_ref, h_ref, idx_ref, gate_ref):
    h = _norm_mod(x_ref, gain_ref, sc_ref, sh_ref)
    ts, d = h.shape
    n_col = d // LANES
    for j in range(n_col):
        h_ref[pl.ds(j, ts, stride=n_col), :] = h[:, j * LANES:(j + 1) * LANES]
    h_hi, h_lo = _split_bf16(h)
    w_hi, w_lo = _split_bf16(wr_ref[...])
    dot = functools.partial(jnp.dot, preferred_element_type=f32)
    logits = dot(h_hi, w_hi) + (dot(h_hi, w_lo) + dot(h_lo, w_hi)) + br_ref[...]
    rows, n_exp = logits.shape
    col = lax.broadcasted_iota(i32, (rows, n_exp), 1)
    work = logits
    tops, idxs = [], []
    for _ in range(TOP_K):
        m = jnp.max(work, axis=-1, keepdims=True)
        idx = jnp.min(jnp.where(work == m, col, n_exp), axis=-1, keepdims=True)
        work = jnp.where(col == idx, -jnp.inf, work)
        tops.append(m)
        idxs.append(idx)
    es = [jnp.exp(t - tops[0]) for t in tops]
    denom = es[0] + es[1] + es[2] + es[3]
    lane = lax.broadcasted_iota(i32, (rows, LANES), 1)
    idx_out = jnp.zeros((rows, LANES), i32)
    gate_out = jnp.zeros((rows, LANES), f32)
    for k in range(TOP_K):
        idx_out = jnp.where(lane == k, idxs[k], idx_out)
        gate_out = jnp.where(lane == k, es[k] / denom, gate_out)
    idx_ref[0] = idx_out
    gate_ref[0] = gate_out


def _norm(x, gain, mod3, sc_chunk, sh_chunk, out_dtype, router=None):
    bsz, seq, d = x.shape
    ts = 512
    in_specs = [
        pl.BlockSpec((1, ts, d), lambda b, s: (b, s, 0)),
        pl.BlockSpec((1, d), lambda b, s: (0, 0)),
        pl.BlockSpec((1, 1, d), lambda b, s: (b, 0, sc_chunk)),
        pl.BlockSpec((1, 1, d), lambda b, s: (b, 0, sh_chunk)),
    ]
    h_spec = pl.BlockSpec((1, ts, d), lambda b, s: (b, s, 0))
    args = [x, gain.reshape(1, d), mod3, mod3]
    if router is None:
        return pl.pallas_call(
            _norm_kernel, grid=(bsz, seq // ts), in_specs=in_specs, out_specs=h_spec,
            out_shape=jax.ShapeDtypeStruct((bsz, seq, d), out_dtype),
            compiler_params=_cparams(2), name="norm_mod",
        )(*args)
    w_router, b_router, layer = router
    n_exp = w_router.shape[-1]
    in_specs += [
        pl.BlockSpec((None, d, n_exp), lambda b, s: (layer, 0, 0)),
        pl.BlockSpec((None, 1, n_exp), lambda b, s: (layer, 0, 0)),
    ]
    lane_spec = pl.BlockSpec((1, ts, LANES), lambda b, s: (b, s, 0))
    n_col = d // LANES
    h3_spec = pl.BlockSpec((ts * n_col, LANES), lambda b, s: (b * (seq // ts) + s, 0))
    return pl.pallas_call(
        _norm_router_kernel, grid=(bsz, seq // ts), in_specs=in_specs,
        out_specs=[h3_spec, lane_spec, lane_spec],
        out_shape=[jax.ShapeDtypeStruct((bsz * seq * n_col, LANES), f32),
                   jax.ShapeDtypeStruct((bsz, seq, LANES), i32),
                   jax.ShapeDtypeStruct((bsz, seq, LANES), f32)],
        compiler_params=_cparams(2), name="norm_router",
    )(*args, w_router, b_router.reshape(b_router.shape[0], 1, n_exp))


def _dense_kernel(*refs, n_w, n_extra, epilogue):
    x_ref = refs[0]
    w_refs = refs[1:1 + n_w]
    b_refs = refs[1 + n_w:1 + 2 * n_w]
    e_refs = refs[1 + 2 * n_w:1 + 2 * n_w + n_extra]
    o_ref = refs[1 + 2 * n_w + n_extra]
    wbf_refs = refs[2 + 2 * n_w + n_extra:]

    @pl.when(pl.program_id(1) == 0)
    def _():
        for w_ref, wbf in zip(w_refs, wbf_refs):
            wbf[...] = w_ref[...].astype(bf16)

    x = x_ref[...]
    accs = [jnp.dot(x, wbf[...], preferred_element_type=f32) + b_ref[...] for wbf, b_ref in zip(wbf_refs, b_refs)]
    o_ref[...] = epilogue(accs, [e[...] for e in e_refs]).astype(o_ref.dtype)


def _dense(x, ws, bs, layer, epilogue, extras, out_dtype, tm, tn, name):
    m, k = x.shape
    n = ws[0].shape[-1]
    in_specs = [pl.BlockSpec((tm, k), lambda j, i: (i, 0))]
    in_specs += [pl.BlockSpec((None, k, tn), lambda j, i: (layer, 0, j)) for _ in ws]
    in_specs += [pl.BlockSpec((None, 1, tn), lambda j, i: (layer, 0, j)) for _ in bs]
    in_specs += [spec for _, spec in extras]
    args = [x] + list(ws) + [b.reshape(b.shape[0], 1, n) for b in bs] + [a for a, _ in extras]
    return pl.pallas_call(
        functools.partial(_dense_kernel, n_w=len(ws), n_extra=len(extras), epilogue=epilogue),
        grid=(n // tn, m // tm),
        in_specs=in_specs,
        out_specs=pl.BlockSpec((tm, tn), lambda j, i: (i, j)),
        out_shape=jax.ShapeDtypeStruct((m, n), out_dtype),
        scratch_shapes=[pltpu.VMEM((k, tn), bf16) for _ in ws],
        compiler_params=_cparams(2), name=name,
    )(*args)


def _gated_residual_extras(x2d, mod3, gate_chunk, d, tm, tn, seq):
    tiles_per_batch = seq // tm
    return [
        (x2d, pl.BlockSpec((tm, tn), lambda j, i: (i, j))),
        (mod3, pl.BlockSpec((None, 1, tn), lambda j, i: (i // tiles_per_batch, 0, gate_chunk * (d // tn) + j))),
    ]


def _attn_kernel(sinks_ref, q_ref, kp_ref, kc_ref, vp_ref, vc_ref, qg_ref, kg_ref, bias_ref, o_ref):
    n = pl.program_id(1)
    lane = lax.broadcasted_iota(i32, (1, LANES), 1)
    lo = lane < HEAD_DIM
    bq = q_ref.shape[1]

    def halfnorm(v, gain):
        sq = v * v
        s_lo = jnp.sum(jnp.where(lo, sq, 0.0), axis=-1, keepdims=True)
        s_hi = jnp.sum(jnp.where(lo, 0.0, sq), axis=-1, keepdims=True)
        ms = jnp.where(lo, s_lo, s_hi) * (1.0 / HEAD_DIM)
        return v * lax.rsqrt(ms + NORM_EPS) * gain

    k_all = jnp.concatenate([kp_ref[0], kc_ref[0]], axis=0)
    v_all = jnp.concatenate([vp_ref[0], vc_ref[0]], axis=0)
    kcol = lax.broadcasted_iota(i32, (1, 2 * bq), 1)
    key_ok = jnp.logical_or(kcol >= bq, n > 0)
    qgain = qg_ref[...]
    kgain = kg_ref[...]
    pairs = Q_PER_KV // 2
    for c in range(N_KV_HEADS // 2):
        kn = halfnorm(k_all[:, c * LANES:(c + 1) * LANES], kgain)
        kr = pltpu.roll(kn, HEAD_DIM, 1)
        vn = v_all[:, c * LANES:(c + 1) * LANES]
        vr = pltpu.roll(vn, HEAD_DIM, 1)
        for half in range(2):
            g = 2 * c + half
            k_src, k_rot = (kn, kr) if half == 0 else (kr, kn)
            v_src, v_rot = (vn, vr) if half == 0 else (vr, vn)
            k_par = [jnp.where(lo, k_src, 0.0).astype(bf16), jnp.where(lo, 0.0, k_rot).astype(bf16)]
            v_par = [jnp.where(lo, v_src, 0.0).astype(bf16), jnp.where(lo, 0.0, v_rot).astype(bf16)]
            qs = [halfnorm(q_ref[0, :, (g * pairs + p) * LANES:(g * pairs + p + 1) * LANES], qgain)
                  for p in range(pairs)]
            qg = (jnp.concatenate(qs, axis=0) * (1.0 / math.sqrt(HEAD_DIM))).astype(bf16)
            acc = None
            for par in range(2):
                s = lax.dot_general(qg, k_par[par], (((1,), (1,)), ((), ())), preferred_element_type=f32)
                s = s + bias_ref[g, par]
                s = jnp.where(key_ok, s, -jnp.inf)
                sink = jnp.concatenate(
                    [jnp.full((bq, 1), sinks_ref[g * Q_PER_KV + 2 * p + par], f32) for p in range(pairs)], axis=0)
                m = jnp.maximum(jnp.max(s, axis=-1, keepdims=True), sink)
                e = jnp.exp(s - m)
                den = jnp.sum(e, axis=-1, keepdims=True) + jnp.exp(sink - m)
                probs = (e * (1.0 / den)).astype(bf16)
                o = jnp.dot(probs, v_par[par], preferred_element_type=f32)
                acc = o if acc is None else acc + o
            for p in range(pairs):
                col = (g * pairs + p) * LANES
                o_ref[0, :, col:col + LANES] = acc[p * bq:(p + 1) * bq].astype(o_ref.dtype)


def _t5_bucket(dist):
    nn = np.maximum(dist, 0)
    max_exact = NUM_BUCKETS // 2
    large = max_exact + (np.log(np.maximum(nn, 1) / max_exact) / np.log(MAX_DISTANCE / max_exact)
                         * (NUM_BUCKETS - max_exact)).astype(np.int32)
    large = np.minimum(large, NUM_BUCKETS - 1)
    return np.where(nn < max_exact, nn, large).astype(np.int32)


def _attention(qkv, q_gain, k_gain, sinks, rel_bias):
    bsz, seq, _ = qkv.shape
    bq = WINDOW
    q_dim = N_Q_HEADS * HEAD_DIM
    kv_dim = N_KV_HEADS * HEAD_DIM
    pairs = Q_PER_KV // 2
    ql = np.arange(bq)[:, None]
    kl = np.arange(2 * bq)[None, :]
    dist = ql + bq - kl
    in_window = (dist >= 0) & (dist < WINDOW)
    bias = jnp.take(rel_bias.astype(f32), jnp.asarray(_t5_bucket(dist)), axis=0)
    bias = jnp.where(jnp.asarray(in_window)[:, :, None], bias, -jnp.inf)
    bias = jnp.transpose(bias, (2, 0, 1)).reshape(N_KV_HEADS, pairs, 2, bq, 2 * bq)
    bias = jnp.transpose(bias, (0, 2, 1, 3, 4)).reshape(N_KV_HEADS, 2, pairs * bq, 2 * bq)
    gain2 = lambda gn: jnp.concatenate([gn, gn]).reshape(1, LANES).astype(f32)
    k_blk = q_dim // kv_dim
    grid_spec = pltpu.PrefetchScalarGridSpec(
        num_scalar_prefetch=1,
        grid=(bsz, seq // bq),
        in_specs=[
            pl.BlockSpec((1, bq, q_dim), lambda b, n, s: (b, n, 0)),
            pl.BlockSpec((1, bq, kv_dim), lambda b, n, s: (b, jnp.maximum(n - 1, 0), k_blk)),
            pl.BlockSpec((1, bq, kv_dim), lambda b, n, s: (b, n, k_blk)),
            pl.BlockSpec((1, bq, kv_dim), lambda b, n, s: (b, jnp.maximum(n - 1, 0), k_blk + 1)),
            pl.BlockSpec((1, bq, kv_dim), lambda b, n, s: (b, n, k_blk + 1)),
            pl.BlockSpec((1, LANES), lambda b, n, s: (0, 0)),
            pl.BlockSpec((1, LANES), lambda b, n, s: (0, 0)),
            pl.BlockSpec((N_KV_HEADS, 2, pairs * bq, 2 * bq), lambda b, n, s: (0, 0, 0, 0)),
        ],
        out_specs=pl.BlockSpec((1, bq, q_dim), lambda b, n, s: (b, n, 0)),
    )
    return pl.pallas_call(
        _attn_kernel, grid_spec=grid_spec,
        out_shape=jax.ShapeDtypeStruct((bsz, seq, q_dim), bf16),
        compiler_params=_cparams(2), name="swa_attention",
    )(sinks.astype(f32), qkv, qkv, qkv, qkv, qkv, gain2(q_gain), gain2(k_gain), bias)


def _gelu_tanh(y):
    return 0.5 * y * (1.0 + jnp.tanh(math.sqrt(2.0 / math.pi) * (y + 0.044715 * (y * y * y))))


def _ssm_kernel(u_ref, bm_ref, cm_ref, are_ref, aim_ref, d_ref, y_ref, us, xs, st, ys):
    bsz, ts, _ = u_ref.shape
    n_state = are_ref.shape[-1]

    @pl.when(pl.program_id(1) == 0)
    def _():
        st[...] = jnp.zeros_like(st)

    n_slab = us.shape[0]
    for h in range(n_slab):
        for b in range(bsz):
            us[h, pl.ds(b, ts, stride=bsz), :] = u_ref[b, :, h * LANES:(h + 1) * LANES]
    u = jnp.concatenate([us[h] for h in range(n_slab)], axis=1)
    xs[...] = jnp.dot(u.astype(bf16), bm_ref[0], preferred_element_type=f32)
    a_re = jnp.broadcast_to(are_ref[0], (bsz, n_state))
    a_im = jnp.broadcast_to(aim_ref[0], (bsz, n_state))

    def step(t, carry):
        s_re, s_im = carry
        r0 = pl.multiple_of(t * bsz, bsz)
        x_re = xs[pl.ds(r0, bsz), 0:n_state]
        x_im = xs[pl.ds(r0, bsz), n_state:2 * n_state]
        n_re = a_re * s_re - a_im * s_im + x_re
        n_im = a_re * s_im + a_im * s_re + x_im
        xs[pl.ds(r0, bsz), 0:n_state] = n_re
        xs[pl.ds(r0, bsz), n_state:2 * n_state] = n_im
        return n_re, n_im

    s_re, s_im = lax.fori_loop(0, ts, step, (st[0], st[1]), unroll=8)
    st[0] = s_re
    st[1] = s_im
    y = jnp.dot(xs[...].astype(bf16), cm_ref[0], preferred_element_type=f32) + d_ref[0] * u
    y = _gelu_tanh(y)
    for h in range(n_slab):
        ys[h] = y[:, h * LANES:(h + 1) * LANES]
        for b in range(bsz):
            y_ref[b, :, h * LANES:(h + 1) * LANES] = ys[h, pl.ds(b, ts, stride=bsz), :].astype(y_ref.dtype)


def _ssm(h, lam_re, lam_im, log_dt, b_re, b_im, c_re, c_im, d_skip):
    bsz, seq, width = h.shape
    assert bsz == SUBLANES
    n_grp, n_st = lam_re.shape
    tile_ch = SSM_TILE_CH
    gpt = tile_ch // SSM_GROUP_CH
    n_tiles = width // tile_ch
    ns = gpt * n_st
    dt = jnp.exp(log_dt.astype(f32))[:, None]
    lr, li = lam_re.astype(f32), lam_im.astype(f32)
    mag = jnp.exp(lr * dt)
    ab_re, ab_im = mag * jnp.cos(li * dt), mag * jnp.sin(li * dt)
    den = lr * lr + li * li
    nr, ni = ab_re - 1.0, ab_im
    f_re = (nr * lr + ni * li) / den
    f_im = (ni * lr - nr * li) / den
    br, bi = b_re.astype(f32), b_im.astype(f32)
    bb_re = f_re[..., None] * br - f_im[..., None] * bi
    bb_im = f_re[..., None] * bi + f_im[..., None] * br
    eye = jnp.eye(gpt, dtype=f32)

    def blockdiag_in(bb):
        t = bb.reshape(n_tiles, gpt, n_st, SSM_GROUP_CH)
        return jnp.einsum("tgpc,gh->tgchp", t, eye).reshape(n_tiles, tile_ch, ns)

    def blockdiag_out(cc):
        t = cc.reshape(n_tiles, gpt, SSM_GROUP_CH, n_st)
        return jnp.einsum("tgcp,gh->tgphc", t, eye).reshape(n_tiles, ns, tile_ch)

    bm = jnp.concatenate([blockdiag_in(bb_re), blockdiag_in(bb_im)], axis=-1).astype(bf16)
    cm = jnp.concatenate([blockdiag_out(c_re.astype(f32)), -blockdiag_out(c_im.astype(f32))], axis=1).astype(bf16)
    a_re = ab_re.reshape(n_tiles, 1, ns)
    a_im = ab_im.reshape(n_tiles, 1, ns)
    dsk = d_skip.astype(f32).reshape(n_tiles, 1, tile_ch)
    ts = SSM_TIME_CHUNK
    return pl.pallas_call(
        _ssm_kernel,
        grid=(n_tiles, seq // ts),
        in_specs=[
            pl.BlockSpec((bsz, ts, tile_ch), lambda g, t: (0, t, g)),
            pl.BlockSpec((1, tile_ch, 2 * ns), lambda g, t: (g, 0, 0)),
            pl.BlockSpec((1, 2 * ns, tile_ch), lambda g, t: (g, 0, 0)),
            pl.BlockSpec((1, 1, ns), lambda g, t: (g, 0, 0)),
            pl.BlockSpec((1, 1, ns), lambda g, t: (g, 0, 0)),
            pl.BlockSpec((1, 1, tile_ch), lambda g, t: (g, 0, 0)),
        ],
        out_specs=pl.BlockSpec((bsz, ts, tile_ch), lambda g, t: (0, t, g)),
        out_shape=jax.ShapeDtypeStruct((bsz, seq, width), bf16),
        scratch_shapes=[
            pltpu.VMEM((tile_ch // LANES, ts * bsz, LANES), f32),
            pltpu.VMEM((ts * bsz, 2 * ns), f32),
            pltpu.VMEM((2, bsz, ns), f32),
            pltpu.VMEM((tile_ch // LANES, ts * bsz, LANES), f32),
        ],
        compiler_params=_cparams(2), name="s5_ssm",
    )(h, bm, cm, a_re, a_im, dsk)


GATHER_PITCH = 24


def _swiglu(g, l):
    glu = jnp.minimum(g, SWIGLU_LIMIT)
    lin = jnp.clip(l, -SWIGLU_LIMIT, SWIGLU_LIMIT)
    return glu * jax.nn.sigmoid(SWIGLU_ALPHA * glu) * (lin + 1.0)


def _for_row_groups(nsub, cast, fn):
    n_big = nsub // MOE_GROUP
    big_rows = MOE_GROUP * MOE_SUB

    @pl.when(n_big > 0)
    def _():
        fn(0, big_rows, cast())

    @pl.when(n_big == 0)
    def _():
        fn(0, MOE_SUB, cast())

    def big(i, carry):
        fn(pl.multiple_of(i * big_rows, big_rows), big_rows, None)
        return carry

    def small(i, carry):
        fn(pl.multiple_of(i * MOE_SUB, MOE_SUB), MOE_SUB, None)
        return carry

    lax.fori_loop(1, n_big, big, 0)
    lax.fori_loop(jnp.maximum(n_big * MOE_GROUP, 1), nsub, small, 0)


def _moe_kernel(sbe_ref, sbn_ref, idx0_ref, idxn_ref, h_hbm, wg_ref, wl_ref, bg_ref, bl_ref, wd_ref, bd_ref, o_ref,
                xbf, stg, sem, a_scr, wg_bf, wl_bf, wd_bf, *, n_chunks):
    s = pl.program_id(0)
    c = pl.program_id(1)
    n_sb = pl.num_programs(0)
    nsub = sbn_ref[s]
    slot = lax.rem(s, 2)
    nsub_next = jnp.where(s + 1 < n_sb, sbn_ref[jnp.minimum(s + 1, n_sb - 1)], 0)
    n_col = xbf.shape[2] // LANES
    stg_rows = MOE_SUB * GATHER_PITCH

    def issue(idx_ref, off, st_slot):
        def body(r2, carry):
            for prio in range(N_DMA_PRIORITIES):
                r = r2 * N_DMA_PRIORITIES + prio
                src = pl.multiple_of(idx_ref[0, 0, off + r] * n_col, SUBLANES)
                dst = pl.multiple_of(st_slot * stg_rows + r * GATHER_PITCH, SUBLANES)
                pltpu.make_async_copy(h_hbm.at[pl.ds(src, n_col)], stg.at[pl.ds(dst, n_col)],
                                      sem.at[st_slot]).start(priority=prio)
            return carry

        lax.fori_loop(0, MOE_SUB // N_DMA_PRIORITIES, body, 0, unroll=4)

    def finish(st_slot, x_slot, sub):
        base = pl.multiple_of(st_slot * stg_rows, SUBLANES)
        pltpu.make_async_copy(h_hbm.at[pl.ds(0, MOE_SUB * n_col)], stg.at[pl.ds(base, MOE_SUB * n_col)],
                              sem.at[st_slot]).wait()
        r0 = pl.multiple_of(sub * MOE_SUB, MOE_SUB)
        for j in range(n_col):
            piece = stg[pl.ds(base + j, MOE_SUB, stride=GATHER_PITCH), :]
            xbf[x_slot, pl.ds(r0, MOE_SUB), j * LANES:(j + 1) * LANES] = piece.astype(bf16)

    @pl.when(jnp.logical_and(s == 0, c == 0))
    def _():
        def first(j, carry):
            issue(idx0_ref, j * MOE_SUB, 0)
            finish(0, 0, j)
            return carry

        lax.fori_loop(0, nsub, first, 0)

    @pl.when(jnp.logical_and(c >= 1, c - 1 < nsub_next))
    def _():
        finish(lax.rem(c - 1, 2), 1 - slot, c - 1)

    @pl.when(c < nsub_next)
    def _():
        issue(idxn_ref, 0, lax.rem(c, 2))

    @pl.when(jnp.logical_and(c < n_chunks, nsub > 0))
    def _():
        def cast():
            wg, wl = wg_ref[...].astype(bf16), wl_ref[...].astype(bf16)
            wg_bf[...] = wg
            wl_bf[...] = wl
            return wg, wl

        def up(r, rows, weights):
            wg, wl = weights if weights is not None else (wg_bf[...], wl_bf[...])
            xs = xbf[slot, pl.ds(r, rows), :]
            g = jnp.dot(xs, wg, preferred_element_type=f32) + bg_ref[...]
            l = jnp.dot(xs, wl, preferred_element_type=f32) + bl_ref[...]
            a_scr[c, pl.ds(r, rows), :] = _swiglu(g, l).astype(bf16)

        _for_row_groups(nsub, cast, up)

    @pl.when(jnp.logical_and(c >= n_chunks, nsub > 0))
    def _():
        def cast():
            wd = wd_ref[...].astype(bf16)
            wd_bf[...] = wd
            return wd

        def down(r, rows, weights):
            wd = weights if weights is not None else wd_bf[...]
            a = jnp.concatenate([a_scr[cc, pl.ds(r, rows), :] for cc in range(n_chunks)], axis=1)
            o_ref[pl.ds(r, rows), :] = jnp.dot(a, wd, preferred_element_type=f32) + bd_ref[...]

        _for_row_groups(nsub, cast, down)

    @pl.when(c >= n_chunks)
    def _():
        def zero(i, carry):
            r = pl.multiple_of(i * MOE_SUB, MOE_SUB)
            o_ref[pl.ds(r, MOE_SUB), :] = jnp.zeros((MOE_SUB, o_ref.shape[1]), f32)
            return carry

        lax.fori_loop(nsub, MOE_SB_SUBS, zero, 0)


def _moe_experts(h_rows, n_col, row_tok, sb_expert, sb_nsub, w_gate_up, b_gate_up, w_down, b_down, layer):
    n_rows = row_tok.shape[0]
    d = n_col * LANES
    n_sb = n_rows // MOE_SB_ROWS
    d_exp = w_down.shape[2]
    n_chunks = d_exp // MOE_CHUNK
    n_out_chunks = d // MOE_CHUNK
    assert n_out_chunks == n_chunks
    last = n_chunks - 1

    def valid(s, n):
        return n[s] > 0

    def wg_map(s, c, e, n):
        return (layer, e[s], 0, jnp.where(valid(s, n), jnp.minimum(c, last), last))

    def wl_map(s, c, e, n):
        return (layer, e[s], 0, n_chunks + jnp.where(valid(s, n), jnp.minimum(c, last), last))

    def wd_map(s, c, e, n):
        return (layer, e[s], 0, jnp.where(valid(s, n), jnp.maximum(c - n_chunks, 0), last))

    def out_map(s, c, e, n):
        return (s, jnp.maximum(c - n_chunks, 0))

    def next_idx_map(s, c, e, n):
        return (jnp.minimum(s + 1, n_sb - 1) * MOE_SB_SUBS + jnp.minimum(c, MOE_SB_SUBS - 1), 0, 0)

    grid_spec = pltpu.PrefetchScalarGridSpec(
        num_scalar_prefetch=2,
        grid=(n_sb, n_chunks + n_out_chunks),
        in_specs=[
            pl.BlockSpec((1, 1, MOE_SB_ROWS), lambda s, c, e, n: (0, 0, 0), memory_space=pltpu.SMEM),
            pl.BlockSpec((1, 1, MOE_SUB), next_idx_map, memory_space=pltpu.SMEM),
            pl.BlockSpec(memory_space=pl.ANY),
            pl.BlockSpec((None, None, d, MOE_CHUNK), wg_map),
            pl.BlockSpec((None, None, d, MOE_CHUNK), wl_map),
            pl.BlockSpec((None, None, 1, MOE_CHUNK), wg_map),
            pl.BlockSpec((None, None, 1, MOE_CHUNK), wl_map),
            pl.BlockSpec((None, None, d_exp, MOE_CHUNK), wd_map),
            pl.BlockSpec((None, None, 1, MOE_CHUNK), wd_map),
        ],
        out_specs=pl.BlockSpec((MOE_SB_ROWS, MOE_CHUNK), out_map),
        scratch_shapes=[
            pltpu.VMEM((2, MOE_SB_ROWS, d), bf16),
            pltpu.VMEM((2 * MOE_SUB * GATHER_PITCH, LANES), f32),
            pltpu.SemaphoreType.DMA((2,)),
            pltpu.VMEM((n_chunks, MOE_SB_ROWS, MOE_CHUNK), bf16),
            pltpu.VMEM((d, MOE_CHUNK), bf16),
            pltpu.VMEM((d, MOE_CHUNK), bf16),
            pltpu.VMEM((d_exp, MOE_CHUNK), bf16),
        ],
    )
    n_l, n_e = b_gate_up.shape[:2]
    return pl.pallas_call(
        functools.partial(_moe_kernel, n_chunks=n_chunks), grid_spec=grid_spec,
        out_shape=jax.ShapeDtypeStruct((n_rows, d), f32),
        compiler_params=_cparams(2), name="moe_experts",
    )(sb_expert, sb_nsub, row_tok.reshape(n_sb, 1, MOE_SB_ROWS), row_tok.reshape(n_sb * MOE_SB_SUBS, 1, MOE_SUB),
      h_rows, w_gate_up, w_gate_up,
      b_gate_up.reshape(n_l, n_e, 1, -1), b_gate_up.reshape(n_l, n_e, 1, -1),
      w_down, b_down.reshape(n_l, n_e, 1, -1))


def _combine_kernel(dest_ref, dest_next_ref, y_hbm, gate_ref, x_ref, g2_ref, o_ref, ybuf, sem):
    i = pl.program_id(0)
    n = pl.num_programs(0)
    tm = x_ref.shape[0]
    slot = lax.rem(i, 2)

    def issue(dest_blk, sl):
        for k in range(TOP_K):
            def body(t2, carry, k=k):
                for prio in range(N_DMA_PRIORITIES):
                    t = t2 * N_DMA_PRIORITIES + prio
                    pltpu.make_async_copy(y_hbm.at[pl.ds(dest_blk[0, 0, k * tm + t], 1)],
                                          ybuf.at[sl * TOP_K + k, pl.ds(t, 1)], sem.at[sl]).start(priority=prio)
                return carry

            lax.fori_loop(0, tm // N_DMA_PRIORITIES, body, 0, unroll=4)

    @pl.when(i == 0)
    def _():
        issue(dest_ref, 0)

    @pl.when(i + 1 < n)
    def _():
        issue(dest_next_ref, 1 - slot)

    for k in range(TOP_K):
        pltpu.make_async_copy(y_hbm.at[pl.ds(0, tm)], ybuf.at[slot * TOP_K + k], sem.at[slot]).wait()
    gates = gate_ref[...]
    acc = gates[:, 0:1] * ybuf[slot * TOP_K]
    for k in range(1, TOP_K):
        acc = acc + gates[:, k:k + 1] * ybuf[slot * TOP_K + k]
    o_ref[...] = x_ref[...] + g2_ref[...] * acc


def _moe_combine(y_sorted, dest, gates, x2d, mod3, gate_chunk, seq):
    n_tok, d = x2d.shape
    tm = 128
    n_tiles = n_tok // tm
    dest_km = jnp.transpose(dest.reshape(TOP_K, n_tiles, tm), (1, 0, 2)).reshape(n_tiles, 1, TOP_K * tm)
    tiles_per_batch = seq // tm
    return pl.pallas_call(
        _combine_kernel,
        grid=(n_tiles,),
        in_specs=[
            pl.BlockSpec((1, 1, TOP_K * tm), lambda i: (i, 0, 0), memory_space=pltpu.SMEM),
            pl.BlockSpec((1, 1, TOP_K * tm), lambda i: (jnp.minimum(i + 1, n_tiles - 1), 0, 0),
                         memory_space=pltpu.SMEM),
            pl.BlockSpec(memory_space=pl.ANY),
            pl.BlockSpec((tm, LANES), lambda i: (i, 0)),
            pl.BlockSpec((tm, d), lambda i: (i, 0)),
            pl.BlockSpec((None, 1, d), lambda i: (i // tiles_per_batch, 0, gate_chunk)),
        ],
        out_specs=pl.BlockSpec((tm, d), lambda i: (i, 0)),
        out_shape=jax.ShapeDtypeStruct((n_tok, d), f32),
        scratch_shapes=[pltpu.VMEM((2 * TOP_K, tm, d), f32), pltpu.SemaphoreType.DMA((2,))],
        compiler_params=_cparams(1), name="moe_combine",
    )(dest_km, dest_km, y_sorted, gates, x2d, mod3)


def _routing_tables(idx_km):
    n_tok = idx_km.shape[1]
    n_assign = n_tok * TOP_K
    n_sb_max = (n_assign // MOE_SUB + N_EXPERTS + N_EXPERTS * (MOE_SB_SUBS - 1)) // MOE_SB_SUBS
    experts = jnp.arange(N_EXPERTS, dtype=i32)
    onehot = idx_km[:, :, None] == experts[None, None, :]
    per_tok = jnp.sum(onehot.astype(i32), axis=0)
    before = jnp.cumsum(per_tok, axis=0) - per_tok
    rank = jnp.sum(jnp.where(onehot, before[None], 0), axis=2)
    counts = before[-1] + per_tok[-1]
    n_sub_e = (counts + MOE_SUB - 1) // MOE_SUB
    n_sb_e = (n_sub_e + MOE_SB_SUBS - 1) // MOE_SB_SUBS
    sb_end = jnp.cumsum(n_sb_e)
    sb_start = sb_end - n_sb_e
    base = n_sub_e // jnp.maximum(n_sb_e, 1)
    rem = n_sub_e - base * n_sb_e
    tables = jnp.stack([sb_start, base, rem], axis=1).astype(bf16)
    looked = jnp.dot(onehot.astype(bf16).reshape(n_assign, N_EXPERTS), tables,
                     preferred_element_type=f32).astype(i32).reshape(TOP_K, n_tok, 3)
    a_start, a_base, a_rem = looked[..., 0], looked[..., 1], looked[..., 2]
    q = rank // MOE_SUB
    thr = a_rem * (a_base + 1)
    sb_local = jnp.where(q < thr, q // (a_base + 1), a_rem + (q - thr) // jnp.maximum(a_base, 1))
    sub_in = jnp.where(q < thr, q % (a_base + 1), (q - thr) % jnp.maximum(a_base, 1))
    dest = ((a_start + sb_local) * MOE_SB_ROWS + sub_in * MOE_SUB + rank % MOE_SUB).astype(i32)
    tok = jnp.broadcast_to(jnp.arange(n_tok, dtype=i32)[None, :], (TOP_K, n_tok))
    row_tok = jnp.zeros((n_sb_max * MOE_SB_ROWS,), i32).at[dest.reshape(-1)].set(
        tok.reshape(-1), unique_indices=True, mode="promise_in_bounds")
    s = jnp.arange(n_sb_max, dtype=i32)
    n_sb_total = sb_end[-1]
    sb_valid = s < n_sb_total
    sb_src = jnp.where(sb_valid, s, n_sb_total - 1).astype(i32)
    sb_e = jnp.minimum(jnp.searchsorted(sb_end, sb_src, side="right"), N_EXPERTS - 1).astype(i32)
    sb_local_s = sb_src - sb_start[sb_e]
    sb_nsub = jnp.where(sb_valid, base[sb_e] + (sb_local_s < rem[sb_e]).astype(i32), 0).astype(i32)
    return dest, row_tok, sb_e, sb_nsub


def kernel(x, c, rel_bias, norm_gain, ada_w, ada_b, attn_w_qkv, attn_b_qkv, attn_q_gain, attn_k_gain, attn_sinks, attn_w_o, attn_b_o, ssm_lam_re, ssm_lam_im, ssm_log_dt, ssm_b_re, ssm_b_im, ssm_c_re, ssm_c_im, ssm_d, ssm_w_glu_a, ssm_b_glu_a, ssm_w_glu_b, ssm_b_glu_b, moe_w_router, moe_b_router, moe_w_gate_up, moe_b_gate_up, moe_w_down, moe_b_down):
    bsz, seq, d = x.shape
    n_tok = bsz * seq
    depth = norm_gain.shape[0]
    tm, tn = 1024, 512
    SH1, SC1, G1, SH2, SC2, G2 = range(6)

    def gated(accs, extras):
        x_res, gate = extras
        return x_res + gate * accs[0]

    def gated_glu(accs, extras):
        x_res, gate = extras
        return x_res + gate * (accs[0] * jax.nn.sigmoid(accs[1]))

    for layer in range(depth):
        mod3 = _modulation(c, ada_w, ada_b, layer).reshape(bsz, 1, 6 * d)
        x2d = x.reshape(n_tok, d)
        res_extras = _gated_residual_extras(x2d, mod3, G1, d, tm, tn, seq)
        i = layer // 2
        if layer % 2 == 0:
            h = _norm(x, norm_gain[layer, 0], mod3, SC1, SH1, bf16)
            qkv = _dense(h.reshape(n_tok, d), [attn_w_qkv], [attn_b_qkv], i, lambda accs, extras: accs[0], [],
                         f32, tm, 512, "qkv_proj")
            o = _attention(qkv.reshape(bsz, seq, -1), attn_q_gain[i], attn_k_gain[i], attn_sinks[i], rel_bias)
            x2d = _dense(o.reshape(n_tok, -1), [attn_w_o], [attn_b_o], i, gated, res_extras, f32, tm, tn, "attn_out")
        else:
            h = _norm(x, norm_gain[layer, 0], mod3, SC1, SH1, f32)
            y = _ssm(h, ssm_lam_re[i], ssm_lam_im[i], ssm_log_dt[i], ssm_b_re[i], ssm_b_im[i],
                     ssm_c_re[i], ssm_c_im[i], ssm_d[i])
            x2d = _dense(y.reshape(n_tok, d), [ssm_w_glu_a, ssm_w_glu_b], [ssm_b_glu_a, ssm_b_glu_b], i,
                         gated_glu, res_extras, f32, tm, tn, "ssm_glu")
        x = x2d.reshape(bsz, seq, d)
        h, top_idx, gates = _norm(x, norm_gain[layer, 1], mod3, SC2, SH2, f32,
                                  router=(moe_w_router, moe_b_router, layer))
        idx_km = top_idx.reshape(n_tok, LANES)[:, :TOP_K].T
        dest, row_tok, sb_e, sb_nsub = _routing_tables(idx_km)
        y_sorted = _moe_experts(h, d // LANES, row_tok, sb_e, sb_nsub, moe_w_gate_up, moe_b_gate_up,
                                moe_w_down, moe_b_down, layer)
        x = _moe_combine(y_sorted, dest, gates.reshape(n_tok, LANES), x2d, mod3, G2, seq).reshape(bsz, seq, d)
    return x
```

```python
import functools
import math

import numpy as np
import jax
import jax.numpy as jnp
from jax import lax
from jax.experimental import pallas as pl
from jax.experimental.pallas import tpu as pltpu

f32 = jnp.float32
bf16 = jnp.bfloat16
i32 = jnp.int32

N_Q_HEADS = 32
N_KV_HEADS = 4
HEAD_DIM = 64
Q_PER_KV = N_Q_HEADS // N_KV_HEADS
WINDOW = 128
NUM_BUCKETS = 32
MAX_DISTANCE = 128
SSM_GROUP_CH = 16
SSM_STATE = 64
N_EXPERTS = 32
TOP_K = 4
SWIGLU_ALPHA = 1.702
SWIGLU_LIMIT = 7.0
NORM_EPS = 1e-5

LANES = 128
SUBLANES = 8
VMEM_LIMIT = 56 * 1024 * 1024
MOE_VMEM_LIMIT = 60 * 1024 * 1024
N_DMA_PRIORITIES = 2

SSM_TILE_CH = 128
SSM_TIME_CHUNK = 256
MOE_SUB = 256
MOE_SB_SUBS = 8
MOE_SB_ROWS = MOE_SUB * MOE_SB_SUBS
MOE_GROUP = 4
MOE_CHUNK = 256
MOE_DOWN_CHUNK = 512


def _cparams(n_axes, vmem_limit=VMEM_LIMIT):
    return pltpu.CompilerParams(dimension_semantics=("arbitrary",) * n_axes, vmem_limit_bytes=vmem_limit)


def _mod_kernel(c_ref, w_ref, b_ref, o_ref):
    c = c_ref[...]
    cond = c * jax.nn.sigmoid(c)
    o_ref[...] = jnp.dot(cond.astype(bf16), w_ref[...].astype(bf16), preferred_element_type=f32) + b_ref[...]


def _modulation(c, ada_w, ada_b, layer):
    bsz, d = c.shape
    n = ada_w.shape[-1]
    tn = 1024
    return pl.pallas_call(
        _mod_kernel,
        grid=(n // tn,),
        in_specs=[
            pl.BlockSpec((bsz, d), lambda j: (0, 0)),
            pl.BlockSpec((None, d, tn), lambda j: (layer, 0, j)),
            pl.BlockSpec((None, 1, tn), lambda j: (layer, 0, j)),
        ],
        out_specs=pl.BlockSpec((bsz, tn), lambda j: (0, j)),
        out_shape=jax.ShapeDtypeStruct((bsz, n), f32),
        compiler_params=_cparams(1),
        name="adaln_mod",
    )(c, ada_w, ada_b.reshape(ada_b.shape[0], 1, n))


def _norm_mod(x_ref, gain_ref, sc_ref, sh_ref):
    x = x_ref[0]
    ms = jnp.mean(x * x, axis=-1, keepdims=True)
    y = x * lax.rsqrt(ms + NORM_EPS) * gain_ref[...]
    return y * (1.0 + sc_ref[0]) + sh_ref[0]


def _norm_kernel(x_ref, gain_ref, sc_ref, sh_ref, h_ref):
    h_ref[0] = _norm_mod(x_ref, gain_ref, sc_ref, sh_ref).astype(h_ref.dtype)


def _split_bf16(v):
    hi = v.astype(bf16)
    lo = (v - hi.astype(f32)).astype(bf16)
    return hi, lo


def _norm_router_kernel(x_ref, gain_ref, sc_ref, sh_ref, wr_ref, br_ref, h_ref, idx_ref, gate_ref):
    h = _norm_mod(x_ref, gain_ref, sc_ref, sh_ref)
    ts, d = h.shape
    n_col = d // LANES
    for j in range(n_col):
        h_ref[pl.ds(j, ts, stride=n_col), :] = h[:, j * LANES:(j + 1) * LANES]
    h_hi, h_lo = _split_bf16(h)
    w_hi, w_lo = _split_bf16(wr_ref[...])
    dot = functools.partial(jnp.dot, preferred_element_type=f32)
    logits = dot(h_hi, w_hi) + (dot(h_hi, w_lo) + dot(h_lo, w_hi)) + br_ref[...]
    rows, n_exp = logits.shape
    col = lax.broadcasted_iota(i32, (rows, n_exp), 1)
    work = logits
    tops, idxs = [], []
    for _ in range(TOP_K):
        m = jnp.max(work, axis=-1, keepdims=True)
        idx = jnp.min(jnp.where(work == m, col, n_exp), axis=-1, keepdims=True)
        work = jnp.where(col == idx, -jnp.inf, work)
        tops.append(m)
        idxs.append(idx)
    es = [jnp.exp(t - tops[0]) for t in tops]
    denom = es[0] + es[1] + es[2] + es[3]
    lane = lax.broadcasted_iota(i32, (rows, LANES), 1)
    idx_out = jnp.zeros((rows, LANES), i32)
    gate_out = jnp.zeros((rows, LANES), f32)
    for k in range(TOP_K):
        idx_out = jnp.where(lane == k, idxs[k], idx_out)
        gate_out = jnp.where(lane == k, es[k] / denom, gate_out)
    idx_ref[0] = idx_out
    gate_ref[0] = gate_out


def _norm(x, gain, mod3, sc_chunk, sh_chunk, out_dtype, router=None):
    bsz, seq, d = x.shape
    ts = 512
    in_specs = [
        pl.BlockSpec((1, ts, d), lambda b, s: (b, s, 0)),
        pl.BlockSpec((1, d), lambda b, s: (0, 0)),
        pl.BlockSpec((1, 1, d), lambda b, s: (b, 0, sc_chunk)),
        pl.BlockSpec((1, 1, d), lambda b, s: (b, 0, sh_chunk)),
    ]
    h_spec = pl.BlockSpec((1, ts, d), lambda b, s: (b, s, 0))
    args = [x, gain.reshape(1, d), mod3, mod3]
    if router is None:
        return pl.pallas_call(
            _norm_kernel, grid=(bsz, seq // ts), in_specs=in_specs, out_specs=h_spec,
            out_shape=jax.ShapeDtypeStruct((bsz, seq, d), out_dtype),
            compiler_params=_cparams(2), name="norm_mod",
        )(*args)
    w_router, b_router, layer = router
    n_exp = w_router.shape[-1]
    in_specs += [
        pl.BlockSpec((None, d, n_exp), lambda b, s: (layer, 0, 0)),
        pl.BlockSpec((None, 1, n_exp), lambda b, s: (layer, 0, 0)),
    ]
    lane_spec = pl.BlockSpec((1, ts, LANES), lambda b, s: (b, s, 0))
    n_col = d // LANES
    h3_spec = pl.BlockSpec((ts * n_col, LANES), lambda b, s: (b * (seq // ts) + s, 0))
    return pl.pallas_call(
        _norm_router_kernel, grid=(bsz, seq // ts), in_specs=in_specs,
        out_specs=[h3_spec, lane_spec, lane_spec],
        out_shape=[jax.ShapeDtypeStruct((bsz * seq * n_col, LANES), f32),
                   jax.ShapeDtypeStruct((bsz, seq, LANES), i32),
                   jax.ShapeDtypeStruct((bsz, seq, LANES), f32)],
        compiler_params=_cparams(2), name="norm_router",
    )(*args, w_router, b_router.reshape(b_router.shape[0], 1, n_exp))


def _dense_kernel(*refs, n_w, n_extra, epilogue):
    x_ref = refs[0]
    w_refs = refs[1:1 + n_w]
    b_refs = refs[1 + n_w:1 + 2 * n_w]
    e_refs = refs[1 + 2 * n_w:1 + 2 * n_w + n_extra]
    o_ref = refs[1 + 2 * n_w + n_extra]
    wbf_refs = refs[2 + 2 * n_w + n_extra:]

    @pl.when(pl.program_id(1) == 0)
    def _():
        for w_ref, wbf in zip(w_refs, wbf_refs):
            wbf[...] = w_ref[...].astype(bf16)

    x = x_ref[...]
    accs = [jnp.dot(x, wbf[...], preferred_element_type=f32) + b_ref[...] for wbf, b_ref in zip(wbf_refs, b_refs)]
    o_ref[...] = epilogue(accs, [e[...] for e in e_refs]).astype(o_ref.dtype)


def _dense(x, ws, bs, layer, epilogue, extras, out_dtype, tm, tn, name):
    m, k = x.shape
    n = ws[0].shape[-1]
    in_specs = [pl.BlockSpec((tm, k), lambda j, i: (i, 0))]
    in_specs += [pl.BlockSpec((None, k, tn), lambda j, i: (layer, 0, j)) for _ in ws]
    in_specs += [pl.BlockSpec((None, 1, tn), lambda j, i: (layer, 0, j)) for _ in bs]
    in_specs += [spec for _, spec in extras]
    args = [x] + list(ws) + [b.reshape(b.shape[0], 1, n) for b in bs] + [a for a, _ in extras]
    return pl.pallas_call(
        functools.partial(_dense_kernel, n_w=len(ws), n_extra=len(extras), epilogue=epilogue),
        grid=(n // tn, m // tm),
        in_specs=in_specs,
        out_specs=pl.BlockSpec((tm, tn), lambda j, i: (i, j)),
        out_shape=jax.ShapeDtypeStruct((m, n), out_dtype),
        scratch_shapes=[pltpu.VMEM((k, tn), bf16) for _ in ws],
        compiler_params=_cparams(2), name=name,
    )(*args)


def _gated_residual_extras(x2d, mod3, gate_chunk, d, tm, tn, seq):
    tiles_per_batch = seq // tm
    return [
        (x2d, pl.BlockSpec((tm, tn), lambda j, i: (i, j))),
        (mod3, pl.BlockSpec((None, 1, tn), lambda j, i: (i // tiles_per_batch, 0, gate_chunk * (d // tn) + j))),
    ]


def _attn_kernel(sinks_ref, q_ref, kp_ref, kc_ref, vp_ref, vc_ref, qg_ref, kg_ref, bias_ref, o_ref):
    n = pl.program_id(1)
    lane = lax.broadcasted_iota(i32, (1, LANES), 1)
    lo = lane < HEAD_DIM
    bq = q_ref.shape[1]

    def halfnorm(v, gain):
        sq = v * v
        s_lo = jnp.sum(jnp.where(lo, sq, 0.0), axis=-1, keepdims=True)
        s_hi = jnp.sum(jnp.where(lo, 0.0, sq), axis=-1, keepdims=True)
        ms = jnp.where(lo, s_lo, s_hi) * (1.0 / HEAD_DIM)
        return v * lax.rsqrt(ms + NORM_EPS) * gain

    k_all = jnp.concatenate([kp_ref[0], kc_ref[0]], axis=0)
    v_all = jnp.concatenate([vp_ref[0], vc_ref[0]], axis=0)
    kcol = lax.broadcasted_iota(i32, (1, 2 * bq), 1)
    key_ok = jnp.logical_or(kcol >= bq, n > 0)
    qgain = qg_ref[...]
    kgain = kg_ref[...]
    pairs = Q_PER_KV // 2
    for c in range(N_KV_HEADS // 2):
        kn = halfnorm(k_all[:, c * LANES:(c + 1) * LANES], kgain)
        kr = pltpu.roll(kn, HEAD_DIM, 1)
        vn = v_all[:, c * LANES:(c + 1) * LANES]
        vr = pltpu.roll(vn, HEAD_DIM, 1)
        for half in range(2):
            g = 2 * c + half
            k_src, k_rot = (kn, kr) if half == 0 else (kr, kn)
            v_src, v_rot = (vn, vr) if half == 0 else (vr, vn)
            k_par = [jnp.where(lo, k_src, 0.0).astype(bf16), jnp.where(lo, 0.0, k_rot).astype(bf16)]
            v_par = [jnp.where(lo, v_src, 0.0).astype(bf16), jnp.where(lo, 0.0, v_rot).astype(bf16)]
            qs = [halfnorm(q_ref[0, :, (g * pairs + p) * LANES:(g * pairs + p + 1) * LANES], qgain)
                  for p in range(pairs)]
            qg = (jnp.concatenate(qs, axis=0) * (1.0 / math.sqrt(HEAD_DIM))).astype(bf16)
            acc = None
            for par in range(2):
                s = lax.dot_general(qg, k_par[par], (((1,), (1,)), ((), ())), preferred_element_type=f32)
                s = s + bias_ref[g, par]
                s = jnp.where(key_ok, s, -jnp.inf)
                sink = jnp.concatenate(
                    [jnp.full((bq, 1), sinks_ref[g * Q_PER_KV + 2 * p + par], f32) for p in range(pairs)], axis=0)
                m = jnp.maximum(jnp.max(s, axis=-1, keepdims=True), sink)
                e = jnp.exp(s - m)
                den = jnp.sum(e, axis=-1, keepdims=True) + jnp.exp(sink - m)
                probs = (e * (1.0 / den)).astype(bf16)
                o = jnp.dot(probs, v_par[par], preferred_element_type=f32)
                acc = o if acc is None else acc + o
            for p in range(pairs):
                col = (g * pairs + p) * LANES
                o_ref[0, :, col:col + LANES] = acc[p * bq:(p + 1) * bq].astype(o_ref.dtype)


def _t5_bucket(dist):
    nn = np.maximum(dist, 0)
    max_exact = NUM_BUCKETS // 2
    large = max_exact + (np.log(np.maximum(nn, 1) / max_exact) / np.log(MAX_DISTANCE / max_exact)
                         * (NUM_BUCKETS - max_exact)).astype(np.int32)
    large = np.minimum(large, NUM_BUCKETS - 1)
    return np.where(nn < max_exact, nn, large).astype(np.int32)


def _attention(qkv, q_gain, k_gain, sinks, rel_bias):
    bsz, seq, _ = qkv.shape
    bq = WINDOW
    q_dim = N_Q_HEADS * HEAD_DIM
    kv_dim = N_KV_HEADS * HEAD_DIM
    pairs = Q_PER_KV // 2
    ql = np.arange(bq)[:, None]
    kl = np.arange(2 * bq)[None, :]
    dist = ql + bq - kl
    in_window = (dist >= 0) & (dist < WINDOW)
    bucket = jnp.asarray(_t5_bucket(dist).reshape(1, -1))
    onehot_t = (bucket == jnp.arange(NUM_BUCKETS, dtype=i32)[:, None]).astype(f32)
    bias = jnp.dot(rel_bias.astype(f32).T, onehot_t, precision=lax.Precision.HIGHEST).reshape(N_Q_HEADS, bq, 2 * bq)
    bias = jnp.where(jnp.asarray(in_window)[None], bias, -jnp.inf)
    bias = bias.reshape(N_KV_HEADS, pairs, 2, bq, 2 * bq)
    bias = jnp.transpose(bias, (0, 2, 1, 3, 4)).reshape(N_KV_HEADS, 2, pairs * bq, 2 * bq)
    gain2 = lambda gn: jnp.concatenate([gn, gn]).reshape(1, LANES).astype(f32)
    k_blk = q_dim // kv_dim
    grid_spec = pltpu.PrefetchScalarGridSpec(
        num_scalar_prefetch=1,
        grid=(bsz, seq // bq),
        in_specs=[
            pl.BlockSpec((1, bq, q_dim), lambda b, n, s: (b, n, 0)),
            pl.BlockSpec((1, bq, kv_dim), lambda b, n, s: (b, jnp.maximum(n - 1, 0), k_blk)),
            pl.BlockSpec((1, bq, kv_dim), lambda b, n, s: (b, n, k_blk)),
            pl.BlockSpec((1, bq, kv_dim), lambda b, n, s: (b, jnp.maximum(n - 1, 0), k_blk + 1)),
            pl.BlockSpec((1, bq, kv_dim), lambda b, n, s: (b, n, k_blk + 1)),
            pl.BlockSpec((1, LANES), lambda b, n, s: (0, 0)),
            pl.BlockSpec((1, LANES), lambda b, n, s: (0, 0)),
            pl.BlockSpec((N_KV_HEADS, 2, pairs * bq, 2 * bq), lambda b, n, s: (0, 0, 0, 0)),
        ],
        out_specs=pl.BlockSpec((1, bq, q_dim), lambda b, n, s: (b, n, 0)),
    )
    return pl.pallas_call(
        _attn_kernel, grid_spec=grid_spec,
        out_shape=jax.ShapeDtypeStruct((bsz, seq, q_dim), bf16),
        compiler_params=_cparams(2), name="swa_attention",
    )(sinks.astype(f32), qkv, qkv, qkv, qkv, qkv, gain2(q_gain), gain2(k_gain), bias)


def _gelu_tanh(y):
    return 0.5 * y * (1.0 + jnp.tanh(math.sqrt(2.0 / math.pi) * (y + 0.044715 * (y * y * y))))


def _ssm_kernel(u_ref, bm_ref, cm_ref, are_ref, aim_ref, d_ref, y_ref, us, xs, st, ys):
    bsz, ts, _ = u_ref.shape
    n_state = are_ref.shape[-1]

    @pl.when(pl.program_id(1) == 0)
    def _():
        st[...] = jnp.zeros_like(st)

    n_slab = us.shape[0]
    for h in range(n_slab):
        for b in range(bsz):
            us[h, pl.ds(b, ts, stride=bsz), :] = u_ref[b, :, h * LANES:(h + 1) * LANES]
    u = jnp.concatenate([us[h] for h in range(n_slab)], axis=1)
    xs[...] = jnp.dot(u.astype(bf16), bm_ref[0], preferred_element_type=f32)
    a_re = jnp.broadcast_to(are_ref[0], (bsz, n_state))
    a_im = jnp.broadcast_to(aim_ref[0], (bsz, n_state))

    def step(t, carry):
        s_re, s_im = carry
        r0 = pl.multiple_of(t * bsz, bsz)
        x_re = xs[pl.ds(r0, bsz), 0:n_state]
        x_im = xs[pl.ds(r0, bsz), n_state:2 * n_state]
        n_re = a_re * s_re - a_im * s_im + x_re
        n_im = a_re * s_im + a_im * s_re + x_im
        xs[pl.ds(r0, bsz), 0:n_state] = n_re
        xs[pl.ds(r0, bsz), n_state:2 * n_state] = n_im
        return n_re, n_im

    s_re, s_im = lax.fori_loop(0, ts, step, (st[0], st[1]), unroll=8)
    st[0] = s_re
    st[1] = s_im
    y = jnp.dot(xs[...].astype(bf16), cm_ref[0], preferred_element_type=f32) + d_ref[0] * u
    y = _gelu_tanh(y)
    for h in range(n_slab):
        ys[h] = y[:, h * LANES:(h + 1) * LANES]
        for b in range(bsz):
            y_ref[b, :, h * LANES:(h + 1) * LANES] = ys[h, pl.ds(b, ts, stride=bsz), :].astype(y_ref.dtype)


def _ssm(h, lam_re, lam_im, log_dt, b_re, b_im, c_re, c_im, d_skip):
    bsz, seq, width = h.shape
    assert bsz == SUBLANES
    n_grp, n_st = lam_re.shape
    tile_ch = SSM_TILE_CH
    gpt = tile_ch // SSM_GROUP_CH
    n_tiles = width // tile_ch
    ns = gpt * n_st
    dt = jnp.exp(log_dt.astype(f32))[:, None]
    lr, li = lam_re.astype(f32), lam_im.astype(f32)
    mag = jnp.exp(lr * dt)
    ab_re, ab_im = mag * jnp.cos(li * dt), mag * jnp.sin(li * dt)
    den = lr * lr + li * li
    nr, ni = ab_re - 1.0, ab_im
    f_re = (nr * lr + ni * li) / den
    f_im = (ni * lr - nr * li) / den
    br, bi = b_re.astype(f32), b_im.astype(f32)
    bb_re = f_re[..., None] * br - f_im[..., None] * bi
    bb_im = f_re[..., None] * bi + f_im[..., None] * br
    eye = jnp.eye(gpt, dtype=f32)

    def blockdiag_in(bb):
        t = bb.reshape(n_tiles, gpt, n_st, SSM_GROUP_CH)
        return jnp.einsum("tgpc,gh->tgchp", t, eye).reshape(n_tiles, tile_ch, ns)

    def blockdiag_out(cc):
        t = cc.reshape(n_tiles, gpt, SSM_GROUP_CH, n_st)
        return jnp.einsum("tgcp,gh->tgphc", t, eye).reshape(n_tiles, ns, tile_ch)

    bm = jnp.concatenate([blockdiag_in(bb_re), blockdiag_in(bb_im)], axis=-1).astype(bf16)
    cm = jnp.concatenate([blockdiag_out(c_re.astype(f32)), -blockdiag_out(c_im.astype(f32))], axis=1).astype(bf16)
    a_re = ab_re.reshape(n_tiles, 1, ns)
    a_im = ab_im.reshape(n_tiles, 1, ns)
    dsk = d_skip.astype(f32).reshape(n_tiles, 1, tile_ch)
    ts = SSM_TIME_CHUNK
    return pl.pallas_call(
        _ssm_kernel,
        grid=(n_tiles, seq // ts),
        in_specs=[
            pl.BlockSpec((bsz, ts, tile_ch), lambda g, t: (0, t, g)),
            pl.BlockSpec((1, tile_ch, 2 * ns), lambda g, t: (g, 0, 0)),
            pl.BlockSpec((1, 2 * ns, tile_ch), lambda g, t: (g, 0, 0)),
            pl.BlockSpec((1, 1, ns), lambda g, t: (g, 0, 0)),
            pl.BlockSpec((1, 1, ns), lambda g, t: (g, 0, 0)),
            pl.BlockSpec((1, 1, tile_ch), lambda g, t: (g, 0, 0)),
        ],
        out_specs=pl.BlockSpec((bsz, ts, tile_ch), lambda g, t: (0, t, g)),
        out_shape=jax.ShapeDtypeStruct((bsz, seq, width), bf16),
        scratch_shapes=[
            pltpu.VMEM((tile_ch // LANES, ts * bsz, LANES), f32),
            pltpu.VMEM((ts * bsz, 2 * ns), f32),
            pltpu.VMEM((2, bsz, ns), f32),
            pltpu.VMEM((tile_ch // LANES, ts * bsz, LANES), f32),
        ],
        compiler_params=_cparams(2), name="s5_ssm",
    )(h, bm, cm, a_re, a_im, dsk)


GATHER_PITCH = 24


def _swiglu(g, l):
    glu = jnp.minimum(g, SWIGLU_LIMIT)
    lin = jnp.clip(l, -SWIGLU_LIMIT, SWIGLU_LIMIT)
    return glu * jax.nn.sigmoid(SWIGLU_ALPHA * glu) * (lin + 1.0)


def _for_row_groups(nsub, cast, fn):
    n_big = nsub // MOE_GROUP
    big_rows = MOE_GROUP * MOE_SUB

    @pl.when(n_big > 0)
    def _():
        fn(0, big_rows, cast())

    @pl.when(n_big == 0)
    def _():
        fn(0, MOE_SUB, cast())

    def big(i, carry):
        fn(pl.multiple_of(i * big_rows, big_rows), big_rows, None)
        return carry

    def small(i, carry):
        fn(pl.multiple_of(i * MOE_SUB, MOE_SUB), MOE_SUB, None)
        return carry

    lax.fori_loop(1, n_big, big, 0)
    lax.fori_loop(jnp.maximum(n_big * MOE_GROUP, 1), nsub, small, 0)


def _moe_kernel(sbe_ref, sbn_ref, idx0_ref, idxn_ref, h_hbm, wg_ref, wl_ref, bg_ref, bl_ref, wd_ref, bd_ref, o_ref,
                xbf, stg, sem, a_scr, wg_bf, wl_bf, wd_bf, *, n_chunks):
    s = pl.program_id(0)
    c = pl.program_id(1)
    n_sb = pl.num_programs(0)
    nsub = sbn_ref[s]
    slot = lax.rem(s, 2)
    nsub_next = jnp.where(s + 1 < n_sb, sbn_ref[jnp.minimum(s + 1, n_sb - 1)], 0)
    n_col = xbf.shape[2] // LANES
    stg_rows = MOE_SUB * GATHER_PITCH

    def issue(idx_ref, off, st_slot):
        def body(r2, carry):
            for prio in range(N_DMA_PRIORITIES):
                r = r2 * N_DMA_PRIORITIES + prio
                src = pl.multiple_of(idx_ref[0, 0, off + r] * n_col, SUBLANES)
                dst = pl.multiple_of(st_slot * stg_rows + r * GATHER_PITCH, SUBLANES)
                pltpu.make_async_copy(h_hbm.at[pl.ds(src, n_col)], stg.at[pl.ds(dst, n_col)],
                                      sem.at[st_slot]).start(priority=prio)
            return carry

        lax.fori_loop(0, MOE_SUB // N_DMA_PRIORITIES, body, 0, unroll=4)

    def finish(st_slot, x_slot, sub):
        base = pl.multiple_of(st_slot * stg_rows, SUBLANES)
        pltpu.make_async_copy(h_hbm.at[pl.ds(0, MOE_SUB * n_col)], stg.at[pl.ds(base, MOE_SUB * n_col)],
                              sem.at[st_slot]).wait()
        r0 = pl.multiple_of(sub * MOE_SUB, MOE_SUB)
        for j in range(n_col):
            piece = stg[pl.ds(base + j, MOE_SUB, stride=GATHER_PITCH), :]
            xbf[x_slot, pl.ds(r0, MOE_SUB), j * LANES:(j + 1) * LANES] = piece.astype(bf16)

    @pl.when(jnp.logical_and(s == 0, c == 0))
    def _():
        def first(j, carry):
            issue(idx0_ref, j * MOE_SUB, 0)
            finish(0, 0, j)
            return carry

        lax.fori_loop(0, nsub, first, 0)

    @pl.when(jnp.logical_and(c >= 1, c - 1 < nsub_next))
    def _():
        finish(lax.rem(c - 1, 2), 1 - slot, c - 1)

    @pl.when(c < nsub_next)
    def _():
        issue(idxn_ref, 0, lax.rem(c, 2))

    @pl.when(jnp.logical_and(c < n_chunks, nsub > 0))
    def _():
        def cast():
            wg, wl = wg_ref[...].astype(bf16), wl_ref[...].astype(bf16)
            wg_bf[...] = wg
            wl_bf[...] = wl
            return wg, wl

        def up(r, rows, weights):
            wg, wl = weights if weights is not None else (wg_bf[...], wl_bf[...])
            xs = xbf[slot, pl.ds(r, rows), :]
            g = jnp.dot(xs, wg, preferred_element_type=f32) + bg_ref[...]
            l = jnp.dot(xs, wl, preferred_element_type=f32) + bl_ref[...]
            a_scr[c, pl.ds(r, rows), :] = _swiglu(g, l).astype(bf16)

        _for_row_groups(nsub, cast, up)

    @pl.when(jnp.logical_and(c >= n_chunks, nsub > 0))
    def _():
        def cast():
            wd = wd_ref[...].astype(bf16)
            wd_bf[...] = wd
            return wd

        def down(r, rows, weights):
            wd = weights if weights is not None else wd_bf[...]
            a = jnp.concatenate([a_scr[cc, pl.ds(r, rows), :] for cc in range(n_chunks)], axis=1)
            o_ref[pl.ds(r, rows), :] = jnp.dot(a, wd, preferred_element_type=f32) + bd_ref[...]

        _for_row_groups(nsub, cast, down)

    @pl.when(c >= n_chunks)
    def _():
        def zero(i, carry):
            r = pl.multiple_of(i * MOE_SUB, MOE_SUB)
            o_ref[pl.ds(r, MOE_SUB), :] = jnp.zeros((MOE_SUB, o_ref.shape[1]), f32)
            return carry

        lax.fori_loop(nsub, MOE_SB_SUBS, zero, 0)


def _moe_experts(h_rows, n_col, row_tok, sb_expert, sb_nsub, w_gate_up, b_gate_up, w_down, b_down, layer):
    n_rows = row_tok.shape[0]
    d = n_col * LANES
    n_sb = n_rows // MOE_SB_ROWS
    d_exp = w_down.shape[2]
    n_chunks = d_exp // MOE_CHUNK
    n_out_chunks = d // MOE_DOWN_CHUNK
    last = n_chunks - 1
    last_out = n_out_chunks - 1

    def valid(s, n):
        return n[s] > 0

    def wg_map(s, c, e, n):
        return (layer, e[s], 0, jnp.where(valid(s, n), jnp.minimum(c, last), last))

    def wl_map(s, c, e, n):
        return (layer, e[s], 0, n_chunks + jnp.where(valid(s, n), jnp.minimum(c, last), last))

    def wd_map(s, c, e, n):
        return (layer, e[s], 0, jnp.where(valid(s, n), jnp.maximum(c - n_chunks, 0), last_out))

    def out_map(s, c, e, n):
        return (s, jnp.maximum(c - n_chunks, 0))

    def next_idx_map(s, c, e, n):
        return (jnp.minimum(s + 1, n_sb - 1) * MOE_SB_SUBS + jnp.minimum(c, MOE_SB_SUBS - 1), 0, 0)

    grid_spec = pltpu.PrefetchScalarGridSpec(
        num_scalar_prefetch=2,
        grid=(n_sb, n_chunks + n_out_chunks),
        in_specs=[
            pl.BlockSpec((1, 1, MOE_SB_ROWS), lambda s, c, e, n: (0, 0, 0), memory_space=pltpu.SMEM),
            pl.BlockSpec((1, 1, MOE_SUB), next_idx_map, memory_space=pltpu.SMEM),
            pl.BlockSpec(memory_space=pl.ANY),
            pl.BlockSpec((None, None, d, MOE_CHUNK), wg_map),
            pl.BlockSpec((None, None, d, MOE_CHUNK), wl_map),
            pl.BlockSpec((None, None, 1, MOE_CHUNK), wg_map),
            pl.BlockSpec((None, None, 1, MOE_CHUNK), wl_map),
            pl.BlockSpec((None, None, d_exp, MOE_DOWN_CHUNK), wd_map),
            pl.BlockSpec((None, None, 1, MOE_DOWN_CHUNK), wd_map),
        ],
        out_specs=pl.BlockSpec((MOE_SB_ROWS, MOE_DOWN_CHUNK), out_map),
        scratch_shapes=[
            pltpu.VMEM((2, MOE_SB_ROWS, d), bf16),
            pltpu.VMEM((2 * MOE_SUB * GATHER_PITCH, LANES), f32),
            pltpu.SemaphoreType.DMA((2,)),
            pltpu.VMEM((n_chunks, MOE_SB_ROWS, MOE_CHUNK), bf16),
            pltpu.VMEM((d, MOE_CHUNK), bf16),
            pltpu.VMEM((d, MOE_CHUNK), bf16),
            pltpu.VMEM((d_exp, MOE_DOWN_CHUNK), bf16),
        ],
    )
    n_l, n_e = b_gate_up.shape[:2]
    return pl.pallas_call(
        functools.partial(_moe_kernel, n_chunks=n_chunks), grid_spec=grid_spec,
        out_shape=jax.ShapeDtypeStruct((n_rows, d), f32),
        compiler_params=_cparams(2, MOE_VMEM_LIMIT), name="moe_experts",
    )(sb_expert, sb_nsub, row_tok.reshape(n_sb, 1, MOE_SB_ROWS), row_tok.reshape(n_sb * MOE_SB_SUBS, 1, MOE_SUB),
      h_rows, w_gate_up, w_gate_up,
      b_gate_up.reshape(n_l, n_e, 1, -1), b_gate_up.reshape(n_l, n_e, 1, -1),
      w_down, b_down.reshape(n_l, n_e, 1, -1))


def _combine_kernel(dest_ref, dest_next_ref, y_hbm, gate_ref, x_ref, g2_ref, o_ref, ybuf, sem):
    i = pl.program_id(0)
    n = pl.num_programs(0)
    tm = x_ref.shape[0]
    slot = lax.rem(i, 2)

    def issue(dest_blk, sl):
        for k in range(TOP_K):
            def body(t2, carry, k=k):
                for prio in range(N_DMA_PRIORITIES):
                    t = t2 * N_DMA_PRIORITIES + prio
                    pltpu.make_async_copy(y_hbm.at[pl.ds(dest_blk[0, 0, k * tm + t], 1)],
                                          ybuf.at[sl * TOP_K + k, pl.ds(t, 1)], sem.at[sl]).start(priority=prio)
                return carry

            lax.fori_loop(0, tm // N_DMA_PRIORITIES, body, 0, unroll=4)

    @pl.when(i == 0)
    def _():
        issue(dest_ref, 0)

    @pl.when(i + 1 < n)
    def _():
        issue(dest_next_ref, 1 - slot)

    for k in range(TOP_K):
        pltpu.make_async_copy(y_hbm.at[pl.ds(0, tm)], ybuf.at[slot * TOP_K + k], sem.at[slot]).wait()
    gates = gate_ref[...]
    acc = gates[:, 0:1] * ybuf[slot * TOP_K]
    for k in range(1, TOP_K):
        acc = acc + gates[:, k:k + 1] * ybuf[slot * TOP_K + k]
    o_ref[...] = x_ref[...] + g2_ref[...] * acc


def _moe_combine(y_sorted, dest, gates, x2d, mod3, gate_chunk, seq):
    n_tok, d = x2d.shape
    tm = 128
    n_tiles = n_tok // tm
    dest_km = jnp.transpose(dest.reshape(TOP_K, n_tiles, tm), (1, 0, 2)).reshape(n_tiles, 1, TOP_K * tm)
    tiles_per_batch = seq // tm
    return pl.pallas_call(
        _combine_kernel,
        grid=(n_tiles,),
        in_specs=[
            pl.BlockSpec((1, 1, TOP_K * tm), lambda i: (i, 0, 0), memory_space=pltpu.SMEM),
            pl.BlockSpec((1, 1, TOP_K * tm), lambda i: (jnp.minimum(i + 1, n_tiles - 1), 0, 0),
                         memory_space=pltpu.SMEM),
            pl.BlockSpec(memory_space=pl.ANY),
            pl.BlockSpec((tm, LANES), lambda i: (i, 0)),
            pl.BlockSpec((tm, d), lambda i: (i, 0)),
            pl.BlockSpec((None, 1, d), lambda i: (i // tiles_per_batch, 0, gate_chunk)),
        ],
        out_specs=pl.BlockSpec((tm, d), lambda i: (i, 0)),
        out_shape=jax.ShapeDtypeStruct((n_tok, d), f32),
        scratch_shapes=[pltpu.VMEM((2 * TOP_K, tm, d), f32), pltpu.SemaphoreType.DMA((2,))],
        compiler_params=_cparams(1), name="moe_combine",
    )(dest_km, dest_km, y_sorted, gates, x2d, mod3)


def _routing_tables(idx_km):
    n_tok = idx_km.shape[1]
    n_assign = n_tok * TOP_K
    n_sb_max = (n_assign // MOE_SUB + N_EXPERTS + N_EXPERTS * (MOE_SB_SUBS - 1)) // MOE_SB_SUBS
    experts = jnp.arange(N_EXPERTS, dtype=i32)
    onehot = idx_km[:, :, None] == experts[None, None, :]
    per_tok = jnp.sum(onehot.astype(i32), axis=0)
    before = jnp.cumsum(per_tok, axis=0) - per_tok
    rank = jnp.sum(jnp.where(onehot, before[None], 0), axis=2)
    counts = before[-1] + per_tok[-1]
    n_sub_e = (counts + MOE_SUB - 1) // MOE_SUB
    n_sb_e = (n_sub_e + MOE_SB_SUBS - 1) // MOE_SB_SUBS
    sb_end = jnp.cumsum(n_sb_e)
    sb_start = sb_end - n_sb_e
    base = n_sub_e // jnp.maximum(n_sb_e, 1)
    rem = n_sub_e - base * n_sb_e
    tables = jnp.stack([sb_start, base, rem], axis=1).astype(bf16)
    looked = jnp.dot(onehot.astype(bf16).reshape(n_assign, N_EXPERTS), tables,
                     preferred_element_type=f32).astype(i32).reshape(TOP_K, n_tok, 3)
    a_start, a_base, a_rem = looked[..., 0], looked[..., 1], looked[..., 2]
    q = rank // MOE_SUB
    thr = a_rem * (a_base + 1)
    sb_local = jnp.where(q < thr, q // (a_base + 1), a_rem + (q - thr) // jnp.maximum(a_base, 1))
    sub_in = jnp.where(q < thr, q % (a_base + 1), (q - thr) % jnp.maximum(a_base, 1))
    dest = ((a_start + sb_local) * MOE_SB_ROWS + sub_in * MOE_SUB + rank % MOE_SUB).astype(i32)
    tok = jnp.broadcast_to(jnp.arange(n_tok, dtype=i32)[None, :], (TOP_K, n_tok))
    row_tok = jnp.zeros((n_sb_max * MOE_SB_ROWS,), i32).at[dest.reshape(-1)].set(
        tok.reshape(-1), unique_indices=True, mode="promise_in_bounds")
    s = jnp.arange(n_sb_max, dtype=i32)
    n_sb_total = sb_end[-1]
    sb_valid = s < n_sb_total
    sb_src = jnp.where(sb_valid, s, n_sb_total - 1).astype(i32)
    sb_e = jnp.minimum(jnp.searchsorted(sb_end, sb_src, side="right"), N_EXPERTS - 1).astype(i32)
    sb_local_s = sb_src - sb_start[sb_e]
    sb_nsub = jnp.where(sb_valid, base[sb_e] + (sb_local_s < rem[sb_e]).astype(i32), 0).astype(i32)
    return dest, row_tok, sb_e, sb_nsub


def kernel(x, c, rel_bias, norm_gain, ada_w, ada_b, attn_w_qkv, attn_b_qkv, attn_q_gain, attn_k_gain, attn_sinks, attn_w_o, attn_b_o, ssm_lam_re, ssm_lam_im, ssm_log_dt, ssm_b_re, ssm_b_im, ssm_c_re, ssm_c_im, ssm_d, ssm_w_glu_a, ssm_b_glu_a, ssm_w_glu_b, ssm_b_glu_b, moe_w_router, moe_b_router, moe_w_gate_up, moe_b_gate_up, moe_w_down, moe_b_down):
    bsz, seq, d = x.shape
    n_tok = bsz * seq
    depth = norm_gain.shape[0]
    tm, tn = 1024, 512
    SH1, SC1, G1, SH2, SC2, G2 = range(6)

    def gated(accs, extras):
        x_res, gate = extras
        return x_res + gate * accs[0]

    def gated_glu(accs, extras):
        x_res, gate = extras
        return x_res + gate * (accs[0] * jax.nn.sigmoid(accs[1]))

    for layer in range(depth):
        mod3 = _modulation(c, ada_w, ada_b, layer).reshape(bsz, 1, 6 * d)
        x2d = x.reshape(n_tok, d)
        res_extras = _gated_residual_extras(x2d, mod3, G1, d, tm, tn, seq)
        i = layer // 2
        if layer % 2 == 0:
            h = _norm(x, norm_gain[layer, 0], mod3, SC1, SH1, bf16)
            qkv = _dense(h.reshape(n_tok, d), [attn_w_qkv], [attn_b_qkv], i, lambda accs, extras: accs[0], [],
                         f32, tm, 512, "qkv_proj")
            o = _attention(qkv.reshape(bsz, seq, -1), attn_q_gain[i], attn_k_gain[i], attn_sinks[i], rel_bias)
            x2d = _dense(o.reshape(n_tok, -1), [attn_w_o], [attn_b_o], i, gated, res_extras, f32, tm, tn, "attn_out")
        else:
            h = _norm(x, norm_gain[layer, 0], mod3, SC1, SH1, f32)
            y = _ssm(h, ssm_lam_re[i], ssm_lam_im[i], ssm_log_dt[i], ssm_b_re[i], ssm_b_im[i],
                     ssm_c_re[i], ssm_c_im[i], ssm_d[i])
            x2d = _dense(y.reshape(n_tok, d), [ssm_w_glu_a, ssm_w_glu_b], [ssm_b_glu_a, ssm_b_glu_b], i,
                         gated_glu, res_extras, f32, tm, tn, "ssm_glu")
        x = x2d.reshape(bsz, seq, d)
        h, top_idx, gates = _norm(x, norm_gain[layer, 1], mod3, SC2, SH2, f32,
                                  router=(moe_w_router, moe_b_router, layer))
        idx_km = top_idx.reshape(n_tok, LANES)[:, :TOP_K].T
        dest, row_tok, sb_e, sb_nsub = _routing_tables(idx_km)
        y_sorted = _moe_experts(h, d // LANES, row_tok, sb_e, sb_nsub, moe_w_gate_up, moe_b_gate_up,
                                moe_w_down, moe_b_down, layer)
        x = _moe_combine(y_sorted, dest, gates.reshape(n_tok, LANES), x2d, mod3, G2, seq).reshape(bsz, seq, d)
    return x
```

```python
import functools
import math

import numpy as np
import jax
import jax.numpy as jnp
from jax import lax
from jax.experimental import pallas as pl
from jax.experimental.pallas import tpu as pltpu

f32 = jnp.float32
bf16 = jnp.bfloat16
i32 = jnp.int32

N_Q_HEADS = 32
N_KV_HEADS = 4
HEAD_DIM = 64
Q_PER_KV = N_Q_HEADS // N_KV_HEADS
WINDOW = 128
NUM_BUCKETS = 32
MAX_DISTANCE = 128
SSM_GROUP_CH = 16
SSM_STATE = 64
N_EXPERTS = 32
TOP_K = 4
SWIGLU_ALPHA = 1.702
SWIGLU_LIMIT = 7.0
NORM_EPS = 1e-5

LANES = 128
SUBLANES = 8
VMEM_LIMIT = 56 * 1024 * 1024
MOE_VMEM_LIMIT = 60 * 1024 * 1024
N_DMA_PRIORITIES = 2

SSM_TILE_CH = 128
SSM_TIME_CHUNK = 256
MOE_SUB = 256
MOE_SB_SUBS = 8
MOE_SB_ROWS = MOE_SUB * MOE_SB_SUBS
MOE_GROUP = 4
MOE_CHUNK = 256
MOE_DOWN_CHUNK = 512


def _cparams(n_axes, vmem_limit=VMEM_LIMIT):
    return pltpu.CompilerParams(dimension_semantics=("arbitrary",) * n_axes, vmem_limit_bytes=vmem_limit)


def _mod_kernel(c_ref, w_ref, b_ref, o_ref):
    c = c_ref[...]
    cond = c * jax.nn.sigmoid(c)
    o_ref[...] = jnp.dot(cond.astype(bf16), w_ref[...].astype(bf16), preferred_element_type=f32) + b_ref[...]


def _modulation(c, ada_w, ada_b, layer):
    bsz, d = c.shape
    n = ada_w.shape[-1]
    tn = 1024
    return pl.pallas_call(
        _mod_kernel,
        grid=(n // tn,),
        in_specs=[
            pl.BlockSpec((bsz, d), lambda j: (0, 0)),
            pl.BlockSpec((None, d, tn), lambda j: (layer, 0, j)),
            pl.BlockSpec((None, 1, tn), lambda j: (layer, 0, j)),
        ],
        out_specs=pl.BlockSpec((bsz, tn), lambda j: (0, j)),
        out_shape=jax.ShapeDtypeStruct((bsz, n), f32),
        compiler_params=_cparams(1),
        name="adaln_mod",
    )(c, ada_w, ada_b.reshape(ada_b.shape[0], 1, n))


def _norm_mod(x_ref, gain_ref, sc_ref, sh_ref):
    x = x_ref[0]
    ms = jnp.mean(x * x, axis=-1, keepdims=True)
    y = x * lax.rsqrt(ms + NORM_EPS) * gain_ref[...]
    return y * (1.0 + sc_ref[0]) + sh_ref[0]


def _norm_kernel(x_ref, gain_ref, sc_ref, sh_ref, h_ref):
    h_ref[0] = _norm_mod(x_ref, gain_ref, sc_ref, sh_ref).astype(h_ref.dtype)


def _split_bf16(v):
    hi = v.astype(bf16)
    lo = (v - hi.astype(f32)).astype(bf16)
    return hi, lo


def _norm_router_kernel(x_ref, gain_ref, sc_ref, sh_ref, wr_ref, br_ref, h_ref, idx_ref, gate_ref):
    h = _norm_mod(x_ref, gain_ref, sc_ref, sh_ref)
    ts, d = h.shape
    n_col = d // LANES
    for j in range(n_col):
        h_ref[pl.ds(j, ts, stride=n_col), :] = h[:, j * LANES:(j + 1) * LANES]
    h_hi, h_lo = _split_bf16(h)
    w_hi, w_lo = _split_bf16(wr_ref[...])
    dot = functools.partial(jnp.dot, preferred_element_type=f32)
    logits = dot(h_hi, w_hi) + (dot(h_hi, w_lo) + dot(h_lo, w_hi)) + br_ref[...]
    rows, n_exp = logits.shape
    col = lax.broadcasted_iota(i32, (rows, n_exp), 1)
    work = logits
    tops, idxs = [], []
    for _ in range(TOP_K):
        m = jnp.max(work, axis=-1, keepdims=True)
        idx = jnp.min(jnp.where(work == m, col, n_exp), axis=-1, keepdims=True)
        work = jnp.where(col == idx, -jnp.inf, work)
        tops.append(m)
        idxs.append(idx)
    es = [jnp.exp(t - tops[0]) for t in tops]
    denom = es[0] + es[1] + es[2] + es[3]
    lane = lax.broadcasted_iota(i32, (rows, LANES), 1)
    idx_out = jnp.zeros((rows, LANES), i32)
    gate_out = jnp.zeros((rows, LANES), f32)
    for k in range(TOP_K):
        idx_out = jnp.where(lane == k, idxs[k], idx_out)
        gate_out = jnp.where(lane == k, es[k] / denom, gate_out)
    idx_ref[0] = idx_out
    gate_ref[0] = gate_out


def _norm(x, gain, mod3, sc_chunk, sh_chunk, out_dtype, router=None):
    bsz, seq, d = x.shape
    ts = 512
    in_specs = [
        pl.BlockSpec((1, ts, d), lambda b, s: (b, s, 0)),
        pl.BlockSpec((1, d), lambda b, s: (0, 0)),
        pl.BlockSpec((1, 1, d), lambda b, s: (b, 0, sc_chunk)),
        pl.BlockSpec((1, 1, d), lambda b, s: (b, 0, sh_chunk)),
    ]
    h_spec = pl.BlockSpec((1, ts, d), lambda b, s: (b, s, 0))
    args = [x, gain.reshape(1, d), mod3, mod3]
    if router is None:
        return pl.pallas_call(
            _norm_kernel, grid=(bsz, seq // ts), in_specs=in_specs, out_specs=h_spec,
            out_shape=jax.ShapeDtypeStruct((bsz, seq, d), out_dtype),
            compiler_params=_cparams(2), name="norm_mod",
        )(*args)
    w_router, b_router, layer = router
    n_exp = w_router.shape[-1]
    in_specs += [
        pl.BlockSpec((None, d, n_exp), lambda b, s: (layer, 0, 0)),
        pl.BlockSpec((None, 1, n_exp), lambda b, s: (layer, 0, 0)),
    ]
    lane_spec = pl.BlockSpec((1, ts, LANES), lambda b, s: (b, s, 0))
    n_col = d // LANES
    h3_spec = pl.BlockSpec((ts * n_col, LANES), lambda b, s: (b * (seq // ts) + s, 0))
    return pl.pallas_call(
        _norm_router_kernel, grid=(bsz, seq // ts), in_specs=in_specs,
        out_specs=[h3_spec, lane_spec, lane_spec],
        out_shape=[jax.ShapeDtypeStruct((bsz * seq * n_col, LANES), f32),
                   jax.ShapeDtypeStruct((bsz, seq, LANES), i32),
                   jax.ShapeDtypeStruct((bsz, seq, LANES), f32)],
        compiler_params=_cparams(2), name="norm_router",
    )(*args, w_router, b_router.reshape(b_router.shape[0], 1, n_exp))


def _dense_kernel(*refs, n_w, n_extra, epilogue):
    x_ref = refs[0]
    w_refs = refs[1:1 + n_w]
    b_refs = refs[1 + n_w:1 + 2 * n_w]
    e_refs = refs[1 + 2 * n_w:1 + 2 * n_w + n_extra]
    o_ref = refs[1 + 2 * n_w + n_extra]
    wbf_refs = refs[2 + 2 * n_w + n_extra:]

    @pl.when(pl.program_id(1) == 0)
    def _():
        for w_ref, wbf in zip(w_refs, wbf_refs):
            wbf[...] = w_ref[...].astype(bf16)

    x = x_ref[...]
    accs = [jnp.dot(x, wbf[...], preferred_element_type=f32) + b_ref[...] for wbf, b_ref in zip(wbf_refs, b_refs)]
    o_ref[...] = epilogue(accs, [e[...] for e in e_refs]).astype(o_ref.dtype)


def _dense(x, ws, bs, layer, epilogue, extras, out_dtype, tm, tn, name):
    m, k = x.shape
    n = ws[0].shape[-1]
    in_specs = [pl.BlockSpec((tm, k), lambda j, i: (i, 0))]
    in_specs += [pl.BlockSpec((None, k, tn), lambda j, i: (layer, 0, j)) for _ in ws]
    in_specs += [pl.BlockSpec((None, 1, tn), lambda j, i: (layer, 0, j)) for _ in bs]
    in_specs += [spec for _, spec in extras]
    args = [x] + list(ws) + [b.reshape(b.shape[0], 1, n) for b in bs] + [a for a, _ in extras]
    return pl.pallas_call(
        functools.partial(_dense_kernel, n_w=len(ws), n_extra=len(extras), epilogue=epilogue),
        grid=(n // tn, m // tm),
        in_specs=in_specs,
        out_specs=pl.BlockSpec((tm, tn), lambda j, i: (i, j)),
        out_shape=jax.ShapeDtypeStruct((m, n), out_dtype),
        scratch_shapes=[pltpu.VMEM((k, tn), bf16) for _ in ws],
        compiler_params=_cparams(2), name=name,
    )(*args)


def _gated_residual_extras(x2d, mod3, gate_chunk, d, tm, tn, seq):
    tiles_per_batch = seq // tm
    return [
        (x2d, pl.BlockSpec((tm, tn), lambda j, i: (i, j))),
        (mod3, pl.BlockSpec((None, 1, tn), lambda j, i: (i // tiles_per_batch, 0, gate_chunk * (d // tn) + j))),
    ]


def _attn_kernel(sinks_ref, q_ref, kp_ref, kc_ref, vp_ref, vc_ref, qg_ref, kg_ref, bias_ref, o_ref):
    n = pl.program_id(1)
    lane = lax.broadcasted_iota(i32, (1, LANES), 1)
    lo = lane < HEAD_DIM
    bq = q_ref.shape[1]

    def halfnorm(v, gain):
        sq = v * v
        s_lo = jnp.sum(jnp.where(lo, sq, 0.0), axis=-1, keepdims=True)
        s_hi = jnp.sum(jnp.where(lo, 0.0, sq), axis=-1, keepdims=True)
        ms = jnp.where(lo, s_lo, s_hi) * (1.0 / HEAD_DIM)
        return v * lax.rsqrt(ms + NORM_EPS) * gain

    k_all = jnp.concatenate([kp_ref[0], kc_ref[0]], axis=0)
    v_all = jnp.concatenate([vp_ref[0], vc_ref[0]], axis=0)
    kcol = lax.broadcasted_iota(i32, (1, 2 * bq), 1)
    key_ok = jnp.logical_or(kcol >= bq, n > 0)
    qgain = qg_ref[...]
    kgain = kg_ref[...]
    pairs = Q_PER_KV // 2
    for c in range(N_KV_HEADS // 2):
        kn = halfnorm(k_all[:, c * LANES:(c + 1) * LANES], kgain)
        kr = pltpu.roll(kn, HEAD_DIM, 1)
        vn = v_all[:, c * LANES:(c + 1) * LANES]
        vr = pltpu.roll(vn, HEAD_DIM, 1)
        for half in range(2):
            g = 2 * c + half
            k_src, k_rot = (kn, kr) if half == 0 else (kr, kn)
            v_src, v_rot = (vn, vr) if half == 0 else (vr, vn)
            k_par = [jnp.where(lo, k_src, 0.0).astype(bf16), jnp.where(lo, 0.0, k_rot).astype(bf16)]
            v_par = [jnp.where(lo, v_src, 0.0).astype(bf16), jnp.where(lo, 0.0, v_rot).astype(bf16)]
            qs = [halfnorm(q_ref[0, :, (g * pairs + p) * LANES:(g * pairs + p + 1) * LANES], qgain)
                  for p in range(pairs)]
            qg = (jnp.concatenate(qs, axis=0) * (1.0 / math.sqrt(HEAD_DIM))).astype(bf16)
            acc = None
            for par in range(2):
                s = lax.dot_general(qg, k_par[par], (((1,), (1,)), ((), ())), preferred_element_type=f32)
                s = s + bias_ref[g, par]
                s = jnp.where(key_ok, s, -jnp.inf)
                sink = jnp.concatenate(
                    [jnp.full((bq, 1), sinks_ref[g * Q_PER_KV + 2 * p + par], f32) for p in range(pairs)], axis=0)
                m = jnp.maximum(jnp.max(s, axis=-1, keepdims=True), sink)
                e = jnp.exp(s - m)
                den = jnp.sum(e, axis=-1, keepdims=True) + jnp.exp(sink - m)
                probs = (e * (1.0 / den)).astype(bf16)
                o = jnp.dot(probs, v_par[par], preferred_element_type=f32)
                acc = o if acc is None else acc + o
            for p in range(pairs):
                col = (g * pairs + p) * LANES
                o_ref[0, :, col:col + LANES] = acc[p * bq:(p + 1) * bq].astype(o_ref.dtype)


def _t5_bucket(dist):
    nn = np.maximum(dist, 0)
    max_exact = NUM_BUCKETS // 2
    large = max_exact + (np.log(np.maximum(nn, 1) / max_exact) / np.log(MAX_DISTANCE / max_exact)
                         * (NUM_BUCKETS - max_exact)).astype(np.int32)
    large = np.minimum(large, NUM_BUCKETS - 1)
    return np.where(nn < max_exact, nn, large).astype(np.int32)


def _attention(qkv, q_gain, k_gain, sinks, rel_bias):
    bsz, seq, _ = qkv.shape
    bq = WINDOW
    q_dim = N_Q_HEADS * HEAD_DIM
    kv_dim = N_KV_HEADS * HEAD_DIM
    pairs = Q_PER_KV // 2
    ql = np.arange(bq)[:, None]
    kl = np.arange(2 * bq)[None, :]
    dist = ql + bq - kl
    in_window = (dist >= 0) & (dist < WINDOW)
    bucket = jnp.asarray(_t5_bucket(dist).reshape(1, -1))
    onehot_t = (bucket == jnp.arange(NUM_BUCKETS, dtype=i32)[:, None]).astype(f32)
    bias = jnp.dot(rel_bias.astype(f32).T, onehot_t, precision=lax.Precision.HIGHEST).reshape(N_Q_HEADS, bq, 2 * bq)
    bias = jnp.where(jnp.asarray(in_window)[None], bias, -jnp.inf)
    bias = bias.reshape(N_KV_HEADS, pairs, 2, bq, 2 * bq)
    bias = jnp.transpose(bias, (0, 2, 1, 3, 4)).reshape(N_KV_HEADS, 2, pairs * bq, 2 * bq)
    gain2 = lambda gn: jnp.concatenate([gn, gn]).reshape(1, LANES).astype(f32)
    k_blk = q_dim // kv_dim
    grid_spec = pltpu.PrefetchScalarGridSpec(
        num_scalar_prefetch=1,
        grid=(bsz, seq // bq),
        in_specs=[
            pl.BlockSpec((1, bq, q_dim), lambda b, n, s: (b, n, 0)),
            pl.BlockSpec((1, bq, kv_dim), lambda b, n, s: (b, jnp.maximum(n - 1, 0), k_blk)),
            pl.BlockSpec((1, bq, kv_dim), lambda b, n, s: (b, n, k_blk)),
            pl.BlockSpec((1, bq, kv_dim), lambda b, n, s: (b, jnp.maximum(n - 1, 0), k_blk + 1)),
            pl.BlockSpec((1, bq, kv_dim), lambda b, n, s: (b, n, k_blk + 1)),
            pl.BlockSpec((1, LANES), lambda b, n, s: (0, 0)),
            pl.BlockSpec((1, LANES), lambda b, n, s: (0, 0)),
            pl.BlockSpec((N_KV_HEADS, 2, pairs * bq, 2 * bq), lambda b, n, s: (0, 0, 0, 0)),
        ],
        out_specs=pl.BlockSpec((1, bq, q_dim), lambda b, n, s: (b, n, 0)),
    )
    return pl.pallas_call(
        _attn_kernel, grid_spec=grid_spec,
        out_shape=jax.ShapeDtypeStruct((bsz, seq, q_dim), bf16),
        compiler_params=_cparams(2), name="swa_attention",
    )(sinks.astype(f32), qkv, qkv, qkv, qkv, qkv, gain2(q_gain), gain2(k_gain), bias)


def _gelu_tanh(y):
    return 0.5 * y * (1.0 + jnp.tanh(math.sqrt(2.0 / math.pi) * (y + 0.044715 * (y * y * y))))


def _ssm_kernel(u_ref, bm_ref, cm_ref, are_ref, aim_ref, d_ref, y_ref, us, xs, st, ys):
    bsz, ts, _ = u_ref.shape
    n_state = are_ref.shape[-1]

    @pl.when(pl.program_id(1) == 0)
    def _():
        st[...] = jnp.zeros_like(st)

    n_slab = us.shape[0]
    for h in range(n_slab):
        for b in range(bsz):
            us[h, pl.ds(b, ts, stride=bsz), :] = u_ref[b, :, h * LANES:(h + 1) * LANES]
    u = jnp.concatenate([us[h] for h in range(n_slab)], axis=1)
    xs[...] = jnp.dot(u.astype(bf16), bm_ref[0], preferred_element_type=f32)
    a_re = jnp.broadcast_to(are_ref[0], (bsz, n_state))
    a_im = jnp.broadcast_to(aim_ref[0], (bsz, n_state))

    def step(t, carry):
        s_re, s_im = carry
        r0 = pl.multiple_of(t * bsz, bsz)
        x_re = xs[pl.ds(r0, bsz), 0:n_state]
        x_im = xs[pl.ds(r0, bsz), n_state:2 * n_state]
        n_re = a_re * s_re - a_im * s_im + x_re
        n_im = a_re * s_im + a_im * s_re + x_im
        xs[pl.ds(r0, bsz), 0:n_state] = n_re
        xs[pl.ds(r0, bsz), n_state:2 * n_state] = n_im
        return n_re, n_im

    s_re, s_im = lax.fori_loop(0, ts, step, (st[0], st[1]), unroll=8)
    st[0] = s_re
    st[1] = s_im
    y = jnp.dot(xs[...].astype(bf16), cm_ref[0], preferred_element_type=f32) + d_ref[0] * u
    y = _gelu_tanh(y)
    for h in range(n_slab):
        ys[h] = y[:, h * LANES:(h + 1) * LANES]
        for b in range(bsz):
            y_ref[b, :, h * LANES:(h + 1) * LANES] = ys[h, pl.ds(b, ts, stride=bsz), :].astype(y_ref.dtype)


def _ssm(h, lam_re, lam_im, log_dt, b_re, b_im, c_re, c_im, d_skip):
    bsz, seq, width = h.shape
    assert bsz == SUBLANES
    n_grp, n_st = lam_re.shape
    tile_ch = SSM_TILE_CH
    gpt = tile_ch // SSM_GROUP_CH
    n_tiles = width // tile_ch
    ns = gpt * n_st
    dt = jnp.exp(log_dt.astype(f32))[:, None]
    lr, li = lam_re.astype(f32), lam_im.astype(f32)
    mag = jnp.exp(lr * dt)
    ab_re, ab_im = mag * jnp.cos(li * dt), mag * jnp.sin(li * dt)
    den = lr * lr + li * li
    nr, ni = ab_re - 1.0, ab_im
    f_re = (nr * lr + ni * li) / den
    f_im = (ni * lr - nr * li) / den
    br, bi = b_re.astype(f32), b_im.astype(f32)
    bb_re = f_re[..., None] * br - f_im[..., None] * bi
    bb_im = f_re[..., None] * bi + f_im[..., None] * br
    eye = jnp.eye(gpt, dtype=f32)

    def blockdiag_in(bb):
        t = bb.reshape(n_tiles, gpt, n_st, SSM_GROUP_CH)
        return jnp.einsum("tgpc,gh->tgchp", t, eye).reshape(n_tiles, tile_ch, ns)

    def blockdiag_out(cc):
        t = cc.reshape(n_tiles, gpt, SSM_GROUP_CH, n_st)
        return jnp.einsum("tgcp,gh->tgphc", t, eye).reshape(n_tiles, ns, tile_ch)

    bm = jnp.concatenate([blockdiag_in(bb_re), blockdiag_in(bb_im)], axis=-1).astype(bf16)
    cm = jnp.concatenate([blockdiag_out(c_re.astype(f32)), -blockdiag_out(c_im.astype(f32))], axis=1).astype(bf16)
    a_re = ab_re.reshape(n_tiles, 1, ns)
    a_im = ab_im.reshape(n_tiles, 1, ns)
    dsk = d_skip.astype(f32).reshape(n_tiles, 1, tile_ch)
    ts = SSM_TIME_CHUNK
    return pl.pallas_call(
        _ssm_kernel,
        grid=(n_tiles, seq // ts),
        in_specs=[
            pl.BlockSpec((bsz, ts, tile_ch), lambda g, t: (0, t, g)),
            pl.BlockSpec((1, tile_ch, 2 * ns), lambda g, t: (g, 0, 0)),
            pl.BlockSpec((1, 2 * ns, tile_ch), lambda g, t: (g, 0, 0)),
            pl.BlockSpec((1, 1, ns), lambda g, t: (g, 0, 0)),
            pl.BlockSpec((1, 1, ns), lambda g, t: (g, 0, 0)),
            pl.BlockSpec((1, 1, tile_ch), lambda g, t: (g, 0, 0)),
        ],
        out_specs=pl.BlockSpec((bsz, ts, tile_ch), lambda g, t: (0, t, g)),
        out_shape=jax.ShapeDtypeStruct((bsz, seq, width), bf16),
        scratch_shapes=[
            pltpu.VMEM((tile_ch // LANES, ts * bsz, LANES), f32),
            pltpu.VMEM((ts * bsz, 2 * ns), f32),
            pltpu.VMEM((2, bsz, ns), f32),
            pltpu.VMEM((tile_ch // LANES, ts * bsz, LANES), f32),
        ],
        compiler_params=_cparams(2), name="s5_ssm",
    )(h, bm, cm, a_re, a_im, dsk)


GATHER_PITCH = 24


def _swiglu(g, l):
    glu = jnp.minimum(g, SWIGLU_LIMIT)
    lin = jnp.clip(l, -SWIGLU_LIMIT, SWIGLU_LIMIT)
    return glu * jax.nn.sigmoid(SWIGLU_ALPHA * glu) * (lin + 1.0)


def _for_row_groups(nsub, cast, fn):
    n_big = nsub // MOE_GROUP
    big_rows = MOE_GROUP * MOE_SUB

    @pl.when(n_big > 0)
    def _():
        fn(0, big_rows, cast())

    @pl.when(n_big == 0)
    def _():
        fn(0, MOE_SUB, cast())

    def big(i, carry):
        fn(pl.multiple_of(i * big_rows, big_rows), big_rows, None)
        return carry

    def small(i, carry):
        fn(pl.multiple_of(i * MOE_SUB, MOE_SUB), MOE_SUB, None)
        return carry

    lax.fori_loop(1, n_big, big, 0)
    lax.fori_loop(jnp.maximum(n_big * MOE_GROUP, 1), nsub, small, 0)


def _moe_kernel(sbe_ref, sbn_ref, idx0_ref, idxn_ref, h_hbm, wg_ref, wl_ref, bgu_ref, wd_ref, bd_ref, o_ref,
                xbf, stg, sem, a_scr, wg_bf, wl_bf, wd_bf, *, n_chunks):
    s = pl.program_id(0)
    c = pl.program_id(1)
    n_sb = pl.num_programs(0)
    nsub = sbn_ref[s]
    slot = lax.rem(s, 2)
    nsub_next = jnp.where(s + 1 < n_sb, sbn_ref[jnp.minimum(s + 1, n_sb - 1)], 0)
    n_col = xbf.shape[2] // LANES
    stg_rows = MOE_SUB * GATHER_PITCH

    def issue(idx_ref, off, st_slot):
        def body(r2, carry):
            for prio in range(N_DMA_PRIORITIES):
                r = r2 * N_DMA_PRIORITIES + prio
                src = pl.multiple_of(idx_ref[0, 0, off + r] * n_col, SUBLANES)
                dst = pl.multiple_of(st_slot * stg_rows + r * GATHER_PITCH, SUBLANES)
                pltpu.make_async_copy(h_hbm.at[pl.ds(src, n_col)], stg.at[pl.ds(dst, n_col)],
                                      sem.at[st_slot]).start(priority=prio)
            return carry

        lax.fori_loop(0, MOE_SUB // N_DMA_PRIORITIES, body, 0, unroll=4)

    def finish(st_slot, x_slot, sub):
        base = pl.multiple_of(st_slot * stg_rows, SUBLANES)
        pltpu.make_async_copy(h_hbm.at[pl.ds(0, MOE_SUB * n_col)], stg.at[pl.ds(base, MOE_SUB * n_col)],
                              sem.at[st_slot]).wait()
        r0 = pl.multiple_of(sub * MOE_SUB, MOE_SUB)
        for j in range(n_col):
            piece = stg[pl.ds(base + j, MOE_SUB, stride=GATHER_PITCH), :]
            xbf[x_slot, pl.ds(r0, MOE_SUB), j * LANES:(j + 1) * LANES] = piece.astype(bf16)

    @pl.when(jnp.logical_and(s == 0, c == 0))
    def _():
        def first(j, carry):
            issue(idx0_ref, j * MOE_SUB, 0)
            finish(0, 0, j)
            return carry

        lax.fori_loop(0, nsub, first, 0)

    @pl.when(jnp.logical_and(c >= 1, c - 1 < nsub_next))
    def _():
        finish(lax.rem(c - 1, 2), 1 - slot, c - 1)

    @pl.when(c < nsub_next)
    def _():
        issue(idxn_ref, c * MOE_SUB, lax.rem(c, 2))

    @pl.when(jnp.logical_and(c < n_chunks, nsub > 0))
    def _():
        def cast():
            wg, wl = wg_ref[...].astype(bf16), wl_ref[...].astype(bf16)
            wg_bf[...] = wg
            wl_bf[...] = wl
            return wg, wl

        def up(r, rows, weights):
            wg, wl = weights if weights is not None else (wg_bf[...], wl_bf[...])
            xs = xbf[slot, pl.ds(r, rows), :]
            g = jnp.dot(xs, wg, preferred_element_type=f32) + bgu_ref[pl.ds(c, 1), :]
            l = jnp.dot(xs, wl, preferred_element_type=f32) + bgu_ref[pl.ds(n_chunks + c, 1), :]
            a_scr[c, pl.ds(r, rows), :] = _swiglu(g, l).astype(bf16)

        _for_row_groups(nsub, cast, up)

    @pl.when(jnp.logical_and(c >= n_chunks, nsub > 0))
    def _():
        def cast():
            wd = wd_ref[...].astype(bf16)
            wd_bf[...] = wd
            return wd

        def down(r, rows, weights):
            wd = weights if weights is not None else wd_bf[...]
            a = jnp.concatenate([a_scr[cc, pl.ds(r, rows), :] for cc in range(n_chunks)], axis=1)
            o_ref[pl.ds(r, rows), :] = jnp.dot(a, wd, preferred_element_type=f32) + bd_ref[pl.ds(c - n_chunks, 1), :]

        _for_row_groups(nsub, cast, down)

    @pl.when(c >= n_chunks)
    def _():
        def zero(i, carry):
            r = pl.multiple_of(i * MOE_SUB, MOE_SUB)
            o_ref[pl.ds(r, MOE_SUB), :] = jnp.zeros((MOE_SUB, o_ref.shape[1]), f32)
            return carry

        lax.fori_loop(nsub, MOE_SB_SUBS, zero, 0)


def _moe_experts(h_rows, n_col, row_tok, sb_expert, sb_nsub, w_gate_up, b_gate_up, w_down, b_down, layer):
    n_rows = row_tok.shape[0]
    d = n_col * LANES
    n_sb = n_rows // MOE_SB_ROWS
    d_exp = w_down.shape[2]
    n_chunks = d_exp // MOE_CHUNK
    n_out_chunks = d // MOE_DOWN_CHUNK
    last = n_chunks - 1
    last_out = n_out_chunks - 1

    def valid(s, n):
        return n[s] > 0

    def wg_map(s, c, e, n):
        return (layer, e[s], 0, jnp.where(valid(s, n), jnp.minimum(c, last), last))

    def wl_map(s, c, e, n):
        return (layer, e[s], 0, n_chunks + jnp.where(valid(s, n), jnp.minimum(c, last), last))

    def wd_map(s, c, e, n):
        return (layer, e[s], 0, jnp.where(valid(s, n), jnp.maximum(c - n_chunks, 0), last_out))

    def out_map(s, c, e, n):
        return (s, jnp.maximum(c - n_chunks, 0))

    def next_idx_map(s, c, e, n):
        return (jnp.minimum(s + 1, n_sb - 1), 0, 0)

    def bias_map(s, c, e, n):
        return (layer, e[s], 0, 0)

    grid_spec = pltpu.PrefetchScalarGridSpec(
        num_scalar_prefetch=2,
        grid=(n_sb, n_chunks + n_out_chunks),
        in_specs=[
            pl.BlockSpec((1, 1, MOE_SB_ROWS), lambda s, c, e, n: (0, 0, 0), memory_space=pltpu.SMEM),
            pl.BlockSpec((1, 1, MOE_SB_ROWS), next_idx_map, memory_space=pltpu.SMEM),
            pl.BlockSpec(memory_space=pl.ANY),
            pl.BlockSpec((None, None, d, MOE_CHUNK), wg_map),
            pl.BlockSpec((None, None, d, MOE_CHUNK), wl_map),
            pl.BlockSpec((None, None, 2 * n_chunks, MOE_CHUNK), bias_map),
            pl.BlockSpec((None, None, d_exp, MOE_DOWN_CHUNK), wd_map),
            pl.BlockSpec((None, None, n_out_chunks, MOE_DOWN_CHUNK), bias_map),
        ],
        out_specs=pl.BlockSpec((MOE_SB_ROWS, MOE_DOWN_CHUNK), out_map),
        scratch_shapes=[
            pltpu.VMEM((2, MOE_SB_ROWS, d), bf16),
            pltpu.VMEM((2 * MOE_SUB * GATHER_PITCH, LANES), f32),
            pltpu.SemaphoreType.DMA((2,)),
            pltpu.VMEM((n_chunks, MOE_SB_ROWS, MOE_CHUNK), bf16),
            pltpu.VMEM((d, MOE_CHUNK), bf16),
            pltpu.VMEM((d, MOE_CHUNK), bf16),
            pltpu.VMEM((d_exp, MOE_DOWN_CHUNK), bf16),
        ],
    )
    n_l, n_e = b_gate_up.shape[:2]
    return pl.pallas_call(
        functools.partial(_moe_kernel, n_chunks=n_chunks), grid_spec=grid_spec,
        out_shape=jax.ShapeDtypeStruct((n_rows, d), f32),
        compiler_params=_cparams(2, MOE_VMEM_LIMIT), name="moe_experts",
    )(sb_expert, sb_nsub, row_tok.reshape(n_sb, 1, MOE_SB_ROWS), row_tok.reshape(n_sb, 1, MOE_SB_ROWS),
      h_rows, w_gate_up, w_gate_up,
      b_gate_up.reshape(n_l, n_e, 2 * n_chunks, MOE_CHUNK),
      w_down, b_down.reshape(n_l, n_e, n_out_chunks, MOE_DOWN_CHUNK))


def _combine_kernel(dest_ref, dest_next_ref, y_hbm, gate_ref, x_ref, g2_ref, o_ref, ybuf, sem):
    i = pl.program_id(0)
    n = pl.num_programs(0)
    tm = x_ref.shape[0]
    slot = lax.rem(i, 2)

    def issue(dest_blk, sl):
        for k in range(TOP_K):
            def body(t2, carry, k=k):
                for prio in range(N_DMA_PRIORITIES):
                    t = t2 * N_DMA_PRIORITIES + prio
                    pltpu.make_async_copy(y_hbm.at[pl.ds(dest_blk[0, 0, k * tm + t], 1)],
                                          ybuf.at[sl * TOP_K + k, pl.ds(t, 1)], sem.at[sl]).start(priority=prio)
                return carry

            lax.fori_loop(0, tm // N_DMA_PRIORITIES, body, 0, unroll=4)

    @pl.when(i == 0)
    def _():
        issue(dest_ref, 0)

    @pl.when(i + 1 < n)
    def _():
        issue(dest_next_ref, 1 - slot)

    for k in range(TOP_K):
        pltpu.make_async_copy(y_hbm.at[pl.ds(0, tm)], ybuf.at[slot * TOP_K + k], sem.at[slot]).wait()
    gates = gate_ref[...]
    acc = gates[:, 0:1] * ybuf[slot * TOP_K]
    for k in range(1, TOP_K):
        acc = acc + gates[:, k:k + 1] * ybuf[slot * TOP_K + k]
    o_ref[...] = x_ref[...] + g2_ref[...] * acc


def _moe_combine(y_sorted, dest, gates, x2d, mod3, gate_chunk, seq):
    n_tok, d = x2d.shape
    tm = 128
    n_tiles = n_tok // tm
    dest_km = jnp.transpose(dest.reshape(TOP_K, n_tiles, tm), (1, 0, 2)).reshape(n_tiles, 1, TOP_K * tm)
    tiles_per_batch = seq // tm
    return pl.pallas_call(
        _combine_kernel,
        grid=(n_tiles,),
        in_specs=[
            pl.BlockSpec((1, 1, TOP_K * tm), lambda i: (i, 0, 0), memory_space=pltpu.SMEM),
            pl.BlockSpec((1, 1, TOP_K * tm), lambda i: (jnp.minimum(i + 1, n_tiles - 1), 0, 0),
                         memory_space=pltpu.SMEM),
            pl.BlockSpec(memory_space=pl.ANY),
            pl.BlockSpec((tm, LANES), lambda i: (i, 0)),
            pl.BlockSpec((tm, d), lambda i: (i, 0)),
            pl.BlockSpec((None, 1, d), lambda i: (i // tiles_per_batch, 0, gate_chunk)),
        ],
        out_specs=pl.BlockSpec((tm, d), lambda i: (i, 0)),
        out_shape=jax.ShapeDtypeStruct((n_tok, d), f32),
        scratch_shapes=[pltpu.VMEM((2 * TOP_K, tm, d), f32), pltpu.SemaphoreType.DMA((2,))],
        compiler_params=_cparams(1), name="moe_combine",
    )(dest_km, dest_km, y_sorted, gates, x2d, mod3)


def _routing_tables(idx_km):
    n_tok = idx_km.shape[1]
    n_assign = n_tok * TOP_K
    n_sb_max = (n_assign // MOE_SUB + N_EXPERTS + N_EXPERTS * (MOE_SB_SUBS - 1)) // MOE_SB_SUBS
    experts = jnp.arange(N_EXPERTS, dtype=i32)
    onehot = idx_km[:, :, None] == experts[None, None, :]
    per_tok = jnp.sum(onehot.astype(i32), axis=0)
    before = jnp.cumsum(per_tok, axis=0) - per_tok
    rank = jnp.sum(jnp.where(onehot, before[None], 0), axis=2)
    counts = before[-1] + per_tok[-1]
    n_sub_e = (counts + MOE_SUB - 1) // MOE_SUB
    n_sb_e = (n_sub_e + MOE_SB_SUBS - 1) // MOE_SB_SUBS
    sb_end = jnp.cumsum(n_sb_e)
    sb_start = sb_end - n_sb_e
    base = n_sub_e // jnp.maximum(n_sb_e, 1)
    rem = n_sub_e - base * n_sb_e
    tables = jnp.stack([sb_start, base, rem], axis=1).astype(bf16)
    looked = jnp.dot(onehot.astype(bf16).reshape(n_assign, N_EXPERTS), tables,
                     preferred_element_type=f32).astype(i32).reshape(TOP_K, n_tok, 3)
    a_start, a_base, a_rem = looked[..., 0], looked[..., 1], looked[..., 2]
    q = rank // MOE_SUB
    thr = a_rem * (a_base + 1)
    sb_local = jnp.where(q < thr, q // (a_base + 1), a_rem + (q - thr) // jnp.maximum(a_base, 1))
    sub_in = jnp.where(q < thr, q % (a_base + 1), (q - thr) % jnp.maximum(a_base, 1))
    dest = ((a_start + sb_local) * MOE_SB_ROWS + sub_in * MOE_SUB + rank % MOE_SUB).astype(i32)
    tok = jnp.broadcast_to(jnp.arange(n_tok, dtype=i32)[None, :], (TOP_K, n_tok))
    row_tok = jnp.zeros((n_sb_max * MOE_SB_ROWS,), i32).at[dest.reshape(-1)].set(
        tok.reshape(-1), unique_indices=True, mode="promise_in_bounds")
    s = jnp.arange(n_sb_max, dtype=i32)
    n_sb_total = sb_end[-1]
    sb_valid = s < n_sb_total
    sb_src = jnp.where(sb_valid, s, n_sb_total - 1).astype(i32)
    sb_e = jnp.minimum(jnp.searchsorted(sb_end, sb_src, side="right"), N_EXPERTS - 1).astype(i32)
    sb_local_s = sb_src - sb_start[sb_e]
    sb_nsub = jnp.where(sb_valid, base[sb_e] + (sb_local_s < rem[sb_e]).astype(i32), 0).astype(i32)
    return dest, row_tok, sb_e, sb_nsub


def kernel(x, c, rel_bias, norm_gain, ada_w, ada_b, attn_w_qkv, attn_b_qkv, attn_q_gain, attn_k_gain, attn_sinks, attn_w_o, attn_b_o, ssm_lam_re, ssm_lam_im, ssm_log_dt, ssm_b_re, ssm_b_im, ssm_c_re, ssm_c_im, ssm_d, ssm_w_glu_a, ssm_b_glu_a, ssm_w_glu_b, ssm_b_glu_b, moe_w_router, moe_b_router, moe_w_gate_up, moe_b_gate_up, moe_w_down, moe_b_down):
    bsz, seq, d = x.shape
    n_tok = bsz * seq
    depth = norm_gain.shape[0]
    tm, tn = 1024, 512
    SH1, SC1, G1, SH2, SC2, G2 = range(6)

    def gated(accs, extras):
        x_res, gate = extras
        return x_res + gate * accs[0]

    def gated_glu(accs, extras):
        x_res, gate = extras
        return x_res + gate * (accs[0] * jax.nn.sigmoid(accs[1]))

    for layer in range(depth):
        mod3 = _modulation(c, ada_w, ada_b, layer).reshape(bsz, 1, 6 * d)
        x2d = x.reshape(n_tok, d)
        res_extras = _gated_residual_extras(x2d, mod3, G1, d, tm, tn, seq)
        i = layer // 2
        if layer % 2 == 0:
            h = _norm(x, norm_gain[layer, 0], mod3, SC1, SH1, bf16)
            qkv = _dense(h.reshape(n_tok, d), [attn_w_qkv], [attn_b_qkv], i, lambda accs, extras: accs[0], [],
                         f32, tm, 512, "qkv_proj")
            o = _attention(qkv.reshape(bsz, seq, -1), attn_q_gain[i], attn_k_gain[i], attn_sinks[i], rel_bias)
            x2d = _dense(o.reshape(n_tok, -1), [attn_w_o], [attn_b_o], i, gated, res_extras, f32, tm, tn, "attn_out")
        else:
            h = _norm(x, norm_gain[layer, 0], mod3, SC1, SH1, f32)
            y = _ssm(h, ssm_lam_re[i], ssm_lam_im[i], ssm_log_dt[i], ssm_b_re[i], ssm_b_im[i],
                     ssm_c_re[i], ssm_c_im[i], ssm_d[i])
            x2d = _dense(y.reshape(n_tok, d), [ssm_w_glu_a, ssm_w_glu_b], [ssm_b_glu_a, ssm_b_glu_b], i,
                         gated_glu, res_extras, f32, tm, tn, "ssm_glu")
        x = x2d.reshape(bsz, seq, d)
        h, top_idx, gates = _norm(x, norm_gain[layer, 1], mod3, SC2, SH2, f32,
                                  router=(moe_w_router, moe_b_router, layer))
        idx_km = top_idx.reshape(n_tok, LANES)[:, :TOP_K].T
        dest, row_tok, sb_e, sb_nsub = _routing_tables(idx_km)
        y_sorted = _moe_experts(h, d // LANES, row_tok, sb_e, sb_nsub, moe_w_gate_up, moe_b_gate_up,
                                moe_w_down, moe_b_down, layer)
        x = _moe_combine(y_sorted, dest, gates.reshape(n_tok, LANES), x2d, mod3, G2, seq).reshape(bsz, seq, d)
    return x
```

```python
import functools
import math

import numpy as np
import jax
import jax.numpy as jnp
from jax import lax
from jax.experimental import pallas as pl
from jax.experimental.pallas import tpu as pltpu

f32 = jnp.float32
bf16 = jnp.bfloat16
i32 = jnp.int32

N_Q_HEADS = 32
N_KV_HEADS = 4
HEAD_DIM = 64
Q_PER_KV = N_Q_HEADS // N_KV_HEADS
WINDOW = 128
NUM_BUCKETS = 32
MAX_DISTANCE = 128
SSM_GROUP_CH = 16
SSM_STATE = 64
N_EXPERTS = 32
TOP_K = 4
SWIGLU_ALPHA = 1.702
SWIGLU_LIMIT = 7.0
NORM_EPS = 1e-5

LANES = 128
SUBLANES = 8
VMEM_LIMIT = 56 * 1024 * 1024
MOE_VMEM_LIMIT = 60 * 1024 * 1024
N_DMA_PRIORITIES = 2

SSM_TILE_CH = 128
SSM_TIME_CHUNK = 512
MOE_SUB = 256
MOE_SB_SUBS = 8
MOE_SB_ROWS = MOE_SUB * MOE_SB_SUBS
MOE_GROUP = 4
MOE_CHUNK = 256
MOE_DOWN_CHUNK = 512


def _cparams(n_axes, vmem_limit=VMEM_LIMIT):
    return pltpu.CompilerParams(dimension_semantics=("arbitrary",) * n_axes, vmem_limit_bytes=vmem_limit)


def _mod_kernel(c_ref, w_ref, b_ref, o_ref):
    c = c_ref[...]
    cond = c * jax.nn.sigmoid(c)
    o_ref[...] = jnp.dot(cond.astype(bf16), w_ref[...].astype(bf16), preferred_element_type=f32) + b_ref[...]


def _modulation(c, ada_w, ada_b, layer):
    bsz, d = c.shape
    n = ada_w.shape[-1]
    tn = 1024
    return pl.pallas_call(
        _mod_kernel,
        grid=(n // tn,),
        in_specs=[
            pl.BlockSpec((bsz, d), lambda j: (0, 0)),
            pl.BlockSpec((None, d, tn), lambda j: (layer, 0, j)),
            pl.BlockSpec((None, 1, tn), lambda j: (layer, 0, j)),
        ],
        out_specs=pl.BlockSpec((bsz, tn), lambda j: (0, j)),
        out_shape=jax.ShapeDtypeStruct((bsz, n), f32),
        compiler_params=_cparams(1),
        name="adaln_mod",
    )(c, ada_w, ada_b.reshape(ada_b.shape[0], 1, n))


def _norm_mod(x_ref, gain_ref, sc_ref, sh_ref):
    x = x_ref[0]
    ms = jnp.mean(x * x, axis=-1, keepdims=True)
    y = x * lax.rsqrt(ms + NORM_EPS) * gain_ref[...]
    return y * (1.0 + sc_ref[0]) + sh_ref[0]


def _norm_kernel(x_ref, gain_ref, sc_ref, sh_ref, h_ref):
    h_ref[0] = _norm_mod(x_ref, gain_ref, sc_ref, sh_ref).astype(h_ref.dtype)


def _split_bf16(v):
    hi = v.astype(bf16)
    lo = (v - hi.astype(f32)).astype(bf16)
    return hi, lo


def _norm_router_kernel(x_ref, gain_ref, sc_ref, sh_ref, wr_ref, br_ref, h_ref, idx_ref, gate_ref):
    h = _norm_mod(x_ref, gain_ref, sc_ref, sh_ref)
    ts, d = h.shape
    n_col = d // LANES
    for j in range(n_col):
        h_ref[pl.ds(j, ts, stride=n_col), :] = h[:, j * LANES:(j + 1) * LANES]
    h_hi, h_lo = _split_bf16(h)
    w_hi, w_lo = _split_bf16(wr_ref[...])
    dot = functools.partial(jnp.dot, preferred_element_type=f32)
    logits = dot(h_hi, w_hi) + (dot(h_hi, w_lo) + dot(h_lo, w_hi)) + br_ref[...]
    rows, n_exp = logits.shape
    col = lax.broadcasted_iota(i32, (rows, n_exp), 1)
    work = logits
    tops, idxs = [], []
    for _ in range(TOP_K):
        m = jnp.max(work, axis=-1, keepdims=True)
        idx = jnp.min(jnp.where(work == m, col, n_exp), axis=-1, keepdims=True)
        work = jnp.where(col == idx, -jnp.inf, work)
        tops.append(m)
        idxs.append(idx)
    es = [jnp.exp(t - tops[0]) for t in tops]
    denom = es[0] + es[1] + es[2] + es[3]
    lane = lax.broadcasted_iota(i32, (rows, LANES), 1)
    idx_out = jnp.zeros((rows, LANES), i32)
    gate_out = jnp.zeros((rows, LANES), f32)
    for k in range(TOP_K):
        idx_out = jnp.where(lane == k, idxs[k], idx_out)
        gate_out = jnp.where(lane == k, es[k] / denom, gate_out)
    idx_ref[0] = idx_out
    gate_ref[0] = gate_out


def _norm(x, gain, mod3, sc_chunk, sh_chunk, out_dtype, router=None):
    bsz, seq, d = x.shape
    ts = 512
    in_specs = [
        pl.BlockSpec((1, ts, d), lambda b, s: (b, s, 0)),
        pl.BlockSpec((1, d), lambda b, s: (0, 0)),
        pl.BlockSpec((1, 1, d), lambda b, s: (b, 0, sc_chunk)),
        pl.BlockSpec((1, 1, d), lambda b, s: (b, 0, sh_chunk)),
    ]
    h_spec = pl.BlockSpec((1, ts, d), lambda b, s: (b, s, 0))
    args = [x, gain.reshape(1, d), mod3, mod3]
    if router is None:
        return pl.pallas_call(
            _norm_kernel, grid=(bsz, seq // ts), in_specs=in_specs, out_specs=h_spec,
            out_shape=jax.ShapeDtypeStruct((bsz, seq, d), out_dtype),
            compiler_params=_cparams(2), name="norm_mod",
        )(*args)
    w_router, b_router, layer = router
    n_exp = w_router.shape[-1]
    in_specs += [
        pl.BlockSpec((None, d, n_exp), lambda b, s: (layer, 0, 0)),
        pl.BlockSpec((None, 1, n_exp), lambda b, s: (layer, 0, 0)),
    ]
    lane_spec = pl.BlockSpec((1, ts, LANES), lambda b, s: (b, s, 0))
    n_col = d // LANES
    h3_spec = pl.BlockSpec((ts * n_col, LANES), lambda b, s: (b * (seq // ts) + s, 0))
    return pl.pallas_call(
        _norm_router_kernel, grid=(bsz, seq // ts), in_specs=in_specs,
        out_specs=[h3_spec, lane_spec, lane_spec],
        out_shape=[jax.ShapeDtypeStruct((bsz * seq * n_col, LANES), f32),
                   jax.ShapeDtypeStruct((bsz, seq, LANES), i32),
                   jax.ShapeDtypeStruct((bsz, seq, LANES), f32)],
        compiler_params=_cparams(2), name="norm_router",
    )(*args, w_router, b_router.reshape(b_router.shape[0], 1, n_exp))


def _dense_kernel(*refs, n_w, n_extra, epilogue):
    x_ref = refs[0]
    w_refs = refs[1:1 + n_w]
    b_refs = refs[1 + n_w:1 + 2 * n_w]
    e_refs = refs[1 + 2 * n_w:1 + 2 * n_w + n_extra]
    o_ref = refs[1 + 2 * n_w + n_extra]
    wbf_refs = refs[2 + 2 * n_w + n_extra:]

    @pl.when(pl.program_id(1) == 0)
    def _():
        for w_ref, wbf in zip(w_refs, wbf_refs):
            wbf[...] = w_ref[...].astype(bf16)

    x = x_ref[...]
    accs = [jnp.dot(x, wbf[...], preferred_element_type=f32) + b_ref[...] for wbf, b_ref in zip(wbf_refs, b_refs)]
    o_ref[...] = epilogue(accs, [e[...] for e in e_refs]).astype(o_ref.dtype)


def _dense(x, ws, bs, layer, epilogue, extras, out_dtype, tm, tn, name):
    m, k = x.shape
    n = ws[0].shape[-1]
    in_specs = [pl.BlockSpec((tm, k), lambda j, i: (i, 0))]
    in_specs += [pl.BlockSpec((None, k, tn), lambda j, i: (layer, 0, j)) for _ in ws]
    in_specs += [pl.BlockSpec((None, 1, tn), lambda j, i: (layer, 0, j)) for _ in bs]
    in_specs += [spec for _, spec in extras]
    args = [x] + list(ws) + [b.reshape(b.shape[0], 1, n) for b in bs] + [a for a, _ in extras]
    return pl.pallas_call(
        functools.partial(_dense_kernel, n_w=len(ws), n_extra=len(extras), epilogue=epilogue),
        grid=(n // tn, m // tm),
        in_specs=in_specs,
        out_specs=pl.BlockSpec((tm, tn), lambda j, i: (i, j)),
        out_shape=jax.ShapeDtypeStruct((m, n), out_dtype),
        scratch_shapes=[pltpu.VMEM((k, tn), bf16) for _ in ws],
        compiler_params=_cparams(2), name=name,
    )(*args)


def _gated_residual_extras(x2d, mod3, gate_chunk, d, tm, tn, seq):
    tiles_per_batch = seq // tm
    return [
        (x2d, pl.BlockSpec((tm, tn), lambda j, i: (i, j))),
        (mod3, pl.BlockSpec((None, 1, tn), lambda j, i: (i // tiles_per_batch, 0, gate_chunk * (d // tn) + j))),
    ]


def _attn_kernel(sinks_ref, q_ref, kp_ref, kc_ref, vp_ref, vc_ref, qg_ref, kg_ref, bias_ref, o_ref):
    n = pl.program_id(1)
    lane = lax.broadcasted_iota(i32, (1, LANES), 1)
    lo = lane < HEAD_DIM
    bq = q_ref.shape[1]

    def halfnorm(v, gain):
        sq = v * v
        s_lo = jnp.sum(jnp.where(lo, sq, 0.0), axis=-1, keepdims=True)
        s_hi = jnp.sum(jnp.where(lo, 0.0, sq), axis=-1, keepdims=True)
        ms = jnp.where(lo, s_lo, s_hi) * (1.0 / HEAD_DIM)
        return v * lax.rsqrt(ms + NORM_EPS) * gain

    k_all = jnp.concatenate([kp_ref[0], kc_ref[0]], axis=0)
    v_all = jnp.concatenate([vp_ref[0], vc_ref[0]], axis=0)
    kcol = lax.broadcasted_iota(i32, (1, 2 * bq), 1)
    key_ok = jnp.logical_or(kcol >= bq, n > 0)
    qgain = qg_ref[...]
    kgain = kg_ref[...]
    pairs = Q_PER_KV // 2
    for c in range(N_KV_HEADS // 2):
        kn = halfnorm(k_all[:, c * LANES:(c + 1) * LANES], kgain)
        kr = pltpu.roll(kn, HEAD_DIM, 1)
        vn = v_all[:, c * LANES:(c + 1) * LANES]
        vr = pltpu.roll(vn, HEAD_DIM, 1)
        for half in range(2):
            g = 2 * c + half
            k_src, k_rot = (kn, kr) if half == 0 else (kr, kn)
            v_src, v_rot = (vn, vr) if half == 0 else (vr, vn)
            k_par = [jnp.where(lo, k_src, 0.0).astype(bf16), jnp.where(lo, 0.0, k_rot).astype(bf16)]
            v_par = [jnp.where(lo, v_src, 0.0).astype(bf16), jnp.where(lo, 0.0, v_rot).astype(bf16)]
            qs = [halfnorm(q_ref[0, :, (g * pairs + p) * LANES:(g * pairs + p + 1) * LANES], qgain)
                  for p in range(pairs)]
            qg = (jnp.concatenate(qs, axis=0) * (1.0 / math.sqrt(HEAD_DIM))).astype(bf16)
            acc = None
            for par in range(2):
                s = lax.dot_general(qg, k_par[par], (((1,), (1,)), ((), ())), preferred_element_type=f32)
                s = s + bias_ref[g, par]
                s = jnp.where(key_ok, s, -jnp.inf)
                sink = jnp.concatenate(
                    [jnp.full((bq, 1), sinks_ref[g * Q_PER_KV + 2 * p + par], f32) for p in range(pairs)], axis=0)
                m = jnp.maximum(jnp.max(s, axis=-1, keepdims=True), sink)
                e = jnp.exp(s - m)
                den = jnp.sum(e, axis=-1, keepdims=True) + jnp.exp(sink - m)
                probs = (e * (1.0 / den)).astype(bf16)
                o = jnp.dot(probs, v_par[par], preferred_element_type=f32)
                acc = o if acc is None else acc + o
            for p in range(pairs):
                col = (g * pairs + p) * LANES
                o_ref[0, :, col:col + LANES] = acc[p * bq:(p + 1) * bq].astype(o_ref.dtype)


def _t5_bucket(dist):
    nn = np.maximum(dist, 0)
    max_exact = NUM_BUCKETS // 2
    large = max_exact + (np.log(np.maximum(nn, 1) / max_exact) / np.log(MAX_DISTANCE / max_exact)
                         * (NUM_BUCKETS - max_exact)).astype(np.int32)
    large = np.minimum(large, NUM_BUCKETS - 1)
    return np.where(nn < max_exact, nn, large).astype(np.int32)


def _attention(qkv, q_gain, k_gain, sinks, rel_bias):
    bsz, seq, _ = qkv.shape
    bq = WINDOW
    q_dim = N_Q_HEADS * HEAD_DIM
    kv_dim = N_KV_HEADS * HEAD_DIM
    pairs = Q_PER_KV // 2
    ql = np.arange(bq)[:, None]
    kl = np.arange(2 * bq)[None, :]
    dist = ql + bq - kl
    in_window = (dist >= 0) & (dist < WINDOW)
    bucket = jnp.asarray(_t5_bucket(dist).reshape(1, -1))
    onehot_t = (bucket == jnp.arange(NUM_BUCKETS, dtype=i32)[:, None]).astype(f32)
    bias = jnp.dot(rel_bias.astype(f32).T, onehot_t, precision=lax.Precision.HIGHEST).reshape(N_Q_HEADS, bq, 2 * bq)
    bias = jnp.where(jnp.asarray(in_window)[None], bias, -jnp.inf)
    bias = bias.reshape(N_KV_HEADS, pairs, 2, bq, 2 * bq)
    bias = jnp.transpose(bias, (0, 2, 1, 3, 4)).reshape(N_KV_HEADS, 2, pairs * bq, 2 * bq)
    gain2 = lambda gn: jnp.concatenate([gn, gn]).reshape(1, LANES).astype(f32)
    k_blk = q_dim // kv_dim
    grid_spec = pltpu.PrefetchScalarGridSpec(
        num_scalar_prefetch=1,
        grid=(bsz, seq // bq),
        in_specs=[
            pl.BlockSpec((1, bq, q_dim), lambda b, n, s: (b, n, 0)),
            pl.BlockSpec((1, bq, kv_dim), lambda b, n, s: (b, jnp.maximum(n - 1, 0), k_blk)),
            pl.BlockSpec((1, bq, kv_dim), lambda b, n, s: (b, n, k_blk)),
            pl.BlockSpec((1, bq, kv_dim), lambda b, n, s: (b, jnp.maximum(n - 1, 0), k_blk + 1)),
            pl.BlockSpec((1, bq, kv_dim), lambda b, n, s: (b, n, k_blk + 1)),
            pl.BlockSpec((1, LANES), lambda b, n, s: (0, 0)),
            pl.BlockSpec((1, LANES), lambda b, n, s: (0, 0)),
            pl.BlockSpec((N_KV_HEADS, 2, pairs * bq, 2 * bq), lambda b, n, s: (0, 0, 0, 0)),
        ],
        out_specs=pl.BlockSpec((1, bq, q_dim), lambda b, n, s: (b, n, 0)),
    )
    return pl.pallas_call(
        _attn_kernel, grid_spec=grid_spec,
        out_shape=jax.ShapeDtypeStruct((bsz, seq, q_dim), bf16),
        compiler_params=_cparams(2), name="swa_attention",
    )(sinks.astype(f32), qkv, qkv, qkv, qkv, qkv, gain2(q_gain), gain2(k_gain), bias)


def _gelu_tanh(y):
    return 0.5 * y * (1.0 + jnp.tanh(math.sqrt(2.0 / math.pi) * (y + 0.044715 * (y * y * y))))


def _ssm_kernel(u_ref, bm_ref, cm_ref, are_ref, aim_ref, d_ref, y_ref, us, xs, st, ys):
    bsz, ts, _ = u_ref.shape
    n_state = are_ref.shape[-1]

    @pl.when(pl.program_id(1) == 0)
    def _():
        st[...] = jnp.zeros_like(st)

    n_slab = us.shape[0]
    for h in range(n_slab):
        for b in range(bsz):
            us[h, pl.ds(b, ts, stride=bsz), :] = u_ref[b, :, h * LANES:(h + 1) * LANES]
    u = jnp.concatenate([us[h] for h in range(n_slab)], axis=1)
    xs[...] = jnp.dot(u.astype(bf16), bm_ref[0], preferred_element_type=f32)
    a_re = jnp.broadcast_to(are_ref[0], (bsz, n_state))
    a_im = jnp.broadcast_to(aim_ref[0], (bsz, n_state))

    def step(t, carry):
        s_re, s_im = carry
        r0 = pl.multiple_of(t * bsz, bsz)
        x_re = xs[pl.ds(r0, bsz), 0:n_state]
        x_im = xs[pl.ds(r0, bsz), n_state:2 * n_state]
        n_re = a_re * s_re - a_im * s_im + x_re
        n_im = a_re * s_im + a_im * s_re + x_im
        xs[pl.ds(r0, bsz), 0:n_state] = n_re
        xs[pl.ds(r0, bsz), n_state:2 * n_state] = n_im
        return n_re, n_im

    s_re, s_im = lax.fori_loop(0, ts, step, (st[0], st[1]), unroll=8)
    st[0] = s_re
    st[1] = s_im
    y = (jnp.dot(xs[:, 0:n_state].astype(bf16), cm_ref[0, 0:n_state, :], preferred_element_type=f32)
         + jnp.dot(xs[:, n_state:2 * n_state].astype(bf16), cm_ref[0, n_state:2 * n_state, :],
                   preferred_element_type=f32)
         + d_ref[0] * u)
    y = _gelu_tanh(y)
    for h in range(n_slab):
        ys[h] = y[:, h * LANES:(h + 1) * LANES]
        for b in range(bsz):
            y_ref[b, :, h * LANES:(h + 1) * LANES] = ys[h, pl.ds(b, ts, stride=bsz), :].astype(y_ref.dtype)


def _ssm(h, lam_re, lam_im, log_dt, b_re, b_im, c_re, c_im, d_skip):
    bsz, seq, width = h.shape
    assert bsz == SUBLANES
    n_grp, n_st = lam_re.shape
    tile_ch = SSM_TILE_CH
    gpt = tile_ch // SSM_GROUP_CH
    n_tiles = width // tile_ch
    ns = gpt * n_st
    dt = jnp.exp(log_dt.astype(f32))[:, None]
    lr, li = lam_re.astype(f32), lam_im.astype(f32)
    mag = jnp.exp(lr * dt)
    ab_re, ab_im = mag * jnp.cos(li * dt), mag * jnp.sin(li * dt)
    den = lr * lr + li * li
    nr, ni = ab_re - 1.0, ab_im
    f_re = (nr * lr + ni * li) / den
    f_im = (ni * lr - nr * li) / den
    br, bi = b_re.astype(f32), b_im.astype(f32)
    bb_re = f_re[..., None] * br - f_im[..., None] * bi
    bb_im = f_re[..., None] * bi + f_im[..., None] * br
    eye = jnp.eye(gpt, dtype=f32)

    def blockdiag_in(bb):
        t = bb.reshape(n_tiles, gpt, n_st, SSM_GROUP_CH)
        return jnp.einsum("tgpc,gh->tgchp", t, eye).reshape(n_tiles, tile_ch, ns)

    def blockdiag_out(cc):
        t = cc.reshape(n_tiles, gpt, SSM_GROUP_CH, n_st)
        return jnp.einsum("tgcp,gh->tgphc", t, eye).reshape(n_tiles, ns, tile_ch)

    bm = jnp.concatenate([blockdiag_in(bb_re), blockdiag_in(bb_im)], axis=-1).astype(bf16)
    cm = jnp.concatenate([blockdiag_out(c_re.astype(f32)), -blockdiag_out(c_im.astype(f32))], axis=1).astype(bf16)
    a_re = ab_re.reshape(n_tiles, 1, ns)
    a_im = ab_im.reshape(n_tiles, 1, ns)
    dsk = d_skip.astype(f32).reshape(n_tiles, 1, tile_ch)
    ts = SSM_TIME_CHUNK
    return pl.pallas_call(
        _ssm_kernel,
        grid=(n_tiles, seq // ts),
        in_specs=[
            pl.BlockSpec((bsz, ts, tile_ch), lambda g, t: (0, t, g)),
            pl.BlockSpec((1, tile_ch, 2 * ns), lambda g, t: (g, 0, 0)),
            pl.BlockSpec((1, 2 * ns, tile_ch), lambda g, t: (g, 0, 0)),
            pl.BlockSpec((1, 1, ns), lambda g, t: (g, 0, 0)),
            pl.BlockSpec((1, 1, ns), lambda g, t: (g, 0, 0)),
            pl.BlockSpec((1, 1, tile_ch), lambda g, t: (g, 0, 0)),
        ],
        out_specs=pl.BlockSpec((bsz, ts, tile_ch), lambda g, t: (0, t, g)),
        out_shape=jax.ShapeDtypeStruct((bsz, seq, width), bf16),
        scratch_shapes=[
            pltpu.VMEM((tile_ch // LANES, ts * bsz, LANES), f32),
            pltpu.VMEM((ts * bsz, 2 * ns), f32),
            pltpu.VMEM((2, bsz, ns), f32),
            pltpu.VMEM((tile_ch // LANES, ts * bsz, LANES), f32),
        ],
        compiler_params=_cparams(2), name="s5_ssm",
    )(h, bm, cm, a_re, a_im, dsk)


GATHER_PITCH = 24


def _swiglu(g, l):
    glu = jnp.minimum(g, SWIGLU_LIMIT)
    lin = jnp.clip(l, -SWIGLU_LIMIT, SWIGLU_LIMIT)
    return glu * jax.nn.sigmoid(SWIGLU_ALPHA * glu) * (lin + 1.0)


def _for_row_groups(nsub, cast, fn):
    n_big = nsub // MOE_GROUP
    big_rows = MOE_GROUP * MOE_SUB

    @pl.when(n_big > 0)
    def _():
        fn(0, big_rows, cast())

    @pl.when(n_big == 0)
    def _():
        fn(0, MOE_SUB, cast())

    def big(i, carry):
        fn(pl.multiple_of(i * big_rows, big_rows), big_rows, None)
        return carry

    def small(i, carry):
        fn(pl.multiple_of(i * MOE_SUB, MOE_SUB), MOE_SUB, None)
        return carry

    lax.fori_loop(1, n_big, big, 0)
    lax.fori_loop(jnp.maximum(n_big * MOE_GROUP, 1), nsub, small, 0)


def _moe_kernel(sbe_ref, sbn_ref, idx0_ref, idxn_ref, h_hbm, wg_ref, wl_ref, bgu_ref, wd_ref, bd_ref, o_ref,
                xbf, stg, sem, a_scr, wg_bf, wl_bf, wd_bf, *, n_chunks):
    s = pl.program_id(0)
    c = pl.program_id(1)
    n_sb = pl.num_programs(0)
    nsub = sbn_ref[s]
    slot = lax.rem(s, 2)
    nsub_next = jnp.where(s + 1 < n_sb, sbn_ref[jnp.minimum(s + 1, n_sb - 1)], 0)
    n_col = xbf.shape[2] // LANES
    stg_rows = MOE_SUB * GATHER_PITCH

    def issue(idx_ref, off, st_slot):
        def body(r2, carry):
            for prio in range(N_DMA_PRIORITIES):
                r = r2 * N_DMA_PRIORITIES + prio
                src = pl.multiple_of(idx_ref[0, 0, off + r] * n_col, SUBLANES)
                dst = pl.multiple_of(st_slot * stg_rows + r * GATHER_PITCH, SUBLANES)
                pltpu.make_async_copy(h_hbm.at[pl.ds(src, n_col)], stg.at[pl.ds(dst, n_col)],
                                      sem.at[st_slot]).start(priority=prio)
            return carry

        lax.fori_loop(0, MOE_SUB // N_DMA_PRIORITIES, body, 0, unroll=4)

    def finish(st_slot, x_slot, sub):
        base = pl.multiple_of(st_slot * stg_rows, SUBLANES)
        pltpu.make_async_copy(h_hbm.at[pl.ds(0, MOE_SUB * n_col)], stg.at[pl.ds(base, MOE_SUB * n_col)],
                              sem.at[st_slot]).wait()
        r0 = pl.multiple_of(sub * MOE_SUB, MOE_SUB)
        for j in range(n_col):
            piece = stg[pl.ds(base + j, MOE_SUB, stride=GATHER_PITCH), :]
            xbf[x_slot, pl.ds(r0, MOE_SUB), j * LANES:(j + 1) * LANES] = piece.astype(bf16)

    @pl.when(jnp.logical_and(s == 0, c == 0))
    def _():
        def first(j, carry):
            issue(idx0_ref, j * MOE_SUB, 0)
            finish(0, 0, j)
            return carry

        lax.fori_loop(0, nsub, first, 0)

    @pl.when(jnp.logical_and(c >= 1, c - 1 < nsub_next))
    def _():
        finish(lax.rem(c - 1, 2), 1 - slot, c - 1)

    @pl.when(c < nsub_next)
    def _():
        issue(idxn_ref, c * MOE_SUB, lax.rem(c, 2))

    @pl.when(jnp.logical_and(c < n_chunks, nsub > 0))
    def _():
        def cast():
            wg, wl = wg_ref[...].astype(bf16), wl_ref[...].astype(bf16)
            wg_bf[...] = wg
            wl_bf[...] = wl
            return wg, wl

        def up(r, rows, weights):
            wg, wl = weights if weights is not None else (wg_bf[...], wl_bf[...])
            xs = xbf[slot, pl.ds(r, rows), :]
            g = jnp.dot(xs, wg, preferred_element_type=f32) + bgu_ref[pl.ds(c, 1), :]
            l = jnp.dot(xs, wl, preferred_element_type=f32) + bgu_ref[pl.ds(n_chunks + c, 1), :]
            a_scr[c, pl.ds(r, rows), :] = _swiglu(g, l).astype(bf16)

        _for_row_groups(nsub, cast, up)

    @pl.when(jnp.logical_and(c >= n_chunks, nsub > 0))
    def _():
        def cast():
            wd = wd_ref[...].astype(bf16)
            wd_bf[...] = wd
            return wd

        def down(r, rows, weights):
            wd = weights if weights is not None else wd_bf[...]
            a = jnp.concatenate([a_scr[cc, pl.ds(r, rows), :] for cc in range(n_chunks)], axis=1)
            o_ref[pl.ds(r, rows), :] = jnp.dot(a, wd, preferred_element_type=f32) + bd_ref[pl.ds(c - n_chunks, 1), :]

        _for_row_groups(nsub, cast, down)

    @pl.when(c >= n_chunks)
    def _():
        def zero(i, carry):
            r = pl.multiple_of(i * MOE_SUB, MOE_SUB)
            o_ref[pl.ds(r, MOE_SUB), :] = jnp.zeros((MOE_SUB, o_ref.shape[1]), f32)
            return carry

        lax.fori_loop(nsub, MOE_SB_SUBS, zero, 0)


def _moe_experts(h_rows, n_col, row_tok, sb_expert, sb_nsub, w_gate_up, b_gate_up, w_down, b_down, layer):
    n_rows = row_tok.shape[0]
    d = n_col * LANES
    n_sb = n_rows // MOE_SB_ROWS
    d_exp = w_down.shape[2]
    n_chunks = d_exp // MOE_CHUNK
    n_out_chunks = d // MOE_DOWN_CHUNK
    last = n_chunks - 1
    last_out = n_out_chunks - 1

    def valid(s, n):
        return n[s] > 0

    def wg_map(s, c, e, n):
        return (layer, e[s], 0, jnp.where(valid(s, n), jnp.minimum(c, last), last))

    def wl_map(s, c, e, n):
        return (layer, e[s], 0, n_chunks + jnp.where(valid(s, n), jnp.minimum(c, last), last))

    def wd_map(s, c, e, n):
        return (layer, e[s], 0, jnp.where(valid(s, n), jnp.maximum(c - n_chunks, 0), last_out))

    def out_map(s, c, e, n):
        return (s, jnp.maximum(c - n_chunks, 0))

    def next_idx_map(s, c, e, n):
        return (jnp.minimum(s + 1, n_sb - 1), 0, 0)

    def bias_map(s, c, e, n):
        return (layer, e[s], 0, 0)

    grid_spec = pltpu.PrefetchScalarGridSpec(
        num_scalar_prefetch=2,
        grid=(n_sb, n_chunks + n_out_chunks),
        in_specs=[
            pl.BlockSpec((1, 1, MOE_SB_ROWS), lambda s, c, e, n: (0, 0, 0), memory_space=pltpu.SMEM),
            pl.BlockSpec((1, 1, MOE_SB_ROWS), next_idx_map, memory_space=pltpu.SMEM),
            pl.BlockSpec(memory_space=pl.ANY),
            pl.BlockSpec((None, None, d, MOE_CHUNK), wg_map),
            pl.BlockSpec((None, None, d, MOE_CHUNK), wl_map),
            pl.BlockSpec((None, None, 2 * n_chunks, MOE_CHUNK), bias_map),
            pl.BlockSpec((None, None, d_exp, MOE_DOWN_CHUNK), wd_map),
            pl.BlockSpec((None, None, n_out_chunks, MOE_DOWN_CHUNK), bias_map),
        ],
        out_specs=pl.BlockSpec((MOE_SB_ROWS, MOE_DOWN_CHUNK), out_map),
        scratch_shapes=[
            pltpu.VMEM((2, MOE_SB_ROWS, d), bf16),
            pltpu.VMEM((2 * MOE_SUB * GATHER_PITCH, LANES), f32),
            pltpu.SemaphoreType.DMA((2,)),
            pltpu.VMEM((n_chunks, MOE_SB_ROWS, MOE_CHUNK), bf16),
            pltpu.VMEM((d, MOE_CHUNK), bf16),
            pltpu.VMEM((d, MOE_CHUNK), bf16),
            pltpu.VMEM((d_exp, MOE_DOWN_CHUNK), bf16),
        ],
    )
    n_l, n_e = b_gate_up.shape[:2]
    return pl.pallas_call(
        functools.partial(_moe_kernel, n_chunks=n_chunks), grid_spec=grid_spec,
        out_shape=jax.ShapeDtypeStruct((n_rows, d), f32),
        compiler_params=_cparams(2, MOE_VMEM_LIMIT), name="moe_experts",
    )(sb_expert, sb_nsub, row_tok.reshape(n_sb, 1, MOE_SB_ROWS), row_tok.reshape(n_sb, 1, MOE_SB_ROWS),
      h_rows, w_gate_up, w_gate_up,
      b_gate_up.reshape(n_l, n_e, 2 * n_chunks, MOE_CHUNK),
      w_down, b_down.reshape(n_l, n_e, n_out_chunks, MOE_DOWN_CHUNK))


def _combine_kernel(dest_ref, dest_next_ref, y_hbm, gate_ref, x_ref, g2_ref, o_ref, ybuf, sem):
    i = pl.program_id(0)
    n = pl.num_programs(0)
    tm = x_ref.shape[0]
    slot = lax.rem(i, 2)

    def issue(dest_blk, sl):
        for k in range(TOP_K):
            def body(t2, carry, k=k):
                for prio in range(N_DMA_PRIORITIES):
                    t = t2 * N_DMA_PRIORITIES + prio
                    pltpu.make_async_copy(y_hbm.at[pl.ds(dest_blk[0, 0, k * tm + t], 1)],
                                          ybuf.at[sl * TOP_K + k, pl.ds(t, 1)], sem.at[sl]).start(priority=prio)
                return carry

            lax.fori_loop(0, tm // N_DMA_PRIORITIES, body, 0, unroll=4)

    @pl.when(i == 0)
    def _():
        issue(dest_ref, 0)

    @pl.when(i + 1 < n)
    def _():
        issue(dest_next_ref, 1 - slot)

    for k in range(TOP_K):
        pltpu.make_async_copy(y_hbm.at[pl.ds(0, tm)], ybuf.at[slot * TOP_K + k], sem.at[slot]).wait()
    gates = gate_ref[...]
    acc = gates[:, 0:1] * ybuf[slot * TOP_K]
    for k in range(1, TOP_K):
        acc = acc + gates[:, k:k + 1] * ybuf[slot * TOP_K + k]
    o_ref[...] = x_ref[...] + g2_ref[...] * acc


def _moe_combine(y_sorted, dest, gates, x2d, mod3, gate_chunk, seq):
    n_tok, d = x2d.shape
    tm = 128
    n_tiles = n_tok // tm
    dest_km = jnp.transpose(dest.reshape(TOP_K, n_tiles, tm), (1, 0, 2)).reshape(n_tiles, 1, TOP_K * tm)
    tiles_per_batch = seq // tm
    return pl.pallas_call(
        _combine_kernel,
        grid=(n_tiles,),
        in_specs=[
            pl.BlockSpec((1, 1, TOP_K * tm), lambda i: (i, 0, 0), memory_space=pltpu.SMEM),
            pl.BlockSpec((1, 1, TOP_K * tm), lambda i: (jnp.minimum(i + 1, n_tiles - 1), 0, 0),
                         memory_space=pltpu.SMEM),
            pl.BlockSpec(memory_space=pl.ANY),
            pl.BlockSpec((tm, LANES), lambda i: (i, 0)),
            pl.BlockSpec((tm, d), lambda i: (i, 0)),
            pl.BlockSpec((None, 1, d), lambda i: (i // tiles_per_batch, 0, gate_chunk)),
        ],
        out_specs=pl.BlockSpec((tm, d), lambda i: (i, 0)),
        out_shape=jax.ShapeDtypeStruct((n_tok, d), f32),
        scratch_shapes=[pltpu.VMEM((2 * TOP_K, tm, d), f32), pltpu.SemaphoreType.DMA((2,))],
        compiler_params=_cparams(1), name="moe_combine",
    )(dest_km, dest_km, y_sorted, gates, x2d, mod3)


def _routing_tables(idx_km):
    n_tok = idx_km.shape[1]
    n_assign = n_tok * TOP_K
    n_sb_max = (n_assign // MOE_SUB + N_EXPERTS + N_EXPERTS * (MOE_SB_SUBS - 1)) // MOE_SB_SUBS
    experts = jnp.arange(N_EXPERTS, dtype=i32)
    onehot = idx_km[:, :, None] == experts[None, None, :]
    per_tok = jnp.sum(onehot.astype(i32), axis=0)
    before = jnp.cumsum(per_tok, axis=0) - per_tok
    rank = jnp.sum(jnp.where(onehot, before[None], 0), axis=2)
    counts = before[-1] + per_tok[-1]
    n_sub_e = (counts + MOE_SUB - 1) // MOE_SUB
    n_sb_e = (n_sub_e + MOE_SB_SUBS - 1) // MOE_SB_SUBS
    sb_end = jnp.cumsum(n_sb_e)
    sb_start = sb_end - n_sb_e
    base = n_sub_e // jnp.maximum(n_sb_e, 1)
    rem = n_sub_e - base * n_sb_e
    tables = jnp.stack([sb_start, base, rem], axis=1).astype(bf16)
    looked = jnp.dot(onehot.astype(bf16).reshape(n_assign, N_EXPERTS), tables,
                     preferred_element_type=f32).astype(i32).reshape(TOP_K, n_tok, 3)
    a_start, a_base, a_rem = looked[..., 0], looked[..., 1], looked[..., 2]
    q = rank // MOE_SUB
    thr = a_rem * (a_base + 1)
    sb_local = jnp.where(q < thr, q // (a_base + 1), a_rem + (q - thr) // jnp.maximum(a_base, 1))
    sub_in = jnp.where(q < thr, q % (a_base + 1), (q - thr) % jnp.maximum(a_base, 1))
    dest = ((a_start + sb_local) * MOE_SB_ROWS + sub_in * MOE_SUB + rank % MOE_SUB).astype(i32)
    tok = jnp.broadcast_to(jnp.arange(n_tok, dtype=i32)[None, :], (TOP_K, n_tok))
    row_tok = jnp.zeros((n_sb_max * MOE_SB_ROWS,), i32).at[dest.reshape(-1)].set(
        tok.reshape(-1), unique_indices=True, mode="promise_in_bounds")
    s = jnp.arange(n_sb_max, dtype=i32)
    n_sb_total = sb_end[-1]
    sb_valid = s < n_sb_total
    sb_src = jnp.where(sb_valid, s, n_sb_total - 1).astype(i32)
    sb_e = jnp.minimum(jnp.searchsorted(sb_end, sb_src, side="right"), N_EXPERTS - 1).astype(i32)
    sb_local_s = sb_src - sb_start[sb_e]
    sb_nsub = jnp.where(sb_valid, base[sb_e] + (sb_local_s < rem[sb_e]).astype(i32), 0).astype(i32)
    return dest, row_tok, sb_e, sb_nsub


def kernel(x, c, rel_bias, norm_gain, ada_w, ada_b, attn_w_qkv, attn_b_qkv, attn_q_gain, attn_k_gain, attn_sinks, attn_w_o, attn_b_o, ssm_lam_re, ssm_lam_im, ssm_log_dt, ssm_b_re, ssm_b_im, ssm_c_re, ssm_c_im, ssm_d, ssm_w_glu_a, ssm_b_glu_a, ssm_w_glu_b, ssm_b_glu_b, moe_w_router, moe_b_router, moe_w_gate_up, moe_b_gate_up, moe_w_down, moe_b_down):
    bsz, seq, d = x.shape
    n_tok = bsz * seq
    depth = norm_gain.shape[0]
    tm, tn = 1024, 512
    SH1, SC1, G1, SH2, SC2, G2 = range(6)

    def gated(accs, extras):
        x_res, gate = extras
        return x_res + gate * accs[0]

    def gated_glu(accs, extras):
        x_res, gate = extras
        return x_res + gate * (accs[0] * jax.nn.sigmoid(accs[1]))

    for layer in range(depth):
        mod3 = _modulation(c, ada_w, ada_b, layer).reshape(bsz, 1, 6 * d)
        x2d = x.reshape(n_tok, d)
        res_extras = _gated_residual_extras(x2d, mod3, G1, d, tm, tn, seq)
        i = layer // 2
        if layer % 2 == 0:
            h = _norm(x, norm_gain[layer, 0], mod3, SC1, SH1, bf16)
            qkv = _dense(h.reshape(n_tok, d), [attn_w_qkv], [attn_b_qkv], i, lambda accs, extras: accs[0], [],
                         f32, tm, 512, "qkv_proj")
            o = _attention(qkv.reshape(bsz, seq, -1), attn_q_gain[i], attn_k_gain[i], attn_sinks[i], rel_bias)
            x2d = _dense(o.reshape(n_tok, -1), [attn_w_o], [attn_b_o], i, gated, res_extras, f32, tm, tn, "attn_out")
        else:
            h = _norm(x, norm_gain[layer, 0], mod3, SC1, SH1, f32)
            y = _ssm(h, ssm_lam_re[i], ssm_lam_im[i], ssm_log_dt[i], ssm_b_re[i], ssm_b_im[i],
                     ssm_c_re[i], ssm_c_im[i], ssm_d[i])
            x2d = _dense(y.reshape(n_tok, d), [ssm_w_glu_a, ssm_w_glu_b], [ssm_b_glu_a, ssm_b_glu_b], i,
                         gated_glu, res_extras, f32, tm, tn, "ssm_glu")
        x = x2d.reshape(bsz, seq, d)
        h, top_idx, gates = _norm(x, norm_gain[layer, 1], mod3, SC2, SH2, f32,
                                  router=(moe_w_router, moe_b_router, layer))
        idx_km = top_idx.reshape(n_tok, LANES)[:, :TOP_K].T
        dest, row_tok, sb_e, sb_nsub = _routing_tables(idx_km)
        y_sorted = _moe_experts(h, d // LANES, row_tok, sb_e, sb_nsub, moe_w_gate_up, moe_b_gate_up,
                                moe_w_down, moe_b_down, layer)
        x = _moe_combine(y_sorted, dest, gates.reshape(n_tok, LANES), x2d, mod3, G2, seq).reshape(bsz, seq, d)
    return x
```

```python
import functools
import math

import numpy as np
import jax
import jax.numpy as jnp
from jax import lax
from jax.experimental import pallas as pl
from jax.experimental.pallas import tpu as pltpu

f32 = jnp.float32
bf16 = jnp.bfloat16
i32 = jnp.int32

N_Q_HEADS = 32
N_KV_HEADS = 4
HEAD_DIM = 64
Q_PER_KV = N_Q_HEADS // N_KV_HEADS
WINDOW = 128
NUM_BUCKETS = 32
MAX_DISTANCE = 128
SSM_GROUP_CH = 16
N_EXPERTS = 32
TOP_K = 4
SWIGLU_ALPHA = 1.702
SWIGLU_LIMIT = 7.0
NORM_EPS = 1e-5

LANES = 128
SUBLANES = 8
VMEM_LIMIT = 56 * 1024 * 1024
MOE_VMEM_LIMIT = 60 * 1024 * 1024
N_DMA_PRIORITIES = 2

ADALN_TN = 1024
NORM_TS = 512
DENSE_TM = 1024
DENSE_TN = 512
COMBINE_TM = 128
SSM_TILE_CH = 128
SSM_TIME_CHUNK = 512
MOE_SUB = 256
MOE_SB_SUBS = 8
MOE_SB_ROWS = MOE_SUB * MOE_SB_SUBS
MOE_GROUP = 4
MOE_CHUNK = 256
MOE_DOWN_CHUNK = 512


def _cparams(n_axes, vmem_limit=VMEM_LIMIT):
    return pltpu.CompilerParams(dimension_semantics=("arbitrary",) * n_axes, vmem_limit_bytes=vmem_limit)


def _mod_kernel(c_ref, w_ref, b_ref, o_ref):
    c = c_ref[...]
    cond = c * jax.nn.sigmoid(c)
    o_ref[...] = jnp.dot(cond.astype(bf16), w_ref[...].astype(bf16), preferred_element_type=f32) + b_ref[...]


def _modulation(c, ada_w, ada_b, layer):
    bsz, d = c.shape
    n = ada_w.shape[-1]
    tn = ADALN_TN
    assert n % tn == 0
    return pl.pallas_call(
        _mod_kernel,
        grid=(n // tn,),
        in_specs=[
            pl.BlockSpec((bsz, d), lambda j: (0, 0)),
            pl.BlockSpec((None, d, tn), lambda j: (layer, 0, j)),
            pl.BlockSpec((None, 1, tn), lambda j: (layer, 0, j)),
        ],
        out_specs=pl.BlockSpec((bsz, tn), lambda j: (0, j)),
        out_shape=jax.ShapeDtypeStruct((bsz, n), f32),
        compiler_params=_cparams(1),
        name="adaln_mod",
    )(c, ada_w, ada_b.reshape(ada_b.shape[0], 1, n))


def _norm_mod(x_ref, gain_ref, sc_ref, sh_ref):
    x = x_ref[0]
    ms = jnp.mean(x * x, axis=-1, keepdims=True)
    y = x * lax.rsqrt(ms + NORM_EPS) * gain_ref[...]
    return y * (1.0 + sc_ref[0]) + sh_ref[0]


def _norm_kernel(x_ref, gain_ref, sc_ref, sh_ref, h_ref):
    h_ref[0] = _norm_mod(x_ref, gain_ref, sc_ref, sh_ref).astype(h_ref.dtype)


def _split_bf16(v):
    hi = v.astype(bf16)
    lo = (v - hi.astype(f32)).astype(bf16)
    return hi, lo


def _norm_router_kernel(x_ref, gain_ref, sc_ref, sh_ref, wr_ref, br_ref, h_ref, idx_ref, gate_ref):
    h = _norm_mod(x_ref, gain_ref, sc_ref, sh_ref)
    ts, d = h.shape
    n_col = d // LANES
    for j in range(n_col):
        h_ref[pl.ds(j, ts, stride=n_col), :] = h[:, j * LANES:(j + 1) * LANES]
    h_hi, h_lo = _split_bf16(h)
    w_hi, w_lo = _split_bf16(wr_ref[...])
    dot = functools.partial(jnp.dot, preferred_element_type=f32)
    logits = dot(h_hi, w_hi) + (dot(h_hi, w_lo) + dot(h_lo, w_hi)) + br_ref[...]
    rows, n_exp = logits.shape
    col = lax.broadcasted_iota(i32, (rows, n_exp), 1)
    work = logits
    tops, idxs = [], []
    for _ in range(TOP_K):
        m = jnp.max(work, axis=-1, keepdims=True)
        idx = jnp.min(jnp.where(work == m, col, n_exp), axis=-1, keepdims=True)
        work = jnp.where(col == idx, -jnp.inf, work)
        tops.append(m)
        idxs.append(idx)
    es = [jnp.exp(t - tops[0]) for t in tops]
    denom = functools.reduce(lambda a, b: a + b, es)
    lane = lax.broadcasted_iota(i32, (rows, LANES), 1)
    idx_out = jnp.zeros((rows, LANES), i32)
    gate_out = jnp.zeros((rows, LANES), f32)
    for k in range(TOP_K):
        idx_out = jnp.where(lane == k, idxs[k], idx_out)
        gate_out = jnp.where(lane == k, es[k] / denom, gate_out)
    idx_ref[0] = idx_out
    gate_ref[0] = gate_out


def _norm(x, gain, mod3, sc_chunk, sh_chunk, out_dtype, router=None):
    bsz, seq, d = x.shape
    ts = NORM_TS
    assert seq % ts == 0 and d % LANES == 0
    in_specs = [
        pl.BlockSpec((1, ts, d), lambda b, s: (b, s, 0)),
        pl.BlockSpec((1, d), lambda b, s: (0, 0)),
        pl.BlockSpec((1, 1, d), lambda b, s: (b, 0, sc_chunk)),
        pl.BlockSpec((1, 1, d), lambda b, s: (b, 0, sh_chunk)),
    ]
    h_spec = pl.BlockSpec((1, ts, d), lambda b, s: (b, s, 0))
    args = [x, gain.reshape(1, d), mod3, mod3]
    if router is None:
        return pl.pallas_call(
            _norm_kernel, grid=(bsz, seq // ts), in_specs=in_specs, out_specs=h_spec,
            out_shape=jax.ShapeDtypeStruct((bsz, seq, d), out_dtype),
            compiler_params=_cparams(2), name="norm_mod",
        )(*args)
    w_router, b_router, layer = router
    n_exp = w_router.shape[-1]
    in_specs += [
        pl.BlockSpec((None, d, n_exp), lambda b, s: (layer, 0, 0)),
        pl.BlockSpec((None, 1, n_exp), lambda b, s: (layer, 0, 0)),
    ]
    lane_spec = pl.BlockSpec((1, ts, LANES), lambda b, s: (b, s, 0))
    n_col = d // LANES
    h3_spec = pl.BlockSpec((ts * n_col, LANES), lambda b, s: (b * (seq // ts) + s, 0))
    return pl.pallas_call(
        _norm_router_kernel, grid=(bsz, seq // ts), in_specs=in_specs,
        out_specs=[h3_spec, lane_spec, lane_spec],
        out_shape=[jax.ShapeDtypeStruct((bsz * seq * n_col, LANES), f32),
                   jax.ShapeDtypeStruct((bsz, seq, LANES), i32),
                   jax.ShapeDtypeStruct((bsz, seq, LANES), f32)],
        compiler_params=_cparams(2), name="norm_router",
    )(*args, w_router, b_router.reshape(b_router.shape[0], 1, n_exp))


def _dense_kernel(*refs, n_w, n_extra, epilogue):
    x_ref = refs[0]
    w_refs = refs[1:1 + n_w]
    b_refs = refs[1 + n_w:1 + 2 * n_w]
    e_refs = refs[1 + 2 * n_w:1 + 2 * n_w + n_extra]
    o_ref = refs[1 + 2 * n_w + n_extra]
    wbf_refs = refs[2 + 2 * n_w + n_extra:]

    @pl.when(pl.program_id(1) == 0)
    def _():
        for w_ref, wbf in zip(w_refs, wbf_refs):
            wbf[...] = w_ref[...].astype(bf16)

    x = x_ref[...]
    accs = [jnp.dot(x, wbf[...], preferred_element_type=f32) + b_ref[...] for wbf, b_ref in zip(wbf_refs, b_refs)]
    o_ref[...] = epilogue(accs, [e[...] for e in e_refs]).astype(o_ref.dtype)


def _dense(x, ws, bs, layer, epilogue, extras, out_dtype, tm, tn, name):
    m, k = x.shape
    n = ws[0].shape[-1]
    assert m % tm == 0 and n % tn == 0
    in_specs = [pl.BlockSpec((tm, k), lambda j, i: (i, 0))]
    in_specs += [pl.BlockSpec((None, k, tn), lambda j, i: (layer, 0, j)) for _ in ws]
    in_specs += [pl.BlockSpec((None, 1, tn), lambda j, i: (layer, 0, j)) for _ in bs]
    in_specs += [spec for _, spec in extras]
    args = [x] + list(ws) + [b.reshape(b.shape[0], 1, n) for b in bs] + [a for a, _ in extras]
    return pl.pallas_call(
        functools.partial(_dense_kernel, n_w=len(ws), n_extra=len(extras), epilogue=epilogue),
        grid=(n // tn, m // tm),
        in_specs=in_specs,
        out_specs=pl.BlockSpec((tm, tn), lambda j, i: (i, j)),
        out_shape=jax.ShapeDtypeStruct((m, n), out_dtype),
        scratch_shapes=[pltpu.VMEM((k, tn), bf16) for _ in ws],
        compiler_params=_cparams(2), name=name,
    )(*args)


def _gated_residual_extras(x2d, mod3, gate_chunk, d, tm, tn, seq):
    tiles_per_batch = seq // tm
    return [
        (x2d, pl.BlockSpec((tm, tn), lambda j, i: (i, j))),
        (mod3, pl.BlockSpec((None, 1, tn), lambda j, i: (i // tiles_per_batch, 0, gate_chunk * (d // tn) + j))),
    ]


def _attn_kernel(sinks_ref, q_ref, kp_ref, kc_ref, vp_ref, vc_ref, qg_ref, kg_ref, bias_ref, o_ref):
    n = pl.program_id(1)
    lane = lax.broadcasted_iota(i32, (1, LANES), 1)
    lo = lane < HEAD_DIM
    bq = q_ref.shape[1]

    def halfnorm(v, gain):
        sq = v * v
        s_lo = jnp.sum(jnp.where(lo, sq, 0.0), axis=-1, keepdims=True)
        s_hi = jnp.sum(jnp.where(lo, 0.0, sq), axis=-1, keepdims=True)
        ms = jnp.where(lo, s_lo, s_hi) * (1.0 / HEAD_DIM)
        return v * lax.rsqrt(ms + NORM_EPS) * gain

    k_all = jnp.concatenate([kp_ref[0], kc_ref[0]], axis=0)
    v_all = jnp.concatenate([vp_ref[0], vc_ref[0]], axis=0)
    kcol = lax.broadcasted_iota(i32, (1, 2 * bq), 1)
    key_ok = jnp.logical_or(kcol >= bq, n > 0)
    qgain = qg_ref[...]
    kgain = kg_ref[...]
    pairs = Q_PER_KV // 2
    for c in range(N_KV_HEADS // 2):
        kn = halfnorm(k_all[:, c * LANES:(c + 1) * LANES], kgain)
        kr = pltpu.roll(kn, HEAD_DIM, 1)
        vn = v_all[:, c * LANES:(c + 1) * LANES]
        vr = pltpu.roll(vn, HEAD_DIM, 1)
        for half in range(2):
            g = 2 * c + half
            k_src, k_rot = (kn, kr) if half == 0 else (kr, kn)
            v_src, v_rot = (vn, vr) if half == 0 else (vr, vn)
            k_par = [jnp.where(lo, k_src, 0.0).astype(bf16), jnp.where(lo, 0.0, k_rot).astype(bf16)]
            v_par = [jnp.where(lo, v_src, 0.0).astype(bf16), jnp.where(lo, 0.0, v_rot).astype(bf16)]
            qs = [halfnorm(q_ref[0, :, (g * pairs + p) * LANES:(g * pairs + p + 1) * LANES], qgain)
                  for p in range(pairs)]
            qg = (jnp.concatenate(qs, axis=0) * (1.0 / math.sqrt(HEAD_DIM))).astype(bf16)
            acc = None
            for par in range(2):
                s = lax.dot_general(qg, k_par[par], (((1,), (1,)), ((), ())), preferred_element_type=f32)
                s = s + bias_ref[g, par]
                s = jnp.where(key_ok, s, -jnp.inf)
                sink = jnp.concatenate(
                    [jnp.full((bq, 1), sinks_ref[g * Q_PER_KV + 2 * p + par], f32) for p in range(pairs)], axis=0)
                m = jnp.maximum(jnp.max(s, axis=-1, keepdims=True), sink)
                e = jnp.exp(s - m)
                den = jnp.sum(e, axis=-1, keepdims=True) + jnp.exp(sink - m)
                probs = (e * (1.0 / den)).astype(bf16)
                o = jnp.dot(probs, v_par[par], preferred_element_type=f32)
                acc = o if acc is None else acc + o
            for p in range(pairs):
                col = (g * pairs + p) * LANES
                o_ref[0, :, col:col + LANES] = acc[p * bq:(p + 1) * bq].astype(o_ref.dtype)


def _t5_bucket(dist):
    nn = np.maximum(dist, 0)
    max_exact = NUM_BUCKETS // 2
    large = max_exact + (np.log(np.maximum(nn, 1) / max_exact) / np.log(MAX_DISTANCE / max_exact)
                         * (NUM_BUCKETS - max_exact)).astype(np.int32)
    large = np.minimum(large, NUM_BUCKETS - 1)
    return np.where(nn < max_exact, nn, large).astype(np.int32)


def _attention(qkv, q_gain, k_gain, sinks, rel_bias):
    bsz, seq, _ = qkv.shape
    bq = WINDOW
    q_dim = N_Q_HEADS * HEAD_DIM
    kv_dim = N_KV_HEADS * HEAD_DIM
    pairs = Q_PER_KV // 2
    ql = np.arange(bq)[:, None]
    kl = np.arange(2 * bq)[None, :]
    dist = ql + bq - kl
    in_window = (dist >= 0) & (dist < WINDOW)
    bucket = jnp.asarray(_t5_bucket(dist).reshape(1, -1))
    onehot_t = (bucket == jnp.arange(NUM_BUCKETS, dtype=i32)[:, None]).astype(f32)
    bias = jnp.dot(rel_bias.astype(f32).T, onehot_t, precision=lax.Precision.HIGHEST).reshape(N_Q_HEADS, bq, 2 * bq)
    bias = jnp.where(jnp.asarray(in_window)[None], bias, -jnp.inf)
    bias = bias.reshape(N_KV_HEADS, pairs, 2, bq, 2 * bq)
    bias = jnp.transpose(bias, (0, 2, 1, 3, 4)).reshape(N_KV_HEADS, 2, pairs * bq, 2 * bq)
    gain2 = lambda gn: jnp.concatenate([gn, gn]).reshape(1, LANES).astype(f32)
    k_blk = q_dim // kv_dim
    grid_spec = pltpu.PrefetchScalarGridSpec(
        num_scalar_prefetch=1,
        grid=(bsz, seq // bq),
        in_specs=[
            pl.BlockSpec((1, bq, q_dim), lambda b, n, s: (b, n, 0)),
            pl.BlockSpec((1, bq, kv_dim), lambda b, n, s: (b, jnp.maximum(n - 1, 0), k_blk)),
            pl.BlockSpec((1, bq, kv_dim), lambda b, n, s: (b, n, k_blk)),
            pl.BlockSpec((1, bq, kv_dim), lambda b, n, s: (b, jnp.maximum(n - 1, 0), k_blk + 1)),
            pl.BlockSpec((1, bq, kv_dim), lambda b, n, s: (b, n, k_blk + 1)),
            pl.BlockSpec((1, LANES), lambda b, n, s: (0, 0)),
            pl.BlockSpec((1, LANES), lambda b, n, s: (0, 0)),
            pl.BlockSpec((N_KV_HEADS, 2, pairs * bq, 2 * bq), lambda b, n, s: (0, 0, 0, 0)),
        ],
        out_specs=pl.BlockSpec((1, bq, q_dim), lambda b, n, s: (b, n, 0)),
    )
    return pl.pallas_call(
        _attn_kernel, grid_spec=grid_spec,
        out_shape=jax.ShapeDtypeStruct((bsz, seq, q_dim), bf16),
        compiler_params=_cparams(2), name="swa_attention",
    )(sinks.astype(f32), qkv, qkv, qkv, qkv, qkv, gain2(q_gain), gain2(k_gain), bias)


def _gelu_tanh(y):
    return 0.5 * y * (1.0 + jnp.tanh(math.sqrt(2.0 / math.pi) * (y + 0.044715 * (y * y * y))))


def _ssm_kernel(u_ref, bm_ref, cm_ref, are_ref, aim_ref, d_ref, y_ref, us, xs, st, ys):
    bsz, ts, _ = u_ref.shape
    n_state = are_ref.shape[-1]

    @pl.when(pl.program_id(1) == 0)
    def _():
        st[...] = jnp.zeros_like(st)

    n_slab = us.shape[0]
    for h in range(n_slab):
        for b in range(bsz):
            us[h, pl.ds(b, ts, stride=bsz), :] = u_ref[b, :, h * LANES:(h + 1) * LANES]
    u = jnp.concatenate([us[h] for h in range(n_slab)], axis=1)
    xs[...] = jnp.dot(u.astype(bf16), bm_ref[0], preferred_element_type=f32)
    a_re = jnp.broadcast_to(are_ref[0], (bsz, n_state))
    a_im = jnp.broadcast_to(aim_ref[0], (bsz, n_state))

    def step(t, carry):
        s_re, s_im = carry
        r0 = pl.multiple_of(t * bsz, bsz)
        x_re = xs[pl.ds(r0, bsz), 0:n_state]
        x_im = xs[pl.ds(r0, bsz), n_state:2 * n_state]
        n_re = a_re * s_re - a_im * s_im + x_re
        n_im = a_re * s_im + a_im * s_re + x_im
        xs[pl.ds(r0, bsz), 0:n_state] = n_re
        xs[pl.ds(r0, bsz), n_state:2 * n_state] = n_im
        return n_re, n_im

    s_re, s_im = lax.fori_loop(0, ts, step, (st[0], st[1]), unroll=8)
    st[0] = s_re
    st[1] = s_im
    y = (jnp.dot(xs[:, 0:n_state].astype(bf16), cm_ref[0, 0:n_state, :], preferred_element_type=f32)
         + jnp.dot(xs[:, n_state:2 * n_state].astype(bf16), cm_ref[0, n_state:2 * n_state, :],
                   preferred_element_type=f32)
         + d_ref[0] * u)
    y = _gelu_tanh(y)
    for h in range(n_slab):
        ys[h] = y[:, h * LANES:(h + 1) * LANES]
        for b in range(bsz):
            y_ref[b, :, h * LANES:(h + 1) * LANES] = ys[h, pl.ds(b, ts, stride=bsz), :].astype(y_ref.dtype)


def _ssm(h, lam_re, lam_im, log_dt, b_re, b_im, c_re, c_im, d_skip):
    bsz, seq, width = h.shape
    assert bsz == SUBLANES
    n_grp, n_st = lam_re.shape
    tile_ch = SSM_TILE_CH
    gpt = tile_ch // SSM_GROUP_CH
    n_tiles = width // tile_ch
    ns = gpt * n_st
    dt = jnp.exp(log_dt.astype(f32))[:, None]
    lr, li = lam_re.astype(f32), lam_im.astype(f32)
    mag = jnp.exp(lr * dt)
    ab_re, ab_im = mag * jnp.cos(li * dt), mag * jnp.sin(li * dt)
    den = lr * lr + li * li
    nr, ni = ab_re - 1.0, ab_im
    f_re = (nr * lr + ni * li) / den
    f_im = (ni * lr - nr * li) / den
    br, bi = b_re.astype(f32), b_im.astype(f32)
    bb_re = f_re[..., None] * br - f_im[..., None] * bi
    bb_im = f_re[..., None] * bi + f_im[..., None] * br
    eye = jnp.eye(gpt, dtype=f32)

    def blockdiag_in(bb):
        t = bb.reshape(n_tiles, gpt, n_st, SSM_GROUP_CH)
        return jnp.einsum("tgpc,gh->tgchp", t, eye).reshape(n_tiles, tile_ch, ns)

    def blockdiag_out(cc):
        t = cc.reshape(n_tiles, gpt, SSM_GROUP_CH, n_st)
        return jnp.einsum("tgcp,gh->tgphc", t, eye).reshape(n_tiles, ns, tile_ch)

    bm = jnp.concatenate([blockdiag_in(bb_re), blockdiag_in(bb_im)], axis=-1).astype(bf16)
    cm = jnp.concatenate([blockdiag_out(c_re.astype(f32)), -blockdiag_out(c_im.astype(f32))], axis=1).astype(bf16)
    a_re = ab_re.reshape(n_tiles, 1, ns)
    a_im = ab_im.reshape(n_tiles, 1, ns)
    dsk = d_skip.astype(f32).reshape(n_tiles, 1, tile_ch)
    ts = SSM_TIME_CHUNK
    return pl.pallas_call(
        _ssm_kernel,
        grid=(n_tiles, seq // ts),
        in_specs=[
            pl.BlockSpec((bsz, ts, tile_ch), lambda g, t: (0, t, g)),
            pl.BlockSpec((1, tile_ch, 2 * ns), lambda g, t: (g, 0, 0)),
            pl.BlockSpec((1, 2 * ns, tile_ch), lambda g, t: (g, 0, 0)),
            pl.BlockSpec((1, 1, ns), lambda g, t: (g, 0, 0)),
            pl.BlockSpec((1, 1, ns), lambda g, t: (g, 0, 0)),
            pl.BlockSpec((1, 1, tile_ch), lambda g, t: (g, 0, 0)),
        ],
        out_specs=pl.BlockSpec((bsz, ts, tile_ch), lambda g, t: (0, t, g)),
        out_shape=jax.ShapeDtypeStruct((bsz, seq, width), bf16),
        scratch_shapes=[
            pltpu.VMEM((tile_ch // LANES, ts * bsz, LANES), f32),
            pltpu.VMEM((ts * bsz, 2 * ns), f32),
            pltpu.VMEM((2, bsz, ns), f32),
            pltpu.VMEM((tile_ch // LANES, ts * bsz, LANES), f32),
        ],
        compiler_params=_cparams(2), name="s5_ssm",
    )(h, bm, cm, a_re, a_im, dsk)


GATHER_PITCH = 24


def _swiglu(g, l):
    glu = jnp.minimum(g, SWIGLU_LIMIT)
    lin = jnp.clip(l, -SWIGLU_LIMIT, SWIGLU_LIMIT)
    return glu * jax.nn.sigmoid(SWIGLU_ALPHA * glu) * (lin + 1.0)


def _for_row_groups(nsub, cast, fn):
    n_big = nsub // MOE_GROUP
    big_rows = MOE_GROUP * MOE_SUB

    @pl.when(n_big > 0)
    def _():
        fn(0, big_rows, cast())

    @pl.when(n_big == 0)
    def _():
        fn(0, MOE_SUB, cast())

    def big(i, carry):
        fn(pl.multiple_of(i * big_rows, big_rows), big_rows, None)
        return carry

    def small(i, carry):
        fn(pl.multiple_of(i * MOE_SUB, MOE_SUB), MOE_SUB, None)
        return carry

    lax.fori_loop(1, n_big, big, 0)
    lax.fori_loop(jnp.maximum(n_big * MOE_GROUP, 1), nsub, small, 0)


def _moe_kernel(sbe_ref, sbn_ref, idx0_ref, idxn_ref, h_hbm, wg_ref, wl_ref, bgu_ref, wd_ref, bd_ref, o_ref,
                xbf, stg, sem, a_scr, wg_bf, wl_bf, wd_bf, *, n_chunks):
    s = pl.program_id(0)
    c = pl.program_id(1)
    n_sb = pl.num_programs(0)
    nsub = sbn_ref[s]
    slot = lax.rem(s, 2)
    nsub_next = jnp.where(s + 1 < n_sb, sbn_ref[jnp.minimum(s + 1, n_sb - 1)], 0)
    n_col = xbf.shape[2] // LANES
    stg_rows = MOE_SUB * GATHER_PITCH

    def issue(idx_ref, off, st_slot):
        def body(r2, carry):
            for prio in range(N_DMA_PRIORITIES):
                r = r2 * N_DMA_PRIORITIES + prio
                src = pl.multiple_of(idx_ref[0, 0, off + r] * n_col, SUBLANES)
                dst = pl.multiple_of(st_slot * stg_rows + r * GATHER_PITCH, SUBLANES)
                pltpu.make_async_copy(h_hbm.at[pl.ds(src, n_col)], stg.at[pl.ds(dst, n_col)],
                                      sem.at[st_slot]).start(priority=prio)
            return carry

        lax.fori_loop(0, MOE_SUB // N_DMA_PRIORITIES, body, 0, unroll=4)

    def finish(st_slot, x_slot, sub):
        base = pl.multiple_of(st_slot * stg_rows, SUBLANES)
        pltpu.make_async_copy(h_hbm.at[pl.ds(0, MOE_SUB * n_col)], stg.at[pl.ds(base, MOE_SUB * n_col)],
                              sem.at[st_slot]).wait()
        r0 = pl.multiple_of(sub * MOE_SUB, MOE_SUB)
        for j in range(n_col):
            piece = stg[pl.ds(base + j, MOE_SUB, stride=GATHER_PITCH), :]
            xbf[x_slot, pl.ds(r0, MOE_SUB), j * LANES:(j + 1) * LANES] = piece.astype(bf16)

    @pl.when(jnp.logical_and(s == 0, c == 0))
    def _():
        def first(j, carry):
            issue(idx0_ref, j * MOE_SUB, 0)
            finish(0, 0, j)
            return carry

        lax.fori_loop(0, nsub, first, 0)

    @pl.when(jnp.logical_and(c >= 1, c - 1 < nsub_next))
    def _():
        finish(lax.rem(c - 1, 2), 1 - slot, c - 1)

    @pl.when(c < nsub_next)
    def _():
        issue(idxn_ref, c * MOE_SUB, lax.rem(c, 2))

    @pl.when(jnp.logical_and(c < n_chunks, nsub > 0))
    def _():
        def cast():
            wg, wl = wg_ref[...].astype(bf16), wl_ref[...].astype(bf16)
            wg_bf[...] = wg
            wl_bf[...] = wl
            return wg, wl

        def up(r, rows, weights):
            wg, wl = weights if weights is not None else (wg_bf[...], wl_bf[...])
            xs = xbf[slot, pl.ds(r, rows), :]
            g = jnp.dot(xs, wg, preferred_element_type=f32) + bgu_ref[pl.ds(c, 1), :]
            l = jnp.dot(xs, wl, preferred_element_type=f32) + bgu_ref[pl.ds(n_chunks + c, 1), :]
            a_scr[c, pl.ds(r, rows), :] = _swiglu(g, l).astype(bf16)

        _for_row_groups(nsub, cast, up)

    @pl.when(jnp.logical_and(c >= n_chunks, nsub > 0))
    def _():
        def cast():
            wd = wd_ref[...].astype(bf16)
            wd_bf[...] = wd
            return wd

        def down(r, rows, weights):
            wd = weights if weights is not None else wd_bf[...]
            a = jnp.concatenate([a_scr[cc, pl.ds(r, rows), :] for cc in range(n_chunks)], axis=1)
            o_ref[pl.ds(r, rows), :] = jnp.dot(a, wd, preferred_element_type=f32) + bd_ref[pl.ds(c - n_chunks, 1), :]

        _for_row_groups(nsub, cast, down)

    @pl.when(c >= n_chunks)
    def _():
        def zero(i, carry):
            r = pl.multiple_of(i * MOE_SUB, MOE_SUB)
            o_ref[pl.ds(r, MOE_SUB), :] = jnp.zeros((MOE_SUB, o_ref.shape[1]), f32)
            return carry

        lax.fori_loop(nsub, MOE_SB_SUBS, zero, 0)


def _moe_experts(h_rows, n_col, row_tok, sb_expert, sb_nsub, w_gate_up, b_gate_up, w_down, b_down, layer):
    n_rows = row_tok.shape[0]
    d = n_col * LANES
    n_sb = n_rows // MOE_SB_ROWS
    d_exp = w_down.shape[2]
    n_chunks = d_exp // MOE_CHUNK
    n_out_chunks = d // MOE_DOWN_CHUNK
    last = n_chunks - 1
    last_out = n_out_chunks - 1

    def valid(s, n):
        return n[s] > 0

    def wg_map(s, c, e, n):
        return (layer, e[s], 0, jnp.where(valid(s, n), jnp.minimum(c, last), last))

    def wl_map(s, c, e, n):
        return (layer, e[s], 0, n_chunks + jnp.where(valid(s, n), jnp.minimum(c, last), last))

    def wd_map(s, c, e, n):
        return (layer, e[s], 0, jnp.where(valid(s, n), jnp.maximum(c - n_chunks, 0), last_out))

    def out_map(s, c, e, n):
        return (s, jnp.maximum(c - n_chunks, 0))

    def next_idx_map(s, c, e, n):
        return (jnp.minimum(s + 1, n_sb - 1), 0, 0)

    def bias_map(s, c, e, n):
        return (layer, e[s], 0, 0)

    grid_spec = pltpu.PrefetchScalarGridSpec(
        num_scalar_prefetch=2,
        grid=(n_sb, n_chunks + n_out_chunks),
        in_specs=[
            pl.BlockSpec((1, 1, MOE_SB_ROWS), lambda s, c, e, n: (0, 0, 0), memory_space=pltpu.SMEM),
            pl.BlockSpec((1, 1, MOE_SB_ROWS), next_idx_map, memory_space=pltpu.SMEM),
            pl.BlockSpec(memory_space=pl.ANY),
            pl.BlockSpec((None, None, d, MOE_CHUNK), wg_map),
            pl.BlockSpec((None, None, d, MOE_CHUNK), wl_map),
            pl.BlockSpec((None, None, 2 * n_chunks, MOE_CHUNK), bias_map),
            pl.BlockSpec((None, None, d_exp, MOE_DOWN_CHUNK), wd_map),
            pl.BlockSpec((None, None, n_out_chunks, MOE_DOWN_CHUNK), bias_map),
        ],
        out_specs=pl.BlockSpec((MOE_SB_ROWS, MOE_DOWN_CHUNK), out_map),
        scratch_shapes=[
            pltpu.VMEM((2, MOE_SB_ROWS, d), bf16),
            pltpu.VMEM((2 * MOE_SUB * GATHER_PITCH, LANES), f32),
            pltpu.SemaphoreType.DMA((2,)),
            pltpu.VMEM((n_chunks, MOE_SB_ROWS, MOE_CHUNK), bf16),
            pltpu.VMEM((d, MOE_CHUNK), bf16),
            pltpu.VMEM((d, MOE_CHUNK), bf16),
            pltpu.VMEM((d_exp, MOE_DOWN_CHUNK), bf16),
        ],
    )
    n_l, n_e = b_gate_up.shape[:2]
    return pl.pallas_call(
        functools.partial(_moe_kernel, n_chunks=n_chunks), grid_spec=grid_spec,
        out_shape=jax.ShapeDtypeStruct((n_rows, d), f32),
        compiler_params=_cparams(2, MOE_VMEM_LIMIT), name="moe_experts",
    )(sb_expert, sb_nsub, row_tok.reshape(n_sb, 1, MOE_SB_ROWS), row_tok.reshape(n_sb, 1, MOE_SB_ROWS),
      h_rows, w_gate_up, w_gate_up,
      b_gate_up.reshape(n_l, n_e, 2 * n_chunks, MOE_CHUNK),
      w_down, b_down.reshape(n_l, n_e, n_out_chunks, MOE_DOWN_CHUNK))


def _combine_kernel(dest_ref, dest_next_ref, y_hbm, gate_ref, x_ref, g2_ref, o_ref, ybuf, sem):
    i = pl.program_id(0)
    n = pl.num_programs(0)
    tm = x_ref.shape[0]
    slot = lax.rem(i, 2)

    def issue(dest_blk, sl):
        for k in range(TOP_K):
            def body(t2, carry, k=k):
                for prio in range(N_DMA_PRIORITIES):
                    t = t2 * N_DMA_PRIORITIES + prio
                    pltpu.make_async_copy(y_hbm.at[pl.ds(dest_blk[0, 0, k * tm + t], 1)],
                                          ybuf.at[sl * TOP_K + k, pl.ds(t, 1)], sem.at[sl]).start(priority=prio)
                return carry

            lax.fori_loop(0, tm // N_DMA_PRIORITIES, body, 0, unroll=4)

    @pl.when(i == 0)
    def _():
        issue(dest_ref, 0)

    @pl.when(i + 1 < n)
    def _():
        issue(dest_next_ref, 1 - slot)

    for k in range(TOP_K):
        pltpu.make_async_copy(y_hbm.at[pl.ds(0, tm)], ybuf.at[slot * TOP_K + k], sem.at[slot]).wait()
    gates = gate_ref[...]
    acc = gates[:, 0:1] * ybuf[slot * TOP_K]
    for k in range(1, TOP_K):
        acc = acc + gates[:, k:k + 1] * ybuf[slot * TOP_K + k]
    o_ref[...] = x_ref[...] + g2_ref[...] * acc


def _moe_combine(y_sorted, dest, gates, x2d, mod3, gate_chunk, seq):
    n_tok, d = x2d.shape
    tm = COMBINE_TM
    assert seq % tm == 0
    n_tiles = n_tok // tm
    dest_km = jnp.transpose(dest.reshape(TOP_K, n_tiles, tm), (1, 0, 2)).reshape(n_tiles, 1, TOP_K * tm)
    tiles_per_batch = seq // tm
    return pl.pallas_call(
        _combine_kernel,
        grid=(n_tiles,),
        in_specs=[
            pl.BlockSpec((1, 1, TOP_K * tm), lambda i: (i, 0, 0), memory_space=pltpu.SMEM),
            pl.BlockSpec((1, 1, TOP_K * tm), lambda i: (jnp.minimum(i + 1, n_tiles - 1), 0, 0),
                         memory_space=pltpu.SMEM),
            pl.BlockSpec(memory_space=pl.ANY),
            pl.BlockSpec((tm, LANES), lambda i: (i, 0)),
            pl.BlockSpec((tm, d), lambda i: (i, 0)),
            pl.BlockSpec((None, 1, d), lambda i: (i // tiles_per_batch, 0, gate_chunk)),
        ],
        out_specs=pl.BlockSpec((tm, d), lambda i: (i, 0)),
        out_shape=jax.ShapeDtypeStruct((n_tok, d), f32),
        scratch_shapes=[pltpu.VMEM((2 * TOP_K, tm, d), f32), pltpu.SemaphoreType.DMA((2,))],
        compiler_params=_cparams(1), name="moe_combine",
    )(dest_km, dest_km, y_sorted, gates, x2d, mod3)


def _routing_tables(idx_km):
    n_tok = idx_km.shape[1]
    n_assign = n_tok * TOP_K
    n_sb_max = (n_assign // MOE_SUB + N_EXPERTS + N_EXPERTS * (MOE_SB_SUBS - 1)) // MOE_SB_SUBS
    experts = jnp.arange(N_EXPERTS, dtype=i32)
    onehot = idx_km[:, :, None] == experts[None, None, :]
    per_tok = jnp.sum(onehot.astype(i32), axis=0)
    before = jnp.cumsum(per_tok, axis=0) - per_tok
    rank = jnp.sum(jnp.where(onehot, before[None], 0), axis=2)
    counts = before[-1] + per_tok[-1]
    n_sub_e = (counts + MOE_SUB - 1) // MOE_SUB
    n_sb_e = (n_sub_e + MOE_SB_SUBS - 1) // MOE_SB_SUBS
    sb_end = jnp.cumsum(n_sb_e)
    sb_start = sb_end - n_sb_e
    base = n_sub_e // jnp.maximum(n_sb_e, 1)
    rem = n_sub_e - base * n_sb_e
    tables = jnp.stack([sb_start, base, rem], axis=1).astype(bf16)
    looked = jnp.dot(onehot.astype(bf16).reshape(n_assign, N_EXPERTS), tables,
                     preferred_element_type=f32).astype(i32).reshape(TOP_K, n_tok, 3)
    a_start, a_base, a_rem = looked[..., 0], looked[..., 1], looked[..., 2]
    q = rank // MOE_SUB
    thr = a_rem * (a_base + 1)
    sb_local = jnp.where(q < thr, q // (a_base + 1), a_rem + (q - thr) // jnp.maximum(a_base, 1))
    sub_in = jnp.where(q < thr, q % (a_base + 1), (q - thr) % jnp.maximum(a_base, 1))
    dest = ((a_start + sb_local) * MOE_SB_ROWS + sub_in * MOE_SUB + rank % MOE_SUB).astype(i32)
    tok = jnp.broadcast_to(jnp.arange(n_tok, dtype=i32)[None, :], (TOP_K, n_tok))
    row_tok = jnp.zeros((n_sb_max * MOE_SB_ROWS,), i32).at[dest.reshape(-1)].set(
        tok.reshape(-1), unique_indices=True, mode="promise_in_bounds")
    s = jnp.arange(n_sb_max, dtype=i32)
    n_sb_total = sb_end[-1]
    sb_valid = s < n_sb_total
    sb_src = jnp.where(sb_valid, s, n_sb_total - 1).astype(i32)
    sb_e = jnp.minimum(jnp.searchsorted(sb_end, sb_src, side="right"), N_EXPERTS - 1).astype(i32)
    sb_local_s = sb_src - sb_start[sb_e]
    sb_nsub = jnp.where(sb_valid, base[sb_e] + (sb_local_s < rem[sb_e]).astype(i32), 0).astype(i32)
    return dest, row_tok, sb_e, sb_nsub


def kernel(x, c, rel_bias, norm_gain, ada_w, ada_b, attn_w_qkv, attn_b_qkv, attn_q_gain, attn_k_gain, attn_sinks, attn_w_o, attn_b_o, ssm_lam_re, ssm_lam_im, ssm_log_dt, ssm_b_re, ssm_b_im, ssm_c_re, ssm_c_im, ssm_d, ssm_w_glu_a, ssm_b_glu_a, ssm_w_glu_b, ssm_b_glu_b, moe_w_router, moe_b_router, moe_w_gate_up, moe_b_gate_up, moe_w_down, moe_b_down):
    bsz, seq, d = x.shape
    n_tok = bsz * seq
    depth = norm_gain.shape[0]
    tm, tn = DENSE_TM, DENSE_TN
    assert seq % tm == 0 and d % tn == 0
    SH1, SC1, G1, SH2, SC2, G2 = range(6)

    def gated(accs, extras):
        x_res, gate = extras
        return x_res + gate * accs[0]

    def gated_glu(accs, extras):
        x_res, gate = extras
        return x_res + gate * (accs[0] * jax.nn.sigmoid(accs[1]))

    for layer in range(depth):
        mod3 = _modulation(c, ada_w, ada_b, layer).reshape(bsz, 1, 6 * d)
        x2d = x.reshape(n_tok, d)
        res_extras = _gated_residual_extras(x2d, mod3, G1, d, tm, tn, seq)
        i = layer // 2
        if layer % 2 == 0:
            h = _norm(x, norm_gain[layer, 0], mod3, SC1, SH1, bf16)
            qkv = _dense(h.reshape(n_tok, d), [attn_w_qkv], [attn_b_qkv], i, lambda accs, extras: accs[0], [],
                         f32, tm, tn, "qkv_proj")
            o = _attention(qkv.reshape(bsz, seq, -1), attn_q_gain[i], attn_k_gain[i], attn_sinks[i], rel_bias)
            x2d = _dense(o.reshape(n_tok, -1), [attn_w_o], [attn_b_o], i, gated, res_extras, f32, tm, tn, "attn_out")
        else:
            h = _norm(x, norm_gain[layer, 0], mod3, SC1, SH1, f32)
            y = _ssm(h, ssm_lam_re[i], ssm_lam_im[i], ssm_log_dt[i], ssm_b_re[i], ssm_b_im[i],
                     ssm_c_re[i], ssm_c_im[i], ssm_d[i])
            x2d = _dense(y.reshape(n_tok, d), [ssm_w_glu_a, ssm_w_glu_b], [ssm_b_glu_a, ssm_b_glu_b], i,
                         gated_glu, res_extras, f32, tm, tn, "ssm_glu")
        x = x2d.reshape(bsz, seq, d)
        h, top_idx, gates = _norm(x, norm_gain[layer, 1], mod3, SC2, SH2, f32,
                                  router=(moe_w_router, moe_b_router, layer))
        idx_km = top_idx.reshape(n_tok, LANES)[:, :TOP_K].T
        dest, row_tok, sb_e, sb_nsub = _routing_tables(idx_km)
        y_sorted = _moe_experts(h, d // LANES, row_tok, sb_e, sb_nsub, moe_w_gate_up, moe_b_gate_up,
                                moe_w_down, moe_b_down, layer)
        x = _moe_combine(y_sorted, dest, gates.reshape(n_tok, LANES), x2d, mod3, G2, seq).reshape(bsz, seq, d)
    return x
```

```python
import functools
import math

import numpy as np
import jax
import jax.numpy as jnp
from jax import lax
from jax.experimental import pallas as pl
from jax.experimental.pallas import tpu as pltpu

f32 = jnp.float32
bf16 = jnp.bfloat16
i32 = jnp.int32

N_Q_HEADS = 32
N_KV_HEADS = 4
HEAD_DIM = 64
Q_PER_KV = N_Q_HEADS // N_KV_HEADS
WINDOW = 128
NUM_BUCKETS = 32
MAX_DISTANCE = 128
SSM_GROUP_CH = 16
N_EXPERTS = 32
TOP_K = 4
SWIGLU_ALPHA = 1.702
SWIGLU_LIMIT = 7.0
NORM_EPS = 1e-5

LANES = 128
SUBLANES = 8
VMEM_LIMIT = 56 * 1024 * 1024
MOE_VMEM_LIMIT = 60 * 1024 * 1024
N_DMA_PRIORITIES = 2

ADALN_TN = 1024
NORM_TS = 512
DENSE_TM = 1024
DENSE_TN = 512
COMBINE_TM = 128
ATTN_BLOCKS = 4
SSM_TILE_CH = 128
SSM_TIME_CHUNK = 512
MOE_SUB = 256
MOE_SB_SUBS = 8
MOE_SB_ROWS = MOE_SUB * MOE_SB_SUBS
MOE_GROUP = 4
MOE_CHUNK = 256
MOE_DOWN_CHUNK = 512


def _cparams(n_axes, vmem_limit=VMEM_LIMIT):
    return pltpu.CompilerParams(dimension_semantics=("arbitrary",) * n_axes, vmem_limit_bytes=vmem_limit)


def _mod_kernel(c_ref, w_ref, b_ref, o_ref):
    c = c_ref[...]
    cond = c * jax.nn.sigmoid(c)
    o_ref[...] = jnp.dot(cond.astype(bf16), w_ref[...].astype(bf16), preferred_element_type=f32) + b_ref[...]


def _modulation(c, ada_w, ada_b, layer):
    bsz, d = c.shape
    n = ada_w.shape[-1]
    tn = ADALN_TN
    assert n % tn == 0
    return pl.pallas_call(
        _mod_kernel,
        grid=(n // tn,),
        in_specs=[
            pl.BlockSpec((bsz, d), lambda j: (0, 0)),
            pl.BlockSpec((None, d, tn), lambda j: (layer, 0, j)),
            pl.BlockSpec((None, 1, tn), lambda j: (layer, 0, j)),
        ],
        out_specs=pl.BlockSpec((bsz, tn), lambda j: (0, j)),
        out_shape=jax.ShapeDtypeStruct((bsz, n), f32),
        compiler_params=_cparams(1),
        name="adaln_mod",
    )(c, ada_w, ada_b.reshape(ada_b.shape[0], 1, n))


def _norm_mod(x_ref, gain_ref, sc_ref, sh_ref):
    x = x_ref[0]
    ms = jnp.mean(x * x, axis=-1, keepdims=True)
    y = x * lax.rsqrt(ms + NORM_EPS) * gain_ref[...]
    return y * (1.0 + sc_ref[0]) + sh_ref[0]


def _norm_kernel(x_ref, gain_ref, sc_ref, sh_ref, h_ref):
    h_ref[0] = _norm_mod(x_ref, gain_ref, sc_ref, sh_ref).astype(h_ref.dtype)


def _split_bf16(v):
    hi = v.astype(bf16)
    lo = (v - hi.astype(f32)).astype(bf16)
    return hi, lo


def _norm_router_kernel(x_ref, gain_ref, sc_ref, sh_ref, wr_ref, br_ref, h_ref, idx_ref, gate_ref):
    h = _norm_mod(x_ref, gain_ref, sc_ref, sh_ref)
    ts, d = h.shape
    n_col = d // LANES
    for j in range(n_col):
        h_ref[pl.ds(j, ts, stride=n_col), :] = h[:, j * LANES:(j + 1) * LANES]
    h_hi, h_lo = _split_bf16(h)
    w_hi, w_lo = _split_bf16(wr_ref[...])
    dot = functools.partial(jnp.dot, preferred_element_type=f32)
    logits = dot(h_hi, w_hi) + (dot(h_hi, w_lo) + dot(h_lo, w_hi)) + br_ref[...]
    rows, n_exp = logits.shape
    col = lax.broadcasted_iota(i32, (rows, n_exp), 1)
    work = logits
    tops, idxs = [], []
    for _ in range(TOP_K):
        m = jnp.max(work, axis=-1, keepdims=True)
        idx = jnp.min(jnp.where(work == m, col, n_exp), axis=-1, keepdims=True)
        work = jnp.where(col == idx, -jnp.inf, work)
        tops.append(m)
        idxs.append(idx)
    es = [jnp.exp(t - tops[0]) for t in tops]
    denom = functools.reduce(lambda a, b: a + b, es)
    lane = lax.broadcasted_iota(i32, (rows, LANES), 1)
    idx_out = jnp.zeros((rows, LANES), i32)
    gate_out = jnp.zeros((rows, LANES), f32)
    for k in range(TOP_K):
        idx_out = jnp.where(lane == k, idxs[k], idx_out)
        gate_out = jnp.where(lane == k, es[k] / denom, gate_out)
    idx_ref[0] = idx_out
    gate_ref[0] = gate_out


def _norm(x, gain, mod3, sc_chunk, sh_chunk, out_dtype, router=None):
    bsz, seq, d = x.shape
    ts = NORM_TS
    assert seq % ts == 0 and d % LANES == 0
    in_specs = [
        pl.BlockSpec((1, ts, d), lambda b, s: (b, s, 0)),
        pl.BlockSpec((1, d), lambda b, s: (0, 0)),
        pl.BlockSpec((1, 1, d), lambda b, s: (b, 0, sc_chunk)),
        pl.BlockSpec((1, 1, d), lambda b, s: (b, 0, sh_chunk)),
    ]
    h_spec = pl.BlockSpec((1, ts, d), lambda b, s: (b, s, 0))
    args = [x, gain.reshape(1, d), mod3, mod3]
    if router is None:
        return pl.pallas_call(
            _norm_kernel, grid=(bsz, seq // ts), in_specs=in_specs, out_specs=h_spec,
            out_shape=jax.ShapeDtypeStruct((bsz, seq, d), out_dtype),
            compiler_params=_cparams(2), name="norm_mod",
        )(*args)
    w_router, b_router, layer = router
    n_exp = w_router.shape[-1]
    in_specs += [
        pl.BlockSpec((None, d, n_exp), lambda b, s: (layer, 0, 0)),
        pl.BlockSpec((None, 1, n_exp), lambda b, s: (layer, 0, 0)),
    ]
    lane_spec = pl.BlockSpec((1, ts, LANES), lambda b, s: (b, s, 0))
    n_col = d // LANES
    h3_spec = pl.BlockSpec((ts * n_col, LANES), lambda b, s: (b * (seq // ts) + s, 0))
    return pl.pallas_call(
        _norm_router_kernel, grid=(bsz, seq // ts), in_specs=in_specs,
        out_specs=[h3_spec, lane_spec, lane_spec],
        out_shape=[jax.ShapeDtypeStruct((bsz * seq * n_col, LANES), f32),
                   jax.ShapeDtypeStruct((bsz, seq, LANES), i32),
                   jax.ShapeDtypeStruct((bsz, seq, LANES), f32)],
        compiler_params=_cparams(2), name="norm_router",
    )(*args, w_router, b_router.reshape(b_router.shape[0], 1, n_exp))


def _dense_kernel(*refs, n_w, n_extra, epilogue):
    x_ref = refs[0]
    w_refs = refs[1:1 + n_w]
    b_refs = refs[1 + n_w:1 + 2 * n_w]
    e_refs = refs[1 + 2 * n_w:1 + 2 * n_w + n_extra]
    o_ref = refs[1 + 2 * n_w + n_extra]
    wbf_refs = refs[2 + 2 * n_w + n_extra:]

    @pl.when(pl.program_id(1) == 0)
    def _():
        for w_ref, wbf in zip(w_refs, wbf_refs):
            wbf[...] = w_ref[...].astype(bf16)

    x = x_ref[...]
    accs = [jnp.dot(x, wbf[...], preferred_element_type=f32) + b_ref[...] for wbf, b_ref in zip(wbf_refs, b_refs)]
    o_ref[...] = epilogue(accs, [e[...] for e in e_refs]).astype(o_ref.dtype)


def _dense(x, ws, bs, layer, epilogue, extras, out_dtype, tm, tn, name):
    m, k = x.shape
    n = ws[0].shape[-1]
    assert m % tm == 0 and n % tn == 0
    in_specs = [pl.BlockSpec((tm, k), lambda j, i: (i, 0))]
    in_specs += [pl.BlockSpec((None, k, tn), lambda j, i: (layer, 0, j)) for _ in ws]
    in_specs += [pl.BlockSpec((None, 1, tn), lambda j, i: (layer, 0, j)) for _ in bs]
    in_specs += [spec for _, spec in extras]
    args = [x] + list(ws) + [b.reshape(b.shape[0], 1, n) for b in bs] + [a for a, _ in extras]
    return pl.pallas_call(
        functools.partial(_dense_kernel, n_w=len(ws), n_extra=len(extras), epilogue=epilogue),
        grid=(n // tn, m // tm),
        in_specs=in_specs,
        out_specs=pl.BlockSpec((tm, tn), lambda j, i: (i, j)),
        out_shape=jax.ShapeDtypeStruct((m, n), out_dtype),
        scratch_shapes=[pltpu.VMEM((k, tn), bf16) for _ in ws],
        compiler_params=_cparams(2), name=name,
    )(*args)


def _gated_residual_extras(x2d, mod3, gate_chunk, d, tm, tn, seq):
    tiles_per_batch = seq // tm
    return [
        (x2d, pl.BlockSpec((tm, tn), lambda j, i: (i, j))),
        (mod3, pl.BlockSpec((None, 1, tn), lambda j, i: (i // tiles_per_batch, 0, gate_chunk * (d // tn) + j))),
    ]


def _attn_kernel(sinks_ref, q_ref, kp_ref, kc_ref, vp_ref, vc_ref, qg_ref, kg_ref, bias_ref, o_ref):
    n = pl.program_id(1)
    lane = lax.broadcasted_iota(i32, (1, LANES), 1)
    lo = lane < HEAD_DIM
    bq = kp_ref.shape[1]
    n_blk = q_ref.shape[1] // bq

    def halfnorm(v, gain):
        sq = v * v
        s_lo = jnp.sum(jnp.where(lo, sq, 0.0), axis=-1, keepdims=True)
        s_hi = jnp.sum(jnp.where(lo, 0.0, sq), axis=-1, keepdims=True)
        ms = jnp.where(lo, s_lo, s_hi) * (1.0 / HEAD_DIM)
        return v * lax.rsqrt(ms + NORM_EPS) * gain

    k_all = jnp.concatenate([kp_ref[0], kc_ref[0]], axis=0)
    v_all = jnp.concatenate([vp_ref[0], vc_ref[0]], axis=0)
    kcol = lax.broadcasted_iota(i32, (1, 2 * bq), 1)
    qgain = qg_ref[...]
    kgain = kg_ref[...]
    pairs = Q_PER_KV // 2
    for c in range(N_KV_HEADS // 2):
        kn_all = halfnorm(k_all[:, c * LANES:(c + 1) * LANES], kgain)
        kr_all = pltpu.roll(kn_all, HEAD_DIM, 1)
        vn_all = v_all[:, c * LANES:(c + 1) * LANES]
        vr_all = pltpu.roll(vn_all, HEAD_DIM, 1)
        for blk in range(n_blk):
            rows = slice(blk * bq, (blk + 1) * bq)
            keys = slice(blk * bq, (blk + 2) * bq)
            kn, kr, vn, vr = kn_all[keys], kr_all[keys], vn_all[keys], vr_all[keys]
            key_ok = jnp.logical_or(kcol >= bq, n > 0) if blk == 0 else None
            for half in range(2):
                g = 2 * c + half
                k_src, k_rot = (kn, kr) if half == 0 else (kr, kn)
                v_src, v_rot = (vn, vr) if half == 0 else (vr, vn)
                k_par = [jnp.where(lo, k_src, 0.0).astype(bf16), jnp.where(lo, 0.0, k_rot).astype(bf16)]
                v_par = [jnp.where(lo, v_src, 0.0).astype(bf16), jnp.where(lo, 0.0, v_rot).astype(bf16)]
                qs = [halfnorm(q_ref[0, rows, (g * pairs + p) * LANES:(g * pairs + p + 1) * LANES], qgain)
                      for p in range(pairs)]
                qg = (jnp.concatenate(qs, axis=0) * (1.0 / math.sqrt(HEAD_DIM))).astype(bf16)
                acc = None
                for par in range(2):
                    s = lax.dot_general(qg, k_par[par], (((1,), (1,)), ((), ())), preferred_element_type=f32)
                    s = s + bias_ref[g, par]
                    if key_ok is not None:
                        s = jnp.where(key_ok, s, -jnp.inf)
                    sink = jnp.concatenate(
                        [jnp.full((bq, 1), sinks_ref[g * Q_PER_KV + 2 * p + par], f32) for p in range(pairs)], axis=0)
                    m = jnp.maximum(jnp.max(s, axis=-1, keepdims=True), sink)
                    e = jnp.exp(s - m)
                    den = jnp.sum(e, axis=-1, keepdims=True) + jnp.exp(sink - m)
                    probs = (e * (1.0 / den)).astype(bf16)
                    o = jnp.dot(probs, v_par[par], preferred_element_type=f32)
                    acc = o if acc is None else acc + o
                for p in range(pairs):
                    col = (g * pairs + p) * LANES
                    o_ref[0, rows, col:col + LANES] = acc[p * bq:(p + 1) * bq].astype(o_ref.dtype)


def _t5_bucket(dist):
    nn = np.maximum(dist, 0)
    max_exact = NUM_BUCKETS // 2
    large = max_exact + (np.log(np.maximum(nn, 1) / max_exact) / np.log(MAX_DISTANCE / max_exact)
                         * (NUM_BUCKETS - max_exact)).astype(np.int32)
    large = np.minimum(large, NUM_BUCKETS - 1)
    return np.where(nn < max_exact, nn, large).astype(np.int32)


def _attention(qkv, q_gain, k_gain, sinks, rel_bias):
    bsz, seq, _ = qkv.shape
    bq = WINDOW
    q_dim = N_Q_HEADS * HEAD_DIM
    kv_dim = N_KV_HEADS * HEAD_DIM
    pairs = Q_PER_KV // 2
    ql = np.arange(bq)[:, None]
    kl = np.arange(2 * bq)[None, :]
    dist = ql + bq - kl
    in_window = (dist >= 0) & (dist < WINDOW)
    bucket = jnp.asarray(_t5_bucket(dist).reshape(1, -1))
    onehot_t = (bucket == jnp.arange(NUM_BUCKETS, dtype=i32)[:, None]).astype(f32)
    bias = jnp.dot(rel_bias.astype(f32).T, onehot_t, precision=lax.Precision.HIGHEST).reshape(N_Q_HEADS, bq, 2 * bq)
    bias = jnp.where(jnp.asarray(in_window)[None], bias, -jnp.inf)
    bias = bias.reshape(N_KV_HEADS, pairs, 2, bq, 2 * bq)
    bias = jnp.transpose(bias, (0, 2, 1, 3, 4)).reshape(N_KV_HEADS, 2, pairs * bq, 2 * bq)
    gain2 = lambda gn: jnp.concatenate([gn, gn]).reshape(1, LANES).astype(f32)
    k_blk = q_dim // kv_dim
    nb = ATTN_BLOCKS
    assert seq % (nb * bq) == 0
    grid_spec = pltpu.PrefetchScalarGridSpec(
        num_scalar_prefetch=1,
        grid=(bsz, seq // (nb * bq)),
        in_specs=[
            pl.BlockSpec((1, nb * bq, q_dim), lambda b, n, s: (b, n, 0)),
            pl.BlockSpec((1, bq, kv_dim), lambda b, n, s: (b, jnp.maximum(nb * n - 1, 0), k_blk)),
            pl.BlockSpec((1, nb * bq, kv_dim), lambda b, n, s: (b, n, k_blk)),
            pl.BlockSpec((1, bq, kv_dim), lambda b, n, s: (b, jnp.maximum(nb * n - 1, 0), k_blk + 1)),
            pl.BlockSpec((1, nb * bq, kv_dim), lambda b, n, s: (b, n, k_blk + 1)),
            pl.BlockSpec((1, LANES), lambda b, n, s: (0, 0)),
            pl.BlockSpec((1, LANES), lambda b, n, s: (0, 0)),
            pl.BlockSpec((N_KV_HEADS, 2, pairs * bq, 2 * bq), lambda b, n, s: (0, 0, 0, 0)),
        ],
        out_specs=pl.BlockSpec((1, nb * bq, q_dim), lambda b, n, s: (b, n, 0)),
    )
    return pl.pallas_call(
        _attn_kernel, grid_spec=grid_spec,
        out_shape=jax.ShapeDtypeStruct((bsz, seq, q_dim), bf16),
        compiler_params=_cparams(2), name="swa_attention",
    )(sinks.astype(f32), qkv, qkv, qkv, qkv, qkv, gain2(q_gain), gain2(k_gain), bias)


def _gelu_tanh(y):
    return 0.5 * y * (1.0 + jnp.tanh(math.sqrt(2.0 / math.pi) * (y + 0.044715 * (y * y * y))))


def _ssm_kernel(u_ref, bm_ref, cm_ref, are_ref, aim_ref, d_ref, y_ref, us, xs, st, ys):
    bsz, ts, _ = u_ref.shape
    n_state = are_ref.shape[-1]

    @pl.when(pl.program_id(1) == 0)
    def _():
        st[...] = jnp.zeros_like(st)

    n_slab = us.shape[0]
    for h in range(n_slab):
        for b in range(bsz):
            us[h, pl.ds(b, ts, stride=bsz), :] = u_ref[b, :, h * LANES:(h + 1) * LANES]
    u = jnp.concatenate([us[h] for h in range(n_slab)], axis=1)
    xs[...] = jnp.dot(u.astype(bf16), bm_ref[0], preferred_element_type=f32)
    a_re = jnp.broadcast_to(are_ref[0], (bsz, n_state))
    a_im = jnp.broadcast_to(aim_ref[0], (bsz, n_state))

    def step(t, carry):
        s_re, s_im = carry
        r0 = pl.multiple_of(t * bsz, bsz)
        x_re = xs[pl.ds(r0, bsz), 0:n_state]
        x_im = xs[pl.ds(r0, bsz), n_state:2 * n_state]
        n_re = a_re * s_re - a_im * s_im + x_re
        n_im = a_re * s_im + a_im * s_re + x_im
        xs[pl.ds(r0, bsz), 0:n_state] = n_re
        xs[pl.ds(r0, bsz), n_state:2 * n_state] = n_im
        return n_re, n_im

    s_re, s_im = lax.fori_loop(0, ts, step, (st[0], st[1]), unroll=8)
    st[0] = s_re
    st[1] = s_im
    y = (jnp.dot(xs[:, 0:n_state].astype(bf16), cm_ref[0, 0:n_state, :], preferred_element_type=f32)
         + jnp.dot(xs[:, n_state:2 * n_state].astype(bf16), cm_ref[0, n_state:2 * n_state, :],
                   preferred_element_type=f32)
         + d_ref[0] * u)
    y = _gelu_tanh(y)
    for h in range(n_slab):
        ys[h] = y[:, h * LANES:(h + 1) * LANES]
        for b in range(bsz):
            y_ref[b, :, h * LANES:(h + 1) * LANES] = ys[h, pl.ds(b, ts, stride=bsz), :].astype(y_ref.dtype)


def _ssm(h, lam_re, lam_im, log_dt, b_re, b_im, c_re, c_im, d_skip):
    bsz, seq, width = h.shape
    assert bsz == SUBLANES
    n_grp, n_st = lam_re.shape
    tile_ch = SSM_TILE_CH
    gpt = tile_ch // SSM_GROUP_CH
    n_tiles = width // tile_ch
    ns = gpt * n_st
    dt = jnp.exp(log_dt.astype(f32))[:, None]
    lr, li = lam_re.astype(f32), lam_im.astype(f32)
    mag = jnp.exp(lr * dt)
    ab_re, ab_im = mag * jnp.cos(li * dt), mag * jnp.sin(li * dt)
    den = lr * lr + li * li
    nr, ni = ab_re - 1.0, ab_im
    f_re = (nr * lr + ni * li) / den
    f_im = (ni * lr - nr * li) / den
    br, bi = b_re.astype(f32), b_im.astype(f32)
    bb_re = f_re[..., None] * br - f_im[..., None] * bi
    bb_im = f_re[..., None] * bi + f_im[..., None] * br
    eye = jnp.eye(gpt, dtype=f32)

    def blockdiag_in(bb):
        t = bb.reshape(n_tiles, gpt, n_st, SSM_GROUP_CH)
        return jnp.einsum("tgpc,gh->tgchp", t, eye).reshape(n_tiles, tile_ch, ns)

    def blockdiag_out(cc):
        t = cc.reshape(n_tiles, gpt, SSM_GROUP_CH, n_st)
        return jnp.einsum("tgcp,gh->tgphc", t, eye).reshape(n_tiles, ns, tile_ch)

    bm = jnp.concatenate([blockdiag_in(bb_re), blockdiag_in(bb_im)], axis=-1).astype(bf16)
    cm = jnp.concatenate([blockdiag_out(c_re.astype(f32)), -blockdiag_out(c_im.astype(f32))], axis=1).astype(bf16)
    a_re = ab_re.reshape(n_tiles, 1, ns)
    a_im = ab_im.reshape(n_tiles, 1, ns)
    dsk = d_skip.astype(f32).reshape(n_tiles, 1, tile_ch)
    ts = SSM_TIME_CHUNK
    return pl.pallas_call(
        _ssm_kernel,
        grid=(n_tiles, seq // ts),
        in_specs=[
            pl.BlockSpec((bsz, ts, tile_ch), lambda g, t: (0, t, g)),
            pl.BlockSpec((1, tile_ch, 2 * ns), lambda g, t: (g, 0, 0)),
            pl.BlockSpec((1, 2 * ns, tile_ch), lambda g, t: (g, 0, 0)),
            pl.BlockSpec((1, 1, ns), lambda g, t: (g, 0, 0)),
            pl.BlockSpec((1, 1, ns), lambda g, t: (g, 0, 0)),
            pl.BlockSpec((1, 1, tile_ch), lambda g, t: (g, 0, 0)),
        ],
        out_specs=pl.BlockSpec((bsz, ts, tile_ch), lambda g, t: (0, t, g)),
        out_shape=jax.ShapeDtypeStruct((bsz, seq, width), bf16),
        scratch_shapes=[
            pltpu.VMEM((tile_ch // LANES, ts * bsz, LANES), f32),
            pltpu.VMEM((ts * bsz, 2 * ns), f32),
            pltpu.VMEM((2, bsz, ns), f32),
            pltpu.VMEM((tile_ch // LANES, ts * bsz, LANES), f32),
        ],
        compiler_params=_cparams(2), name="s5_ssm",
    )(h, bm, cm, a_re, a_im, dsk)


GATHER_PITCH = 24


def _swiglu(g, l):
    glu = jnp.minimum(g, SWIGLU_LIMIT)
    lin = jnp.clip(l, -SWIGLU_LIMIT, SWIGLU_LIMIT)
    return glu * jax.nn.sigmoid(SWIGLU_ALPHA * glu) * (lin + 1.0)


def _for_row_groups(nsub, cast, fn):
    n_big = nsub // MOE_GROUP
    big_rows = MOE_GROUP * MOE_SUB

    @pl.when(n_big > 0)
    def _():
        fn(0, big_rows, cast())

    @pl.when(n_big == 0)
    def _():
        fn(0, MOE_SUB, cast())

    def big(i, carry):
        fn(pl.multiple_of(i * big_rows, big_rows), big_rows, None)
        return carry

    def small(i, carry):
        fn(pl.multiple_of(i * MOE_SUB, MOE_SUB), MOE_SUB, None)
        return carry

    lax.fori_loop(1, n_big, big, 0)
    lax.fori_loop(jnp.maximum(n_big * MOE_GROUP, 1), nsub, small, 0)


def _moe_kernel(sbe_ref, sbn_ref, idx0_ref, idxn_ref, h_hbm, wg_ref, wl_ref, bgu_ref, wd_ref, bd_ref, o_ref,
                xbf, stg, sem, a_scr, wg_bf, wl_bf, wd_bf, *, n_chunks):
    s = pl.program_id(0)
    c = pl.program_id(1)
    n_sb = pl.num_programs(0)
    nsub = sbn_ref[s]
    slot = lax.rem(s, 2)
    nsub_next = jnp.where(s + 1 < n_sb, sbn_ref[jnp.minimum(s + 1, n_sb - 1)], 0)
    n_col = xbf.shape[2] // LANES
    stg_rows = MOE_SUB * GATHER_PITCH

    def issue(idx_ref, off, st_slot):
        def body(r2, carry):
            for prio in range(N_DMA_PRIORITIES):
                r = r2 * N_DMA_PRIORITIES + prio
                src = pl.multiple_of(idx_ref[0, 0, off + r] * n_col, SUBLANES)
                dst = pl.multiple_of(st_slot * stg_rows + r * GATHER_PITCH, SUBLANES)
                pltpu.make_async_copy(h_hbm.at[pl.ds(src, n_col)], stg.at[pl.ds(dst, n_col)],
                                      sem.at[st_slot]).start(priority=prio)
            return carry

        lax.fori_loop(0, MOE_SUB // N_DMA_PRIORITIES, body, 0, unroll=4)

    def finish(st_slot, x_slot, sub):
        base = pl.multiple_of(st_slot * stg_rows, SUBLANES)
        pltpu.make_async_copy(h_hbm.at[pl.ds(0, MOE_SUB * n_col)], stg.at[pl.ds(base, MOE_SUB * n_col)],
                              sem.at[st_slot]).wait()
        r0 = pl.multiple_of(sub * MOE_SUB, MOE_SUB)
        for j in range(n_col):
            piece = stg[pl.ds(base + j, MOE_SUB, stride=GATHER_PITCH), :]
            xbf[x_slot, pl.ds(r0, MOE_SUB), j * LANES:(j + 1) * LANES] = piece.astype(bf16)

    @pl.when(jnp.logical_and(s == 0, c == 0))
    def _():
        def first(j, carry):
            issue(idx0_ref, j * MOE_SUB, 0)
            finish(0, 0, j)
            return carry

        lax.fori_loop(0, nsub, first, 0)

    @pl.when(jnp.logical_and(c >= 1, c - 1 < nsub_next))
    def _():
        finish(lax.rem(c - 1, 2), 1 - slot, c - 1)

    @pl.when(c < nsub_next)
    def _():
        issue(idxn_ref, c * MOE_SUB, lax.rem(c, 2))

    @pl.when(jnp.logical_and(c < n_chunks, nsub > 0))
    def _():
        def cast():
            wg, wl = wg_ref[...].astype(bf16), wl_ref[...].astype(bf16)
            wg_bf[...] = wg
            wl_bf[...] = wl
            return wg, wl

        def up(r, rows, weights):
            wg, wl = weights if weights is not None else (wg_bf[...], wl_bf[...])
            xs = xbf[slot, pl.ds(r, rows), :]
            g = jnp.dot(xs, wg, preferred_element_type=f32) + bgu_ref[pl.ds(c, 1), :]
            l = jnp.dot(xs, wl, preferred_element_type=f32) + bgu_ref[pl.ds(n_chunks + c, 1), :]
            a_scr[c, pl.ds(r, rows), :] = _swiglu(g, l).astype(bf16)

        _for_row_groups(nsub, cast, up)

    @pl.when(jnp.logical_and(c >= n_chunks, nsub > 0))
    def _():
        def cast():
            wd = wd_ref[...].astype(bf16)
            wd_bf[...] = wd
            return wd

        def down(r, rows, weights):
            wd = weights if weights is not None else wd_bf[...]
            a = jnp.concatenate([a_scr[cc, pl.ds(r, rows), :] for cc in range(n_chunks)], axis=1)
            o_ref[pl.ds(r, rows), :] = jnp.dot(a, wd, preferred_element_type=f32) + bd_ref[pl.ds(c - n_chunks, 1), :]

        _for_row_groups(nsub, cast, down)

    @pl.when(c >= n_chunks)
    def _():
        def zero(i, carry):
            r = pl.multiple_of(i * MOE_SUB, MOE_SUB)
            o_ref[pl.ds(r, MOE_SUB), :] = jnp.zeros((MOE_SUB, o_ref.shape[1]), f32)
            return carry

        lax.fori_loop(nsub, MOE_SB_SUBS, zero, 0)


def _moe_experts(h_rows, n_col, row_tok, sb_expert, sb_nsub, w_gate_up, b_gate_up, w_down, b_down, layer):
    n_rows = row_tok.shape[0]
    d = n_col * LANES
    n_sb = n_rows // MOE_SB_ROWS
    d_exp = w_down.shape[2]
    n_chunks = d_exp // MOE_CHUNK
    n_out_chunks = d // MOE_DOWN_CHUNK
    last = n_chunks - 1
    last_out = n_out_chunks - 1

    def valid(s, n):
        return n[s] > 0

    def wg_map(s, c, e, n):
        return (layer, e[s], 0, jnp.where(valid(s, n), jnp.minimum(c, last), last))

    def wl_map(s, c, e, n):
        return (layer, e[s], 0, n_chunks + jnp.where(valid(s, n), jnp.minimum(c, last), last))

    def wd_map(s, c, e, n):
        return (layer, e[s], 0, jnp.where(valid(s, n), jnp.maximum(c - n_chunks, 0), last_out))

    def out_map(s, c, e, n):
        return (s, jnp.maximum(c - n_chunks, 0))

    def next_idx_map(s, c, e, n):
        return (jnp.minimum(s + 1, n_sb - 1), 0, 0)

    def bias_map(s, c, e, n):
        return (layer, e[s], 0, 0)

    grid_spec = pltpu.PrefetchScalarGridSpec(
        num_scalar_prefetch=2,
        grid=(n_sb, n_chunks + n_out_chunks),
        in_specs=[
            pl.BlockSpec((1, 1, MOE_SB_ROWS), lambda s, c, e, n: (0, 0, 0), memory_space=pltpu.SMEM),
            pl.BlockSpec((1, 1, MOE_SB_ROWS), next_idx_map, memory_space=pltpu.SMEM),
            pl.BlockSpec(memory_space=pl.ANY),
            pl.BlockSpec((None, None, d, MOE_CHUNK), wg_map),
            pl.BlockSpec((None, None, d, MOE_CHUNK), wl_map),
            pl.BlockSpec((None, None, 2 * n_chunks, MOE_CHUNK), bias_map),
            pl.BlockSpec((None, None, d_exp, MOE_DOWN_CHUNK), wd_map),
            pl.BlockSpec((None, None, n_out_chunks, MOE_DOWN_CHUNK), bias_map),
        ],
        out_specs=pl.BlockSpec((MOE_SB_ROWS, MOE_DOWN_CHUNK), out_map),
        scratch_shapes=[
            pltpu.VMEM((2, MOE_SB_ROWS, d), bf16),
            pltpu.VMEM((2 * MOE_SUB * GATHER_PITCH, LANES), f32),
            pltpu.SemaphoreType.DMA((2,)),
            pltpu.VMEM((n_chunks, MOE_SB_ROWS, MOE_CHUNK), bf16),
            pltpu.VMEM((d, MOE_CHUNK), bf16),
            pltpu.VMEM((d, MOE_CHUNK), bf16),
            pltpu.VMEM((d_exp, MOE_DOWN_CHUNK), bf16),
        ],
    )
    n_l, n_e = b_gate_up.shape[:2]
    return pl.pallas_call(
        functools.partial(_moe_kernel, n_chunks=n_chunks), grid_spec=grid_spec,
        out_shape=jax.ShapeDtypeStruct((n_rows, d), f32),
        compiler_params=_cparams(2, MOE_VMEM_LIMIT), name="moe_experts",
    )(sb_expert, sb_nsub, row_tok.reshape(n_sb, 1, MOE_SB_ROWS), row_tok.reshape(n_sb, 1, MOE_SB_ROWS),
      h_rows, w_gate_up, w_gate_up,
      b_gate_up.reshape(n_l, n_e, 2 * n_chunks, MOE_CHUNK),
      w_down, b_down.reshape(n_l, n_e, n_out_chunks, MOE_DOWN_CHUNK))


def _combine_kernel(dest_ref, dest_next_ref, y_hbm, gate_ref, x_ref, g2_ref, o_ref, ybuf, sem):
    i = pl.program_id(0)
    n = pl.num_programs(0)
    tm = x_ref.shape[0]
    slot = lax.rem(i, 2)

    def issue(dest_blk, sl):
        for k in range(TOP_K):
            def body(t2, carry, k=k):
                for prio in range(N_DMA_PRIORITIES):
                    t = t2 * N_DMA_PRIORITIES + prio
                    pltpu.make_async_copy(y_hbm.at[pl.ds(dest_blk[0, 0, k * tm + t], 1)],
                                          ybuf.at[sl * TOP_K + k, pl.ds(t, 1)], sem.at[sl]).start(priority=prio)
                return carry

            lax.fori_loop(0, tm // N_DMA_PRIORITIES, body, 0, unroll=4)

    @pl.when(i == 0)
    def _():
        issue(dest_ref, 0)

    @pl.when(i + 1 < n)
    def _():
        issue(dest_next_ref, 1 - slot)

    for k in range(TOP_K):
        pltpu.make_async_copy(y_hbm.at[pl.ds(0, tm)], ybuf.at[slot * TOP_K + k], sem.at[slot]).wait()
    gates = gate_ref[...]
    acc = gates[:, 0:1] * ybuf[slot * TOP_K]
    for k in range(1, TOP_K):
        acc = acc + gates[:, k:k + 1] * ybuf[slot * TOP_K + k]
    o_ref[...] = x_ref[...] + g2_ref[...] * acc


def _moe_combine(y_sorted, dest, gates, x2d, mod3, gate_chunk, seq):
    n_tok, d = x2d.shape
    tm = COMBINE_TM
    assert seq % tm == 0
    n_tiles = n_tok // tm
    dest_km = jnp.transpose(dest.reshape(TOP_K, n_tiles, tm), (1, 0, 2)).reshape(n_tiles, 1, TOP_K * tm)
    tiles_per_batch = seq // tm
    return pl.pallas_call(
        _combine_kernel,
        grid=(n_tiles,),
        in_specs=[
            pl.BlockSpec((1, 1, TOP_K * tm), lambda i: (i, 0, 0), memory_space=pltpu.SMEM),
            pl.BlockSpec((1, 1, TOP_K * tm), lambda i: (jnp.minimum(i + 1, n_tiles - 1), 0, 0),
                         memory_space=pltpu.SMEM),
            pl.BlockSpec(memory_space=pl.ANY),
            pl.BlockSpec((tm, LANES), lambda i: (i, 0)),
            pl.BlockSpec((tm, d), lambda i: (i, 0)),
            pl.BlockSpec((None, 1, d), lambda i: (i // tiles_per_batch, 0, gate_chunk)),
        ],
        out_specs=pl.BlockSpec((tm, d), lambda i: (i, 0)),
        out_shape=jax.ShapeDtypeStruct((n_tok, d), f32),
        scratch_shapes=[pltpu.VMEM((2 * TOP_K, tm, d), f32), pltpu.SemaphoreType.DMA((2,))],
        compiler_params=_cparams(1), name="moe_combine",
    )(dest_km, dest_km, y_sorted, gates, x2d, mod3)


def _routing_tables(idx_km):
    n_tok = idx_km.shape[1]
    n_assign = n_tok * TOP_K
    n_sb_max = (n_assign // MOE_SUB + N_EXPERTS + N_EXPERTS * (MOE_SB_SUBS - 1)) // MOE_SB_SUBS
    experts = jnp.arange(N_EXPERTS, dtype=i32)
    onehot = idx_km[:, :, None] == experts[None, None, :]
    per_tok = jnp.sum(onehot.astype(i32), axis=0)
    before = jnp.cumsum(per_tok, axis=0) - per_tok
    rank = jnp.sum(jnp.where(onehot, before[None], 0), axis=2)
    counts = before[-1] + per_tok[-1]
    n_sub_e = (counts + MOE_SUB - 1) // MOE_SUB
    n_sb_e = (n_sub_e + MOE_SB_SUBS - 1) // MOE_SB_SUBS
    sb_end = jnp.cumsum(n_sb_e)
    sb_start = sb_end - n_sb_e
    base = n_sub_e // jnp.maximum(n_sb_e, 1)
    rem = n_sub_e - base * n_sb_e
    tables = jnp.stack([sb_start, base, rem], axis=1).astype(bf16)
    looked = jnp.dot(onehot.astype(bf16).reshape(n_assign, N_EXPERTS), tables,
                     preferred_element_type=f32).astype(i32).reshape(TOP_K, n_tok, 3)
    a_start, a_base, a_rem = looked[..., 0], looked[..., 1], looked[..., 2]
    q = rank // MOE_SUB
    thr = a_rem * (a_base + 1)
    sb_local = jnp.where(q < thr, q // (a_base + 1), a_rem + (q - thr) // jnp.maximum(a_base, 1))
    sub_in = jnp.where(q < thr, q % (a_base + 1), (q - thr) % jnp.maximum(a_base, 1))
    dest = ((a_start + sb_local) * MOE_SB_ROWS + sub_in * MOE_SUB + rank % MOE_SUB).astype(i32)
    tok = jnp.broadcast_to(jnp.arange(n_tok, dtype=i32)[None, :], (TOP_K, n_tok))
    row_tok = jnp.zeros((n_sb_max * MOE_SB_ROWS,), i32).at[dest.reshape(-1)].set(
        tok.reshape(-1), unique_indices=True, mode="promise_in_bounds")
    s = jnp.arange(n_sb_max, dtype=i32)
    n_sb_total = sb_end[-1]
    sb_valid = s < n_sb_total
    sb_src = jnp.where(sb_valid, s, n_sb_total - 1).astype(i32)
    sb_e = jnp.minimum(jnp.searchsorted(sb_end, sb_src, side="right"), N_EXPERTS - 1).astype(i32)
    sb_local_s = sb_src - sb_start[sb_e]
    sb_nsub = jnp.where(sb_valid, base[sb_e] + (sb_local_s < rem[sb_e]).astype(i32), 0).astype(i32)
    return dest, row_tok, sb_e, sb_nsub


def kernel(x, c, rel_bias, norm_gain, ada_w, ada_b, attn_w_qkv, attn_b_qkv, attn_q_gain, attn_k_gain, attn_sinks, attn_w_o, attn_b_o, ssm_lam_re, ssm_lam_im, ssm_log_dt, ssm_b_re, ssm_b_im, ssm_c_re, ssm_c_im, ssm_d, ssm_w_glu_a, ssm_b_glu_a, ssm_w_glu_b, ssm_b_glu_b, moe_w_router, moe_b_router, moe_w_gate_up, moe_b_gate_up, moe_w_down, moe_b_down):
    bsz, seq, d = x.shape
    n_tok = bsz * seq
    depth = norm_gain.shape[0]
    tm, tn = DENSE_TM, DENSE_TN
    assert seq % tm == 0 and d % tn == 0
    SH1, SC1, G1, SH2, SC2, G2 = range(6)

    def gated(accs, extras):
        x_res, gate = extras
        return x_res + gate * accs[0]

    def gated_glu(accs, extras):
        x_res, gate = extras
        return x_res + gate * (accs[0] * jax.nn.sigmoid(accs[1]))

    for layer in range(depth):
        mod3 = _modulation(c, ada_w, ada_b, layer).reshape(bsz, 1, 6 * d)
        x2d = x.reshape(n_tok, d)
        res_extras = _gated_residual_extras(x2d, mod3, G1, d, tm, tn, seq)
        i = layer // 2
        if layer % 2 == 0:
            h = _norm(x, norm_gain[layer, 0], mod3, SC1, SH1, bf16)
            qkv = _dense(h.reshape(n_tok, d), [attn_w_qkv], [attn_b_qkv], i, lambda accs, extras: accs[0], [],
                         f32, tm, tn, "qkv_proj")
            o = _attention(qkv.reshape(bsz, seq, -1), attn_q_gain[i], attn_k_gain[i], attn_sinks[i], rel_bias)
            x2d = _dense(o.reshape(n_tok, -1), [attn_w_o], [attn_b_o], i, gated, res_extras, f32, tm, tn, "attn_out")
        else:
            h = _norm(x, norm_gain[layer, 0], mod3, SC1, SH1, f32)
            y = _ssm(h, ssm_lam_re[i], ssm_lam_im[i], ssm_log_dt[i], ssm_b_re[i], ssm_b_im[i],
                     ssm_c_re[i], ssm_c_im[i], ssm_d[i])
            x2d = _dense(y.reshape(n_tok, d), [ssm_w_glu_a, ssm_w_glu_b], [ssm_b_glu_a, ssm_b_glu_b], i,
                         gated_glu, res_extras, f32, tm, tn, "ssm_glu")
        x = x2d.reshape(bsz, seq, d)
        h, top_idx, gates = _norm(x, norm_gain[layer, 1], mod3, SC2, SH2, f32,
                                  router=(moe_w_router, moe_b_router, layer))
        idx_km = top_idx.reshape(n_tok, LANES)[:, :TOP_K].T
        dest, row_tok, sb_e, sb_nsub = _routing_tables(idx_km)
        y_sorted = _moe_experts(h, d // LANES, row_tok, sb_e, sb_nsub, moe_w_gate_up, moe_b_gate_up,
                                moe_w_down, moe_b_down, layer)
        x = _moe_combine(y_sorted, dest, gates.reshape(n_tok, LANES), x2d, mod3, G2, seq).reshape(bsz, seq, d)
    return x
```

```python
import functools
import math

import numpy as np
import jax
import jax.numpy as jnp
from jax import lax
from jax.experimental import pallas as pl
from jax.experimental.pallas import tpu as pltpu

f32 = jnp.float32
bf16 = jnp.bfloat16
i32 = jnp.int32

N_Q_HEADS = 32
N_KV_HEADS = 4
HEAD_DIM = 64
Q_PER_KV = N_Q_HEADS // N_KV_HEADS
WINDOW = 128
NUM_BUCKETS = 32
MAX_DISTANCE = 128
SSM_GROUP_CH = 16
N_EXPERTS = 32
TOP_K = 4
SWIGLU_ALPHA = 1.702
SWIGLU_LIMIT = 7.0
NORM_EPS = 1e-5

LANES = 128
SUBLANES = 8
VMEM_LIMIT = 56 * 1024 * 1024
MOE_VMEM_LIMIT = 60 * 1024 * 1024
N_DMA_PRIORITIES = 2

ADALN_TN = 1024
NORM_TS = 1024
DENSE_TM = 1024
DENSE_TN = 512
COMBINE_TM = 128
ATTN_BLOCKS = 4
SSM_TILE_CH = 128
SSM_TIME_CHUNK = 512
MOE_SUB = 256
MOE_SB_SUBS = 8
MOE_SB_ROWS = MOE_SUB * MOE_SB_SUBS
MOE_GROUP = 4
MOE_CHUNK = 256
MOE_DOWN_CHUNK = 512


def _cparams(n_axes, vmem_limit=VMEM_LIMIT):
    return pltpu.CompilerParams(dimension_semantics=("arbitrary",) * n_axes, vmem_limit_bytes=vmem_limit)


def _mod_kernel(c_ref, w_ref, b_ref, o_ref):
    c = c_ref[...]
    cond = c * jax.nn.sigmoid(c)
    o_ref[...] = jnp.dot(cond.astype(bf16), w_ref[...].astype(bf16), preferred_element_type=f32) + b_ref[...]


def _modulation(c, ada_w, ada_b, layer):
    bsz, d = c.shape
    n = ada_w.shape[-1]
    tn = ADALN_TN
    assert n % tn == 0
    return pl.pallas_call(
        _mod_kernel,
        grid=(n // tn,),
        in_specs=[
            pl.BlockSpec((bsz, d), lambda j: (0, 0)),
            pl.BlockSpec((None, d, tn), lambda j: (layer, 0, j)),
            pl.BlockSpec((None, 1, tn), lambda j: (layer, 0, j)),
        ],
        out_specs=pl.BlockSpec((bsz, tn), lambda j: (0, j)),
        out_shape=jax.ShapeDtypeStruct((bsz, n), f32),
        compiler_params=_cparams(1),
        name="adaln_mod",
    )(c, ada_w, ada_b.reshape(ada_b.shape[0], 1, n))


def _norm_mod(x_ref, gain_ref, sc_ref, sh_ref):
    x = x_ref[0]
    ms = jnp.mean(x * x, axis=-1, keepdims=True)
    y = x * lax.rsqrt(ms + NORM_EPS) * gain_ref[...]
    return y * (1.0 + sc_ref[0]) + sh_ref[0]


def _norm_kernel(x_ref, gain_ref, sc_ref, sh_ref, h_ref):
    h_ref[0] = _norm_mod(x_ref, gain_ref, sc_ref, sh_ref).astype(h_ref.dtype)


def _split_bf16(v):
    hi = v.astype(bf16)
    lo = (v - hi.astype(f32)).astype(bf16)
    return hi, lo


def _norm_router_kernel(x_ref, gain_ref, sc_ref, sh_ref, wr_ref, br_ref, h_ref, idx_ref, gate_ref):
    h = _norm_mod(x_ref, gain_ref, sc_ref, sh_ref)
    ts, d = h.shape
    n_col = d // LANES
    for j in range(n_col):
        h_ref[pl.ds(j, ts, stride=n_col), :] = h[:, j * LANES:(j + 1) * LANES]
    h_hi, h_lo = _split_bf16(h)
    w_hi, w_lo = _split_bf16(wr_ref[...])
    dot = functools.partial(jnp.dot, preferred_element_type=f32)
    logits = dot(h_hi, w_hi) + (dot(h_hi, w_lo) + dot(h_lo, w_hi)) + br_ref[...]
    rows, n_exp = logits.shape
    col = lax.broadcasted_iota(i32, (rows, n_exp), 1)
    work = logits
    tops, idxs = [], []
    for _ in range(TOP_K):
        m = jnp.max(work, axis=-1, keepdims=True)
        idx = jnp.min(jnp.where(work == m, col, n_exp), axis=-1, keepdims=True)
        work = jnp.where(col == idx, -jnp.inf, work)
        tops.append(m)
        idxs.append(idx)
    es = [jnp.exp(t - tops[0]) for t in tops]
    denom = functools.reduce(lambda a, b: a + b, es)
    lane = lax.broadcasted_iota(i32, (rows, LANES), 1)
    idx_out = jnp.zeros((rows, LANES), i32)
    gate_out = jnp.zeros((rows, LANES), f32)
    for k in range(TOP_K):
        idx_out = jnp.where(lane == k, idxs[k], idx_out)
        gate_out = jnp.where(lane == k, es[k] / denom, gate_out)
    idx_ref[0] = idx_out
    gate_ref[0] = gate_out


def _norm(x, gain, mod3, sc_chunk, sh_chunk, out_dtype, router=None):
    bsz, seq, d = x.shape
    ts = NORM_TS
    assert seq % ts == 0 and d % LANES == 0
    in_specs = [
        pl.BlockSpec((1, ts, d), lambda b, s: (b, s, 0)),
        pl.BlockSpec((1, d), lambda b, s: (0, 0)),
        pl.BlockSpec((1, 1, d), lambda b, s: (b, 0, sc_chunk)),
        pl.BlockSpec((1, 1, d), lambda b, s: (b, 0, sh_chunk)),
    ]
    h_spec = pl.BlockSpec((1, ts, d), lambda b, s: (b, s, 0))
    args = [x, gain.reshape(1, d), mod3, mod3]
    if router is None:
        return pl.pallas_call(
            _norm_kernel, grid=(bsz, seq // ts), in_specs=in_specs, out_specs=h_spec,
            out_shape=jax.ShapeDtypeStruct((bsz, seq, d), out_dtype),
            compiler_params=_cparams(2), name="norm_mod",
        )(*args)
    w_router, b_router, layer = router
    n_exp = w_router.shape[-1]
    in_specs += [
        pl.BlockSpec((None, d, n_exp), lambda b, s: (layer, 0, 0)),
        pl.BlockSpec((None, 1, n_exp), lambda b, s: (layer, 0, 0)),
    ]
    lane_spec = pl.BlockSpec((1, ts, LANES), lambda b, s: (b, s, 0))
    n_col = d // LANES
    h3_spec = pl.BlockSpec((ts * n_col, LANES), lambda b, s: (b * (seq // ts) + s, 0))
    return pl.pallas_call(
        _norm_router_kernel, grid=(bsz, seq // ts), in_specs=in_specs,
        out_specs=[h3_spec, lane_spec, lane_spec],
        out_shape=[jax.ShapeDtypeStruct((bsz * seq * n_col, LANES), f32),
                   jax.ShapeDtypeStruct((bsz, seq, LANES), i32),
                   jax.ShapeDtypeStruct((bsz, seq, LANES), f32)],
        compiler_params=_cparams(2), name="norm_router",
    )(*args, w_router, b_router.reshape(b_router.shape[0], 1, n_exp))


def _dense_kernel(*refs, n_w, n_extra, epilogue):
    x_ref = refs[0]
    w_refs = refs[1:1 + n_w]
    b_refs = refs[1 + n_w:1 + 2 * n_w]
    e_refs = refs[1 + 2 * n_w:1 + 2 * n_w + n_extra]
    o_ref = refs[1 + 2 * n_w + n_extra]
    wbf_refs = refs[2 + 2 * n_w + n_extra:]

    @pl.when(pl.program_id(1) == 0)
    def _():
        for w_ref, wbf in zip(w_refs, wbf_refs):
            wbf[...] = w_ref[...].astype(bf16)

    x = x_ref[...]
    accs = [jnp.dot(x, wbf[...], preferred_element_type=f32) + b_ref[...] for wbf, b_ref in zip(wbf_refs, b_refs)]
    o_ref[...] = epilogue(accs, [e[...] for e in e_refs]).astype(o_ref.dtype)


def _dense(x, ws, bs, layer, epilogue, extras, out_dtype, tm, tn, name):
    m, k = x.shape
    n = ws[0].shape[-1]
    assert m % tm == 0 and n % tn == 0
    in_specs = [pl.BlockSpec((tm, k), lambda j, i: (i, 0))]
    in_specs += [pl.BlockSpec((None, k, tn), lambda j, i: (layer, 0, j)) for _ in ws]
    in_specs += [pl.BlockSpec((None, 1, tn), lambda j, i: (layer, 0, j)) for _ in bs]
    in_specs += [spec for _, spec in extras]
    args = [x] + list(ws) + [b.reshape(b.shape[0], 1, n) for b in bs] + [a for a, _ in extras]
    return pl.pallas_call(
        functools.partial(_dense_kernel, n_w=len(ws), n_extra=len(extras), epilogue=epilogue),
        grid=(n // tn, m // tm),
        in_specs=in_specs,
        out_specs=pl.BlockSpec((tm, tn), lambda j, i: (i, j)),
        out_shape=jax.ShapeDtypeStruct((m, n), out_dtype),
        scratch_shapes=[pltpu.VMEM((k, tn), bf16) for _ in ws],
        compiler_params=_cparams(2), name=name,
    )(*args)


def _gated_residual_extras(x2d, mod3, gate_chunk, d, tm, tn, seq):
    tiles_per_batch = seq // tm
    return [
        (x2d, pl.BlockSpec((tm, tn), lambda j, i: (i, j))),
        (mod3, pl.BlockSpec((None, 1, tn), lambda j, i: (i // tiles_per_batch, 0, gate_chunk * (d // tn) + j))),
    ]


def _attn_kernel(sinks_ref, q_ref, kp_ref, kc_ref, vp_ref, vc_ref, qg_ref, kg_ref, bias_ref, o_ref):
    n = pl.program_id(1)
    lane = lax.broadcasted_iota(i32, (1, LANES), 1)
    lo = lane < HEAD_DIM
    bq = kp_ref.shape[1]
    n_blk = q_ref.shape[1] // bq

    def halfnorm(v, gain):
        sq = v * v
        s_lo = jnp.sum(jnp.where(lo, sq, 0.0), axis=-1, keepdims=True)
        s_hi = jnp.sum(jnp.where(lo, 0.0, sq), axis=-1, keepdims=True)
        ms = jnp.where(lo, s_lo, s_hi) * (1.0 / HEAD_DIM)
        return v * lax.rsqrt(ms + NORM_EPS) * gain

    k_all = jnp.concatenate([kp_ref[0], kc_ref[0]], axis=0)
    v_all = jnp.concatenate([vp_ref[0], vc_ref[0]], axis=0)
    kcol = lax.broadcasted_iota(i32, (1, 2 * bq), 1)
    qgain = qg_ref[...]
    kgain = kg_ref[...]
    pairs = Q_PER_KV // 2
    for c in range(N_KV_HEADS // 2):
        kn_all = halfnorm(k_all[:, c * LANES:(c + 1) * LANES], kgain)
        kr_all = pltpu.roll(kn_all, HEAD_DIM, 1)
        vn_all = v_all[:, c * LANES:(c + 1) * LANES]
        vr_all = pltpu.roll(vn_all, HEAD_DIM, 1)
        for blk in range(n_blk):
            rows = slice(blk * bq, (blk + 1) * bq)
            keys = slice(blk * bq, (blk + 2) * bq)
            kn, kr, vn, vr = kn_all[keys], kr_all[keys], vn_all[keys], vr_all[keys]
            key_ok = jnp.logical_or(kcol >= bq, n > 0) if blk == 0 else None
            for half in range(2):
                g = 2 * c + half
                k_src, k_rot = (kn, kr) if half == 0 else (kr, kn)
                v_src, v_rot = (vn, vr) if half == 0 else (vr, vn)
                k_par = [jnp.where(lo, k_src, 0.0).astype(bf16), jnp.where(lo, 0.0, k_rot).astype(bf16)]
                v_par = [jnp.where(lo, v_src, 0.0).astype(bf16), jnp.where(lo, 0.0, v_rot).astype(bf16)]
                qs = [halfnorm(q_ref[0, rows, (g * pairs + p) * LANES:(g * pairs + p + 1) * LANES], qgain)
                      for p in range(pairs)]
                qg = (jnp.concatenate(qs, axis=0) * (1.0 / math.sqrt(HEAD_DIM))).astype(bf16)
                acc = None
                for par in range(2):
                    s = lax.dot_general(qg, k_par[par], (((1,), (1,)), ((), ())), preferred_element_type=f32)
                    s = s + bias_ref[g, par]
                    if key_ok is not None:
                        s = jnp.where(key_ok, s, -jnp.inf)
                    sink = jnp.concatenate(
                        [jnp.full((bq, 1), sinks_ref[g * Q_PER_KV + 2 * p + par], f32) for p in range(pairs)], axis=0)
                    m = jnp.maximum(jnp.max(s, axis=-1, keepdims=True), sink)
                    e = jnp.exp(s - m)
                    den = jnp.sum(e, axis=-1, keepdims=True) + jnp.exp(sink - m)
                    probs = (e * (1.0 / den)).astype(bf16)
                    o = jnp.dot(probs, v_par[par], preferred_element_type=f32)
                    acc = o if acc is None else acc + o
                for p in range(pairs):
                    col = (g * pairs + p) * LANES
                    o_ref[0, rows, col:col + LANES] = acc[p * bq:(p + 1) * bq].astype(o_ref.dtype)


def _t5_bucket(dist):
    nn = np.maximum(dist, 0)
    max_exact = NUM_BUCKETS // 2
    large = max_exact + (np.log(np.maximum(nn, 1) / max_exact) / np.log(MAX_DISTANCE / max_exact)
                         * (NUM_BUCKETS - max_exact)).astype(np.int32)
    large = np.minimum(large, NUM_BUCKETS - 1)
    return np.where(nn < max_exact, nn, large).astype(np.int32)


def _attention(qkv, q_gain, k_gain, sinks, rel_bias):
    bsz, seq, _ = qkv.shape
    bq = WINDOW
    q_dim = N_Q_HEADS * HEAD_DIM
    kv_dim = N_KV_HEADS * HEAD_DIM
    pairs = Q_PER_KV // 2
    ql = np.arange(bq)[:, None]
    kl = np.arange(2 * bq)[None, :]
    dist = ql + bq - kl
    in_window = (dist >= 0) & (dist < WINDOW)
    bucket = jnp.asarray(_t5_bucket(dist).reshape(1, -1))
    onehot_t = (bucket == jnp.arange(NUM_BUCKETS, dtype=i32)[:, None]).astype(f32)
    bias = jnp.dot(rel_bias.astype(f32).T, onehot_t, precision=lax.Precision.HIGHEST).reshape(N_Q_HEADS, bq, 2 * bq)
    bias = jnp.where(jnp.asarray(in_window)[None], bias, -jnp.inf)
    bias = bias.reshape(N_KV_HEADS, pairs, 2, bq, 2 * bq)
    bias = jnp.transpose(bias, (0, 2, 1, 3, 4)).reshape(N_KV_HEADS, 2, pairs * bq, 2 * bq)
    gain2 = lambda gn: jnp.concatenate([gn, gn]).reshape(1, LANES).astype(f32)
    k_blk = q_dim // kv_dim
    nb = ATTN_BLOCKS
    assert seq % (nb * bq) == 0
    grid_spec = pltpu.PrefetchScalarGridSpec(
        num_scalar_prefetch=1,
        grid=(bsz, seq // (nb * bq)),
        in_specs=[
            pl.BlockSpec((1, nb * bq, q_dim), lambda b, n, s: (b, n, 0)),
            pl.BlockSpec((1, bq, kv_dim), lambda b, n, s: (b, jnp.maximum(nb * n - 1, 0), k_blk)),
            pl.BlockSpec((1, nb * bq, kv_dim), lambda b, n, s: (b, n, k_blk)),
            pl.BlockSpec((1, bq, kv_dim), lambda b, n, s: (b, jnp.maximum(nb * n - 1, 0), k_blk + 1)),
            pl.BlockSpec((1, nb * bq, kv_dim), lambda b, n, s: (b, n, k_blk + 1)),
            pl.BlockSpec((1, LANES), lambda b, n, s: (0, 0)),
            pl.BlockSpec((1, LANES), lambda b, n, s: (0, 0)),
            pl.BlockSpec((N_KV_HEADS, 2, pairs * bq, 2 * bq), lambda b, n, s: (0, 0, 0, 0)),
        ],
        out_specs=pl.BlockSpec((1, nb * bq, q_dim), lambda b, n, s: (b, n, 0)),
    )
    return pl.pallas_call(
        _attn_kernel, grid_spec=grid_spec,
        out_shape=jax.ShapeDtypeStruct((bsz, seq, q_dim), bf16),
        compiler_params=_cparams(2), name="swa_attention",
    )(sinks.astype(f32), qkv, qkv, qkv, qkv, qkv, gain2(q_gain), gain2(k_gain), bias)


def _gelu_tanh(y):
    return 0.5 * y * (1.0 + jnp.tanh(math.sqrt(2.0 / math.pi) * (y + 0.044715 * (y * y * y))))


def _ssm_kernel(u_ref, bm_ref, cm_ref, are_ref, aim_ref, d_ref, y_ref, us, xs, st, ys):
    bsz, ts, _ = u_ref.shape
    n_state = are_ref.shape[-1]

    @pl.when(pl.program_id(1) == 0)
    def _():
        st[...] = jnp.zeros_like(st)

    n_slab = us.shape[0]
    for h in range(n_slab):
        for b in range(bsz):
            us[h, pl.ds(b, ts, stride=bsz), :] = u_ref[b, :, h * LANES:(h + 1) * LANES]
    u = jnp.concatenate([us[h] for h in range(n_slab)], axis=1)
    xs[...] = jnp.dot(u.astype(bf16), bm_ref[0], preferred_element_type=f32)
    a_re = jnp.broadcast_to(are_ref[0], (bsz, n_state))
    a_im = jnp.broadcast_to(aim_ref[0], (bsz, n_state))

    def step(t, carry):
        s_re, s_im = carry
        r0 = pl.multiple_of(t * bsz, bsz)
        x_re = xs[pl.ds(r0, bsz), 0:n_state]
        x_im = xs[pl.ds(r0, bsz), n_state:2 * n_state]
        n_re = a_re * s_re - a_im * s_im + x_re
        n_im = a_re * s_im + a_im * s_re + x_im
        xs[pl.ds(r0, bsz), 0:n_state] = n_re
        xs[pl.ds(r0, bsz), n_state:2 * n_state] = n_im
        return n_re, n_im

    s_re, s_im = lax.fori_loop(0, ts, step, (st[0], st[1]), unroll=8)
    st[0] = s_re
    st[1] = s_im
    y = (jnp.dot(xs[:, 0:n_state].astype(bf16), cm_ref[0, 0:n_state, :], preferred_element_type=f32)
         + jnp.dot(xs[:, n_state:2 * n_state].astype(bf16), cm_ref[0, n_state:2 * n_state, :],
                   preferred_element_type=f32)
         + d_ref[0] * u)
    y = _gelu_tanh(y)
    for h in range(n_slab):
        ys[h] = y[:, h * LANES:(h + 1) * LANES]
        for b in range(bsz):
            y_ref[b, :, h * LANES:(h + 1) * LANES] = ys[h, pl.ds(b, ts, stride=bsz), :].astype(y_ref.dtype)


def _ssm(h, lam_re, lam_im, log_dt, b_re, b_im, c_re, c_im, d_skip):
    bsz, seq, width = h.shape
    assert bsz == SUBLANES
    n_grp, n_st = lam_re.shape
    tile_ch = SSM_TILE_CH
    gpt = tile_ch // SSM_GROUP_CH
    n_tiles = width // tile_ch
    ns = gpt * n_st
    dt = jnp.exp(log_dt.astype(f32))[:, None]
    lr, li = lam_re.astype(f32), lam_im.astype(f32)
    mag = jnp.exp(lr * dt)
    ab_re, ab_im = mag * jnp.cos(li * dt), mag * jnp.sin(li * dt)
    den = lr * lr + li * li
    nr, ni = ab_re - 1.0, ab_im
    f_re = (nr * lr + ni * li) / den
    f_im = (ni * lr - nr * li) / den
    br, bi = b_re.astype(f32), b_im.astype(f32)
    bb_re = f_re[..., None] * br - f_im[..., None] * bi
    bb_im = f_re[..., None] * bi + f_im[..., None] * br
    eye = jnp.eye(gpt, dtype=f32)

    def blockdiag_in(bb):
        t = bb.reshape(n_tiles, gpt, n_st, SSM_GROUP_CH)
        return jnp.einsum("tgpc,gh->tgchp", t, eye).reshape(n_tiles, tile_ch, ns)

    def blockdiag_out(cc):
        t = cc.reshape(n_tiles, gpt, SSM_GROUP_CH, n_st)
        return jnp.einsum("tgcp,gh->tgphc", t, eye).reshape(n_tiles, ns, tile_ch)

    bm = jnp.concatenate([blockdiag_in(bb_re), blockdiag_in(bb_im)], axis=-1).astype(bf16)
    cm = jnp.concatenate([blockdiag_out(c_re.astype(f32)), -blockdiag_out(c_im.astype(f32))], axis=1).astype(bf16)
    a_re = ab_re.reshape(n_tiles, 1, ns)
    a_im = ab_im.reshape(n_tiles, 1, ns)
    dsk = d_skip.astype(f32).reshape(n_tiles, 1, tile_ch)
    ts = SSM_TIME_CHUNK
    return pl.pallas_call(
        _ssm_kernel,
        grid=(n_tiles, seq // ts),
        in_specs=[
            pl.BlockSpec((bsz, ts, tile_ch), lambda g, t: (0, t, g)),
            pl.BlockSpec((1, tile_ch, 2 * ns), lambda g, t: (g, 0, 0)),
            pl.BlockSpec((1, 2 * ns, tile_ch), lambda g, t: (g, 0, 0)),
            pl.BlockSpec((1, 1, ns), lambda g, t: (g, 0, 0)),
            pl.BlockSpec((1, 1, ns), lambda g, t: (g, 0, 0)),
            pl.BlockSpec((1, 1, tile_ch), lambda g, t: (g, 0, 0)),
        ],
        out_specs=pl.BlockSpec((bsz, ts, tile_ch), lambda g, t: (0, t, g)),
        out_shape=jax.ShapeDtypeStruct((bsz, seq, width), bf16),
        scratch_shapes=[
            pltpu.VMEM((tile_ch // LANES, ts * bsz, LANES), f32),
            pltpu.VMEM((ts * bsz, 2 * ns), f32),
            pltpu.VMEM((2, bsz, ns), f32),
            pltpu.VMEM((tile_ch // LANES, ts * bsz, LANES), f32),
        ],
        compiler_params=_cparams(2), name="s5_ssm",
    )(h, bm, cm, a_re, a_im, dsk)


GATHER_PITCH = 24


def _swiglu(g, l):
    glu = jnp.minimum(g, SWIGLU_LIMIT)
    lin = jnp.clip(l, -SWIGLU_LIMIT, SWIGLU_LIMIT)
    return glu * jax.nn.sigmoid(SWIGLU_ALPHA * glu) * (lin + 1.0)


def _for_row_groups(nsub, cast, fn):
    n_big = nsub // MOE_GROUP
    big_rows = MOE_GROUP * MOE_SUB

    @pl.when(n_big > 0)
    def _():
        fn(0, big_rows, cast())

    @pl.when(n_big == 0)
    def _():
        fn(0, MOE_SUB, cast())

    def big(i, carry):
        fn(pl.multiple_of(i * big_rows, big_rows), big_rows, None)
        return carry

    def small(i, carry):
        fn(pl.multiple_of(i * MOE_SUB, MOE_SUB), MOE_SUB, None)
        return carry

    lax.fori_loop(1, n_big, big, 0)
    lax.fori_loop(jnp.maximum(n_big * MOE_GROUP, 1), nsub, small, 0)


def _moe_kernel(sbe_ref, sbn_ref, idx0_ref, idxn_ref, h_hbm, wg_ref, wl_ref, bgu_ref, wd_ref, bd_ref, o_ref,
                xbf, stg, sem, a_scr, wg_bf, wl_bf, wd_bf, *, n_chunks):
    s = pl.program_id(0)
    c = pl.program_id(1)
    n_sb = pl.num_programs(0)
    nsub = sbn_ref[s]
    slot = lax.rem(s, 2)
    nsub_next = jnp.where(s + 1 < n_sb, sbn_ref[jnp.minimum(s + 1, n_sb - 1)], 0)
    n_col = xbf.shape[2] // LANES
    stg_rows = MOE_SUB * GATHER_PITCH

    def issue(idx_ref, off, st_slot):
        def body(r2, carry):
            for prio in range(N_DMA_PRIORITIES):
                r = r2 * N_DMA_PRIORITIES + prio
                src = pl.multiple_of(idx_ref[0, 0, off + r] * n_col, SUBLANES)
                dst = pl.multiple_of(st_slot * stg_rows + r * GATHER_PITCH, SUBLANES)
                pltpu.make_async_copy(h_hbm.at[pl.ds(src, n_col)], stg.at[pl.ds(dst, n_col)],
                                      sem.at[st_slot]).start(priority=prio)
            return carry

        lax.fori_loop(0, MOE_SUB // N_DMA_PRIORITIES, body, 0, unroll=4)

    def finish(st_slot, x_slot, sub):
        base = pl.multiple_of(st_slot * stg_rows, SUBLANES)
        pltpu.make_async_copy(h_hbm.at[pl.ds(0, MOE_SUB * n_col)], stg.at[pl.ds(base, MOE_SUB * n_col)],
                              sem.at[st_slot]).wait()
        r0 = pl.multiple_of(sub * MOE_SUB, MOE_SUB)
        for j in range(n_col):
            piece = stg[pl.ds(base + j, MOE_SUB, stride=GATHER_PITCH), :]
            xbf[x_slot, pl.ds(r0, MOE_SUB), j * LANES:(j + 1) * LANES] = piece.astype(bf16)

    @pl.when(jnp.logical_and(s == 0, c == 0))
    def _():
        def first(j, carry):
            issue(idx0_ref, j * MOE_SUB, 0)
            finish(0, 0, j)
            return carry

        lax.fori_loop(0, nsub, first, 0)

    @pl.when(jnp.logical_and(c >= 1, c - 1 < nsub_next))
    def _():
        finish(lax.rem(c - 1, 2), 1 - slot, c - 1)

    @pl.when(c < nsub_next)
    def _():
        issue(idxn_ref, c * MOE_SUB, lax.rem(c, 2))

    @pl.when(jnp.logical_and(c < n_chunks, nsub > 0))
    def _():
        def cast():
            wg, wl = wg_ref[...].astype(bf16), wl_ref[...].astype(bf16)
            wg_bf[...] = wg
            wl_bf[...] = wl
            return wg, wl

        def up(r, rows, weights):
            wg, wl = weights if weights is not None else (wg_bf[...], wl_bf[...])
            xs = xbf[slot, pl.ds(r, rows), :]
            g = jnp.dot(xs, wg, preferred_element_type=f32) + bgu_ref[pl.ds(c, 1), :]
            l = jnp.dot(xs, wl, preferred_element_type=f32) + bgu_ref[pl.ds(n_chunks + c, 1), :]
            a_scr[c, pl.ds(r, rows), :] = _swiglu(g, l).astype(bf16)

        _for_row_groups(nsub, cast, up)

    @pl.when(jnp.logical_and(c >= n_chunks, nsub > 0))
    def _():
        def cast():
            wd = wd_ref[...].astype(bf16)
            wd_bf[...] = wd
            return wd

        def down(r, rows, weights):
            wd = weights if weights is not None else wd_bf[...]
            a = jnp.concatenate([a_scr[cc, pl.ds(r, rows), :] for cc in range(n_chunks)], axis=1)
            o_ref[pl.ds(r, rows), :] = jnp.dot(a, wd, preferred_element_type=f32) + bd_ref[pl.ds(c - n_chunks, 1), :]

        _for_row_groups(nsub, cast, down)

    @pl.when(c >= n_chunks)
    def _():
        def zero(i, carry):
            r = pl.multiple_of(i * MOE_SUB, MOE_SUB)
            o_ref[pl.ds(r, MOE_SUB), :] = jnp.zeros((MOE_SUB, o_ref.shape[1]), f32)
            return carry

        lax.fori_loop(nsub, MOE_SB_SUBS, zero, 0)


def _moe_experts(h_rows, n_col, row_tok, sb_expert, sb_nsub, w_gate_up, b_gate_up, w_down, b_down, layer):
    n_rows = row_tok.shape[0]
    d = n_col * LANES
    n_sb = n_rows // MOE_SB_ROWS
    d_exp = w_down.shape[2]
    n_chunks = d_exp // MOE_CHUNK
    n_out_chunks = d // MOE_DOWN_CHUNK
    last = n_chunks - 1
    last_out = n_out_chunks - 1

    def valid(s, n):
        return n[s] > 0

    def wg_map(s, c, e, n):
        return (layer, e[s], 0, jnp.where(valid(s, n), jnp.minimum(c, last), last))

    def wl_map(s, c, e, n):
        return (layer, e[s], 0, n_chunks + jnp.where(valid(s, n), jnp.minimum(c, last), last))

    def wd_map(s, c, e, n):
        return (layer, e[s], 0, jnp.where(valid(s, n), jnp.maximum(c - n_chunks, 0), last_out))

    def out_map(s, c, e, n):
        return (s, jnp.maximum(c - n_chunks, 0))

    def next_idx_map(s, c, e, n):
        return (jnp.minimum(s + 1, n_sb - 1), 0, 0)

    def bias_map(s, c, e, n):
        return (layer, e[s], 0, 0)

    grid_spec = pltpu.PrefetchScalarGridSpec(
        num_scalar_prefetch=2,
        grid=(n_sb, n_chunks + n_out_chunks),
        in_specs=[
            pl.BlockSpec((1, 1, MOE_SB_ROWS), lambda s, c, e, n: (0, 0, 0), memory_space=pltpu.SMEM),
            pl.BlockSpec((1, 1, MOE_SB_ROWS), next_idx_map, memory_space=pltpu.SMEM),
            pl.BlockSpec(memory_space=pl.ANY),
            pl.BlockSpec((None, None, d, MOE_CHUNK), wg_map),
            pl.BlockSpec((None, None, d, MOE_CHUNK), wl_map),
            pl.BlockSpec((None, None, 2 * n_chunks, MOE_CHUNK), bias_map),
            pl.BlockSpec((None, None, d_exp, MOE_DOWN_CHUNK), wd_map),
            pl.BlockSpec((None, None, n_out_chunks, MOE_DOWN_CHUNK), bias_map),
        ],
        out_specs=pl.BlockSpec((MOE_SB_ROWS, MOE_DOWN_CHUNK), out_map),
        scratch_shapes=[
            pltpu.VMEM((2, MOE_SB_ROWS, d), bf16),
            pltpu.VMEM((2 * MOE_SUB * GATHER_PITCH, LANES), f32),
            pltpu.SemaphoreType.DMA((2,)),
            pltpu.VMEM((n_chunks, MOE_SB_ROWS, MOE_CHUNK), bf16),
            pltpu.VMEM((d, MOE_CHUNK), bf16),
            pltpu.VMEM((d, MOE_CHUNK), bf16),
            pltpu.VMEM((d_exp, MOE_DOWN_CHUNK), bf16),
        ],
    )
    n_l, n_e = b_gate_up.shape[:2]
    return pl.pallas_call(
        functools.partial(_moe_kernel, n_chunks=n_chunks), grid_spec=grid_spec,
        out_shape=jax.ShapeDtypeStruct((n_rows, d), f32),
        compiler_params=_cparams(2, MOE_VMEM_LIMIT), name="moe_experts",
    )(sb_expert, sb_nsub, row_tok.reshape(n_sb, 1, MOE_SB_ROWS), row_tok.reshape(n_sb, 1, MOE_SB_ROWS),
      h_rows, w_gate_up, w_gate_up,
      b_gate_up.reshape(n_l, n_e, 2 * n_chunks, MOE_CHUNK),
      w_down, b_down.reshape(n_l, n_e, n_out_chunks, MOE_DOWN_CHUNK))


def _combine_kernel(dest_ref, dest_next_ref, y_hbm, gate_ref, x_ref, g2_ref, o_ref, ybuf, sem):
    i = pl.program_id(0)
    n = pl.num_programs(0)
    tm = x_ref.shape[0]
    slot = lax.rem(i, 2)

    def issue(dest_blk, sl):
        for k in range(TOP_K):
            def body(t2, carry, k=k):
                for prio in range(N_DMA_PRIORITIES):
                    t = t2 * N_DMA_PRIORITIES + prio
                    pltpu.make_async_copy(y_hbm.at[pl.ds(dest_blk[0, 0, k * tm + t], 1)],
                                          ybuf.at[sl * TOP_K + k, pl.ds(t, 1)], sem.at[sl]).start(priority=prio)
                return carry

            lax.fori_loop(0, tm // N_DMA_PRIORITIES, body, 0, unroll=4)

    @pl.when(i == 0)
    def _():
        issue(dest_ref, 0)

    @pl.when(i + 1 < n)
    def _():
        issue(dest_next_ref, 1 - slot)

    for k in range(TOP_K):
        pltpu.make_async_copy(y_hbm.at[pl.ds(0, tm)], ybuf.at[slot * TOP_K + k], sem.at[slot]).wait()
    gates = gate_ref[...]
    acc = gates[:, 0:1] * ybuf[slot * TOP_K]
    for k in range(1, TOP_K):
        acc = acc + gates[:, k:k + 1] * ybuf[slot * TOP_K + k]
    o_ref[...] = x_ref[...] + g2_ref[...] * acc


def _moe_combine(y_sorted, dest, gates, x2d, mod3, gate_chunk, seq):
    n_tok, d = x2d.shape
    tm = COMBINE_TM
    assert seq % tm == 0
    n_tiles = n_tok // tm
    dest_km = jnp.transpose(dest.reshape(TOP_K, n_tiles, tm), (1, 0, 2)).reshape(n_tiles, 1, TOP_K * tm)
    tiles_per_batch = seq // tm
    return pl.pallas_call(
        _combine_kernel,
        grid=(n_tiles,),
        in_specs=[
            pl.BlockSpec((1, 1, TOP_K * tm), lambda i: (i, 0, 0), memory_space=pltpu.SMEM),
            pl.BlockSpec((1, 1, TOP_K * tm), lambda i: (jnp.minimum(i + 1, n_tiles - 1), 0, 0),
                         memory_space=pltpu.SMEM),
            pl.BlockSpec(memory_space=pl.ANY),
            pl.BlockSpec((tm, LANES), lambda i: (i, 0)),
            pl.BlockSpec((tm, d), lambda i: (i, 0)),
            pl.BlockSpec((None, 1, d), lambda i: (i // tiles_per_batch, 0, gate_chunk)),
        ],
        out_specs=pl.BlockSpec((tm, d), lambda i: (i, 0)),
        out_shape=jax.ShapeDtypeStruct((n_tok, d), f32),
        scratch_shapes=[pltpu.VMEM((2 * TOP_K, tm, d), f32), pltpu.SemaphoreType.DMA((2,))],
        compiler_params=_cparams(1), name="moe_combine",
    )(dest_km, dest_km, y_sorted, gates, x2d, mod3)


def _routing_tables(idx_km):
    n_tok = idx_km.shape[1]
    n_assign = n_tok * TOP_K
    n_sb_max = (n_assign // MOE_SUB + N_EXPERTS + N_EXPERTS * (MOE_SB_SUBS - 1)) // MOE_SB_SUBS
    experts = jnp.arange(N_EXPERTS, dtype=i32)
    onehot = idx_km[:, :, None] == experts[None, None, :]
    per_tok = jnp.sum(onehot.astype(i32), axis=0)
    before = jnp.cumsum(per_tok, axis=0) - per_tok
    rank = jnp.sum(jnp.where(onehot, before[None], 0), axis=2)
    counts = before[-1] + per_tok[-1]
    n_sub_e = (counts + MOE_SUB - 1) // MOE_SUB
    n_sb_e = (n_sub_e + MOE_SB_SUBS - 1) // MOE_SB_SUBS
    sb_end = jnp.cumsum(n_sb_e)
    sb_start = sb_end - n_sb_e
    base = n_sub_e // jnp.maximum(n_sb_e, 1)
    rem = n_sub_e - base * n_sb_e
    tables = jnp.stack([sb_start, base, rem], axis=1).astype(bf16)
    looked = jnp.dot(onehot.astype(bf16).reshape(n_assign, N_EXPERTS), tables,
                     preferred_element_type=f32).astype(i32).reshape(TOP_K, n_tok, 3)
    a_start, a_base, a_rem = looked[..., 0], looked[..., 1], looked[..., 2]
    assert MOE_SUB & (MOE_SUB - 1) == 0
    q = lax.shift_right_logical(rank, int(math.log2(MOE_SUB)))
    max_sb_per_expert = -(-(-(-n_tok // MOE_SUB)) // MOE_SB_SUBS)
    sb_local = jnp.zeros_like(q)
    for j in range(1, max_sb_per_expert):
        sb_local = sb_local + (q >= j * a_base + jnp.minimum(j, a_rem)).astype(i32)
    sub_in = q - (sb_local * a_base + jnp.minimum(sb_local, a_rem))
    dest = ((a_start + sb_local) * MOE_SB_ROWS + sub_in * MOE_SUB + (rank & (MOE_SUB - 1))).astype(i32)
    dest = lax.optimization_barrier(dest)
    tok = jnp.broadcast_to(jnp.arange(n_tok, dtype=i32)[None, :], (TOP_K, n_tok))
    row_tok = jnp.zeros((n_sb_max * MOE_SB_ROWS,), i32).at[dest.reshape(-1)].set(
        tok.reshape(-1), unique_indices=True, mode="promise_in_bounds")
    s = jnp.arange(n_sb_max, dtype=i32)
    n_sb_total = sb_end[-1]
    sb_valid = s < n_sb_total
    sb_src = jnp.where(sb_valid, s, n_sb_total - 1).astype(i32)
    sb_e = jnp.minimum(jnp.searchsorted(sb_end, sb_src, side="right"), N_EXPERTS - 1).astype(i32)
    sb_local_s = sb_src - sb_start[sb_e]
    sb_nsub = jnp.where(sb_valid, base[sb_e] + (sb_local_s < rem[sb_e]).astype(i32), 0).astype(i32)
    return dest, row_tok, sb_e, sb_nsub


def kernel(x, c, rel_bias, norm_gain, ada_w, ada_b, attn_w_qkv, attn_b_qkv, attn_q_gain, attn_k_gain, attn_sinks, attn_w_o, attn_b_o, ssm_lam_re, ssm_lam_im, ssm_log_dt, ssm_b_re, ssm_b_im, ssm_c_re, ssm_c_im, ssm_d, ssm_w_glu_a, ssm_b_glu_a, ssm_w_glu_b, ssm_b_glu_b, moe_w_router, moe_b_router, moe_w_gate_up, moe_b_gate_up, moe_w_down, moe_b_down):
    bsz, seq, d = x.shape
    n_tok = bsz * seq
    depth = norm_gain.shape[0]
    tm, tn = DENSE_TM, DENSE_TN
    assert seq % tm == 0 and d % tn == 0
    SH1, SC1, G1, SH2, SC2, G2 = range(6)

    def gated(accs, extras):
        x_res, gate = extras
        return x_res + gate * accs[0]

    def gated_glu(accs, extras):
        x_res, gate = extras
        return x_res + gate * (accs[0] * jax.nn.sigmoid(accs[1]))

    for layer in range(depth):
        mod3 = _modulation(c, ada_w, ada_b, layer).reshape(bsz, 1, 6 * d)
        x2d = x.reshape(n_tok, d)
        res_extras = _gated_residual_extras(x2d, mod3, G1, d, tm, tn, seq)
        i = layer // 2
        if layer % 2 == 0:
            h = _norm(x, norm_gain[layer, 0], mod3, SC1, SH1, bf16)
            qkv = _dense(h.reshape(n_tok, d), [attn_w_qkv], [attn_b_qkv], i, lambda accs, extras: accs[0], [],
                         f32, tm, tn, "qkv_proj")
            o = _attention(qkv.reshape(bsz, seq, -1), attn_q_gain[i], attn_k_gain[i], attn_sinks[i], rel_bias)
            x2d = _dense(o.reshape(n_tok, -1), [attn_w_o], [attn_b_o], i, gated, res_extras, f32, tm, tn, "attn_out")
        else:
            h = _norm(x, norm_gain[layer, 0], mod3, SC1, SH1, f32)
            y = _ssm(h, ssm_lam_re[i], ssm_lam_im[i], ssm_log_dt[i], ssm_b_re[i], ssm_b_im[i],
                     ssm_c_re[i], ssm_c_im[i], ssm_d[i])
            x2d = _dense(y.reshape(n_tok, d), [ssm_w_glu_a, ssm_w_glu_b], [ssm_b_glu_a, ssm_b_glu_b], i,
                         gated_glu, res_extras, f32, tm, tn, "ssm_glu")
        x = x2d.reshape(bsz, seq, d)
        h, top_idx, gates = _norm(x, norm_gain[layer, 1], mod3, SC2, SH2, f32,
                                  router=(moe_w_router, moe_b_router, layer))
        idx_km = top_idx.reshape(n_tok, LANES)[:, :TOP_K].T
        dest, row_tok, sb_e, sb_nsub = _routing_tables(idx_km)
        y_sorted = _moe_experts(h, d // LANES, row_tok, sb_e, sb_nsub, moe_w_gate_up, moe_b_gate_up,
                                moe_w_down, moe_b_down, layer)
        x = _moe_combine(y_sorted, dest, gates.reshape(n_tok, LANES), x2d, mod3, G2, seq).reshape(bsz, seq, d)
    return x
```

```python
import functools
import math

import numpy as np
import jax
import jax.numpy as jnp
from jax import lax
from jax.experimental import pallas as pl
from jax.experimental.pallas import tpu as pltpu

f32 = jnp.float32
bf16 = jnp.bfloat16
i32 = jnp.int32

N_Q_HEADS = 32
N_KV_HEADS = 4
HEAD_DIM = 64
Q_PER_KV = N_Q_HEADS // N_KV_HEADS
WINDOW = 128
NUM_BUCKETS = 32
MAX_DISTANCE = 128
SSM_GROUP_CH = 16
N_EXPERTS = 32
TOP_K = 4
SWIGLU_ALPHA = 1.702
SWIGLU_LIMIT = 7.0
NORM_EPS = 1e-5

LANES = 128
SUBLANES = 8
VMEM_LIMIT = 56 * 1024 * 1024
MOE_VMEM_LIMIT = 60 * 1024 * 1024
N_DMA_PRIORITIES = 2

ADALN_TN = 1024
NORM_TS = 1024
DENSE_TM = 1024
DENSE_TN = 512
DENSE_TN_WIDE = 1024
DENSE_TN_QKV = 1280
COMBINE_TM = 128
ATTN_BLOCKS = 4
SSM_TILE_CH = 128
SSM_TIME_CHUNK = 512
MOE_SUB = 256
MOE_SB_SUBS = 8
MOE_SB_ROWS = MOE_SUB * MOE_SB_SUBS
MOE_GROUP = 4
MOE_CHUNK = 256
MOE_DOWN_CHUNK = 512


def _cparams(n_axes, vmem_limit=VMEM_LIMIT):
    return pltpu.CompilerParams(dimension_semantics=("arbitrary",) * n_axes, vmem_limit_bytes=vmem_limit)


def _mod_kernel(c_ref, w_ref, b_ref, o_ref):
    c = c_ref[...]
    cond = c * jax.nn.sigmoid(c)
    o_ref[...] = jnp.dot(cond.astype(bf16), w_ref[...].astype(bf16), preferred_element_type=f32) + b_ref[...]


def _modulation(c, ada_w, ada_b, layer):
    bsz, d = c.shape
    n = ada_w.shape[-1]
    tn = ADALN_TN
    assert n % tn == 0
    return pl.pallas_call(
        _mod_kernel,
        grid=(n // tn,),
        in_specs=[
            pl.BlockSpec((bsz, d), lambda j: (0, 0)),
            pl.BlockSpec((None, d, tn), lambda j: (layer, 0, j)),
            pl.BlockSpec((None, 1, tn), lambda j: (layer, 0, j)),
        ],
        out_specs=pl.BlockSpec((bsz, tn), lambda j: (0, j)),
        out_shape=jax.ShapeDtypeStruct((bsz, n), f32),
        compiler_params=_cparams(1),
        name="adaln_mod",
    )(c, ada_w, ada_b.reshape(ada_b.shape[0], 1, n))


def _norm_mod(x_ref, gain_ref, sc_ref, sh_ref):
    x = x_ref[0]
    ms = jnp.mean(x * x, axis=-1, keepdims=True)
    y = x * lax.rsqrt(ms + NORM_EPS) * gain_ref[...]
    return y * (1.0 + sc_ref[0]) + sh_ref[0]


def _norm_kernel(x_ref, gain_ref, sc_ref, sh_ref, h_ref):
    h_ref[0] = _norm_mod(x_ref, gain_ref, sc_ref, sh_ref).astype(h_ref.dtype)


def _split_bf16(v):
    hi = v.astype(bf16)
    lo = (v - hi.astype(f32)).astype(bf16)
    return hi, lo


def _norm_router_kernel(x_ref, gain_ref, sc_ref, sh_ref, wr_ref, br_ref, h_ref, idx_ref, gate_ref):
    h = _norm_mod(x_ref, gain_ref, sc_ref, sh_ref)
    ts, d = h.shape
    n_col = d // LANES
    for j in range(n_col):
        h_ref[pl.ds(j, ts, stride=n_col), :] = h[:, j * LANES:(j + 1) * LANES]
    h_hi, h_lo = _split_bf16(h)
    w_hi, w_lo = _split_bf16(wr_ref[...])
    dot = functools.partial(jnp.dot, preferred_element_type=f32)
    logits = dot(h_hi, w_hi) + (dot(h_hi, w_lo) + dot(h_lo, w_hi)) + br_ref[...]
    rows, n_exp = logits.shape
    col = lax.broadcasted_iota(i32, (rows, n_exp), 1)
    work = logits
    tops, idxs = [], []
    for _ in range(TOP_K):
        m = jnp.max(work, axis=-1, keepdims=True)
        idx = jnp.min(jnp.where(work == m, col, n_exp), axis=-1, keepdims=True)
        work = jnp.where(col == idx, -jnp.inf, work)
        tops.append(m)
        idxs.append(idx)
    es = [jnp.exp(t - tops[0]) for t in tops]
    denom = functools.reduce(lambda a, b: a + b, es)
    lane = lax.broadcasted_iota(i32, (rows, LANES), 1)
    idx_out = jnp.zeros((rows, LANES), i32)
    gate_out = jnp.zeros((rows, LANES), f32)
    for k in range(TOP_K):
        idx_out = jnp.where(lane == k, idxs[k], idx_out)
        gate_out = jnp.where(lane == k, es[k] / denom, gate_out)
    idx_ref[0] = idx_out
    gate_ref[0] = gate_out


def _norm(x, gain, mod3, sc_chunk, sh_chunk, out_dtype, router=None):
    bsz, seq, d = x.shape
    ts = NORM_TS
    assert seq % ts == 0 and d % LANES == 0
    in_specs = [
        pl.BlockSpec((1, ts, d), lambda b, s: (b, s, 0)),
        pl.BlockSpec((1, d), lambda b, s: (0, 0)),
        pl.BlockSpec((1, 1, d), lambda b, s: (b, 0, sc_chunk)),
        pl.BlockSpec((1, 1, d), lambda b, s: (b, 0, sh_chunk)),
    ]
    h_spec = pl.BlockSpec((1, ts, d), lambda b, s: (b, s, 0))
    args = [x, gain.reshape(1, d), mod3, mod3]
    if router is None:
        return pl.pallas_call(
            _norm_kernel, grid=(bsz, seq // ts), in_specs=in_specs, out_specs=h_spec,
            out_shape=jax.ShapeDtypeStruct((bsz, seq, d), out_dtype),
            compiler_params=_cparams(2), name="norm_mod",
        )(*args)
    w_router, b_router, layer = router
    n_exp = w_router.shape[-1]
    in_specs += [
        pl.BlockSpec((None, d, n_exp), lambda b, s: (layer, 0, 0)),
        pl.BlockSpec((None, 1, n_exp), lambda b, s: (layer, 0, 0)),
    ]
    lane_spec = pl.BlockSpec((1, ts, LANES), lambda b, s: (b, s, 0))
    n_col = d // LANES
    h3_spec = pl.BlockSpec((ts * n_col, LANES), lambda b, s: (b * (seq // ts) + s, 0))
    return pl.pallas_call(
        _norm_router_kernel, grid=(bsz, seq // ts), in_specs=in_specs,
        out_specs=[h3_spec, lane_spec, lane_spec],
        out_shape=[jax.ShapeDtypeStruct((bsz * seq * n_col, LANES), f32),
                   jax.ShapeDtypeStruct((bsz, seq, LANES), i32),
                   jax.ShapeDtypeStruct((bsz, seq, LANES), f32)],
        compiler_params=_cparams(2), name="norm_router",
    )(*args, w_router, b_router.reshape(b_router.shape[0], 1, n_exp))


def _dense_kernel(*refs, n_w, n_extra, epilogue):
    x_ref = refs[0]
    w_refs = refs[1:1 + n_w]
    b_refs = refs[1 + n_w:1 + 2 * n_w]
    e_refs = refs[1 + 2 * n_w:1 + 2 * n_w + n_extra]
    o_ref = refs[1 + 2 * n_w + n_extra]
    wbf_refs = refs[2 + 2 * n_w + n_extra:]

    @pl.when(pl.program_id(1) == 0)
    def _():
        for w_ref, wbf in zip(w_refs, wbf_refs):
            wbf[...] = w_ref[...].astype(bf16)

    x = x_ref[...]
    accs = [jnp.dot(x, wbf[...], preferred_element_type=f32) + b_ref[...] for wbf, b_ref in zip(wbf_refs, b_refs)]
    o_ref[...] = epilogue(accs, [e[...] for e in e_refs]).astype(o_ref.dtype)


def _dense(x, ws, bs, layer, epilogue, extras, out_dtype, tm, tn, name):
    m, k = x.shape
    n = ws[0].shape[-1]
    assert m % tm == 0 and n % tn == 0
    in_specs = [pl.BlockSpec((tm, k), lambda j, i: (i, 0))]
    in_specs += [pl.BlockSpec((None, k, tn), lambda j, i: (layer, 0, j)) for _ in ws]
    in_specs += [pl.BlockSpec((None, 1, tn), lambda j, i: (layer, 0, j)) for _ in bs]
    in_specs += [spec for _, spec in extras]
    args = [x] + list(ws) + [b.reshape(b.shape[0], 1, n) for b in bs] + [a for a, _ in extras]
    return pl.pallas_call(
        functools.partial(_dense_kernel, n_w=len(ws), n_extra=len(extras), epilogue=epilogue),
        grid=(n // tn, m // tm),
        in_specs=in_specs,
        out_specs=pl.BlockSpec((tm, tn), lambda j, i: (i, j)),
        out_shape=jax.ShapeDtypeStruct((m, n), out_dtype),
        scratch_shapes=[pltpu.VMEM((k, tn), bf16) for _ in ws],
        compiler_params=_cparams(2), name=name,
    )(*args)


def _gated_residual_extras(x2d, mod3, gate_chunk, d, tm, tn, seq):
    tiles_per_batch = seq // tm
    return [
        (x2d, pl.BlockSpec((tm, tn), lambda j, i: (i, j))),
        (mod3, pl.BlockSpec((None, 1, tn), lambda j, i: (i // tiles_per_batch, 0, gate_chunk * (d // tn) + j))),
    ]


def _attn_kernel(sinks_ref, q_ref, kp_ref, kc_ref, vp_ref, vc_ref, qg_ref, kg_ref, bias_ref, o_ref):
    n = pl.program_id(1)
    lane = lax.broadcasted_iota(i32, (1, LANES), 1)
    lo = lane < HEAD_DIM
    bq = kp_ref.shape[1]
    n_blk = q_ref.shape[1] // bq

    def halfnorm(v, gain):
        sq = v * v
        s_lo = jnp.sum(jnp.where(lo, sq, 0.0), axis=-1, keepdims=True)
        s_hi = jnp.sum(jnp.where(lo, 0.0, sq), axis=-1, keepdims=True)
        ms = jnp.where(lo, s_lo, s_hi) * (1.0 / HEAD_DIM)
        return v * lax.rsqrt(ms + NORM_EPS) * gain

    k_all = jnp.concatenate([kp_ref[0], kc_ref[0]], axis=0)
    v_all = jnp.concatenate([vp_ref[0], vc_ref[0]], axis=0)
    kcol = lax.broadcasted_iota(i32, (1, 2 * bq), 1)
    qgain = qg_ref[...]
    kgain = kg_ref[...]
    pairs = Q_PER_KV // 2
    for c in range(N_KV_HEADS // 2):
        kn_all = halfnorm(k_all[:, c * LANES:(c + 1) * LANES], kgain)
        kr_all = pltpu.roll(kn_all, HEAD_DIM, 1)
        vn_all = v_all[:, c * LANES:(c + 1) * LANES]
        vr_all = pltpu.roll(vn_all, HEAD_DIM, 1)
        for blk in range(n_blk):
            rows = slice(blk * bq, (blk + 1) * bq)
            keys = slice(blk * bq, (blk + 2) * bq)
            kn, kr, vn, vr = kn_all[keys], kr_all[keys], vn_all[keys], vr_all[keys]
            key_ok = jnp.logical_or(kcol >= bq, n > 0) if blk == 0 else None
            for half in range(2):
                g = 2 * c + half
                k_src, k_rot = (kn, kr) if half == 0 else (kr, kn)
                v_src, v_rot = (vn, vr) if half == 0 else (vr, vn)
                k_par = [jnp.where(lo, k_src, 0.0).astype(bf16), jnp.where(lo, 0.0, k_rot).astype(bf16)]
                v_par = [jnp.where(lo, v_src, 0.0).astype(bf16), jnp.where(lo, 0.0, v_rot).astype(bf16)]
                qs = [halfnorm(q_ref[0, rows, (g * pairs + p) * LANES:(g * pairs + p + 1) * LANES], qgain)
                      for p in range(pairs)]
                qg = (jnp.concatenate(qs, axis=0) * (1.0 / math.sqrt(HEAD_DIM))).astype(bf16)
                acc = None
                for par in range(2):
                    s = lax.dot_general(qg, k_par[par], (((1,), (1,)), ((), ())), preferred_element_type=f32)
                    s = s + bias_ref[g, par]
                    if key_ok is not None:
                        s = jnp.where(key_ok, s, -jnp.inf)
                    sink = jnp.concatenate(
                        [jnp.full((bq, 1), sinks_ref[g * Q_PER_KV + 2 * p + par], f32) for p in range(pairs)], axis=0)
                    m = jnp.maximum(jnp.max(s, axis=-1, keepdims=True), sink)
                    e = jnp.exp(s - m)
                    den = jnp.sum(e, axis=-1, keepdims=True) + jnp.exp(sink - m)
                    probs = (e * (1.0 / den)).astype(bf16)
                    o = jnp.dot(probs, v_par[par], preferred_element_type=f32)
                    acc = o if acc is None else acc + o
                for p in range(pairs):
                    col = (g * pairs + p) * LANES
                    o_ref[0, rows, col:col + LANES] = acc[p * bq:(p + 1) * bq].astype(o_ref.dtype)


def _t5_bucket(dist):
    nn = np.maximum(dist, 0)
    max_exact = NUM_BUCKETS // 2
    large = max_exact + (np.log(np.maximum(nn, 1) / max_exact) / np.log(MAX_DISTANCE / max_exact)
                         * (NUM_BUCKETS - max_exact)).astype(np.int32)
    large = np.minimum(large, NUM_BUCKETS - 1)
    return np.where(nn < max_exact, nn, large).astype(np.int32)


def _attention(qkv, q_gain, k_gain, sinks, rel_bias):
    bsz, seq, _ = qkv.shape
    bq = WINDOW
    q_dim = N_Q_HEADS * HEAD_DIM
    kv_dim = N_KV_HEADS * HEAD_DIM
    pairs = Q_PER_KV // 2
    ql = np.arange(bq)[:, None]
    kl = np.arange(2 * bq)[None, :]
    dist = ql + bq - kl
    in_window = (dist >= 0) & (dist < WINDOW)
    bucket = jnp.asarray(_t5_bucket(dist).reshape(1, -1))
    onehot_t = (bucket == jnp.arange(NUM_BUCKETS, dtype=i32)[:, None]).astype(f32)
    bias = jnp.dot(rel_bias.astype(f32).T, onehot_t, precision=lax.Precision.HIGHEST).reshape(N_Q_HEADS, bq, 2 * bq)
    bias = jnp.where(jnp.asarray(in_window)[None], bias, -jnp.inf)
    bias = bias.reshape(N_KV_HEADS, pairs, 2, bq, 2 * bq)
    bias = jnp.transpose(bias, (0, 2, 1, 3, 4)).reshape(N_KV_HEADS, 2, pairs * bq, 2 * bq)
    gain2 = lambda gn: jnp.concatenate([gn, gn]).reshape(1, LANES).astype(f32)
    k_blk = q_dim // kv_dim
    nb = ATTN_BLOCKS
    assert seq % (nb * bq) == 0
    grid_spec = pltpu.PrefetchScalarGridSpec(
        num_scalar_prefetch=1,
        grid=(bsz, seq // (nb * bq)),
        in_specs=[
            pl.BlockSpec((1, nb * bq, q_dim), lambda b, n, s: (b, n, 0)),
            pl.BlockSpec((1, bq, kv_dim), lambda b, n, s: (b, jnp.maximum(nb * n - 1, 0), k_blk)),
            pl.BlockSpec((1, nb * bq, kv_dim), lambda b, n, s: (b, n, k_blk)),
            pl.BlockSpec((1, bq, kv_dim), lambda b, n, s: (b, jnp.maximum(nb * n - 1, 0), k_blk + 1)),
            pl.BlockSpec((1, nb * bq, kv_dim), lambda b, n, s: (b, n, k_blk + 1)),
            pl.BlockSpec((1, LANES), lambda b, n, s: (0, 0)),
            pl.BlockSpec((1, LANES), lambda b, n, s: (0, 0)),
            pl.BlockSpec((N_KV_HEADS, 2, pairs * bq, 2 * bq), lambda b, n, s: (0, 0, 0, 0)),
        ],
        out_specs=pl.BlockSpec((1, nb * bq, q_dim), lambda b, n, s: (b, n, 0)),
    )
    return pl.pallas_call(
        _attn_kernel, grid_spec=grid_spec,
        out_shape=jax.ShapeDtypeStruct((bsz, seq, q_dim), bf16),
        compiler_params=_cparams(2), name="swa_attention",
    )(sinks.astype(f32), qkv, qkv, qkv, qkv, qkv, gain2(q_gain), gain2(k_gain), bias)


def _gelu_tanh(y):
    return 0.5 * y * (1.0 + jnp.tanh(math.sqrt(2.0 / math.pi) * (y + 0.044715 * (y * y * y))))


def _ssm_kernel(u_ref, bm_ref, cm_ref, are_ref, aim_ref, d_ref, y_ref, us, xs, st, ys):
    bsz, ts, _ = u_ref.shape
    n_state = are_ref.shape[-1]

    @pl.when(pl.program_id(1) == 0)
    def _():
        st[...] = jnp.zeros_like(st)

    n_slab = us.shape[0]
    for h in range(n_slab):
        for b in range(bsz):
            us[h, pl.ds(b, ts, stride=bsz), :] = u_ref[b, :, h * LANES:(h + 1) * LANES]
    u = jnp.concatenate([us[h] for h in range(n_slab)], axis=1)
    xs[...] = jnp.dot(u.astype(bf16), bm_ref[0], preferred_element_type=f32)
    a_re = jnp.broadcast_to(are_ref[0], (bsz, n_state))
    a_im = jnp.broadcast_to(aim_ref[0], (bsz, n_state))

    def step(t, carry):
        s_re, s_im = carry
        r0 = pl.multiple_of(t * bsz, bsz)
        x_re = xs[pl.ds(r0, bsz), 0:n_state]
        x_im = xs[pl.ds(r0, bsz), n_state:2 * n_state]
        n_re = a_re * s_re - a_im * s_im + x_re
        n_im = a_re * s_im + a_im * s_re + x_im
        xs[pl.ds(r0, bsz), 0:n_state] = n_re
        xs[pl.ds(r0, bsz), n_state:2 * n_state] = n_im
        return n_re, n_im

    s_re, s_im = lax.fori_loop(0, ts, step, (st[0], st[1]), unroll=8)
    st[0] = s_re
    st[1] = s_im
    y = (jnp.dot(xs[:, 0:n_state].astype(bf16), cm_ref[0, 0:n_state, :], preferred_element_type=f32)
         + jnp.dot(xs[:, n_state:2 * n_state].astype(bf16), cm_ref[0, n_state:2 * n_state, :],
                   preferred_element_type=f32)
         + d_ref[0] * u)
    y = _gelu_tanh(y)
    for h in range(n_slab):
        ys[h] = y[:, h * LANES:(h + 1) * LANES]
        for b in range(bsz):
            y_ref[b, :, h * LANES:(h + 1) * LANES] = ys[h, pl.ds(b, ts, stride=bsz), :].astype(y_ref.dtype)


def _ssm(h, lam_re, lam_im, log_dt, b_re, b_im, c_re, c_im, d_skip):
    bsz, seq, width = h.shape
    assert bsz == SUBLANES
    n_grp, n_st = lam_re.shape
    tile_ch = SSM_TILE_CH
    gpt = tile_ch // SSM_GROUP_CH
    n_tiles = width // tile_ch
    ns = gpt * n_st
    dt = jnp.exp(log_dt.astype(f32))[:, None]
    lr, li = lam_re.astype(f32), lam_im.astype(f32)
    mag = jnp.exp(lr * dt)
    ab_re, ab_im = mag * jnp.cos(li * dt), mag * jnp.sin(li * dt)
    den = lr * lr + li * li
    nr, ni = ab_re - 1.0, ab_im
    f_re = (nr * lr + ni * li) / den
    f_im = (ni * lr - nr * li) / den
    br, bi = b_re.astype(f32), b_im.astype(f32)
    bb_re = f_re[..., None] * br - f_im[..., None] * bi
    bb_im = f_re[..., None] * bi + f_im[..., None] * br
    eye = jnp.eye(gpt, dtype=f32)

    def blockdiag_in(bb):
        t = bb.reshape(n_tiles, gpt, n_st, SSM_GROUP_CH)
        return jnp.einsum("tgpc,gh->tgchp", t, eye).reshape(n_tiles, tile_ch, ns)

    def blockdiag_out(cc):
        t = cc.reshape(n_tiles, gpt, SSM_GROUP_CH, n_st)
        return jnp.einsum("tgcp,gh->tgphc", t, eye).reshape(n_tiles, ns, tile_ch)

    bm = jnp.concatenate([blockdiag_in(bb_re), blockdiag_in(bb_im)], axis=-1).astype(bf16)
    cm = jnp.concatenate([blockdiag_out(c_re.astype(f32)), -blockdiag_out(c_im.astype(f32))], axis=1).astype(bf16)
    a_re = ab_re.reshape(n_tiles, 1, ns)
    a_im = ab_im.reshape(n_tiles, 1, ns)
    dsk = d_skip.astype(f32).reshape(n_tiles, 1, tile_ch)
    ts = SSM_TIME_CHUNK
    return pl.pallas_call(
        _ssm_kernel,
        grid=(n_tiles, seq // ts),
        in_specs=[
            pl.BlockSpec((bsz, ts, tile_ch), lambda g, t: (0, t, g)),
            pl.BlockSpec((1, tile_ch, 2 * ns), lambda g, t: (g, 0, 0)),
            pl.BlockSpec((1, 2 * ns, tile_ch), lambda g, t: (g, 0, 0)),
            pl.BlockSpec((1, 1, ns), lambda g, t: (g, 0, 0)),
            pl.BlockSpec((1, 1, ns), lambda g, t: (g, 0, 0)),
            pl.BlockSpec((1, 1, tile_ch), lambda g, t: (g, 0, 0)),
        ],
        out_specs=pl.BlockSpec((bsz, ts, tile_ch), lambda g, t: (0, t, g)),
        out_shape=jax.ShapeDtypeStruct((bsz, seq, width), bf16),
        scratch_shapes=[
            pltpu.VMEM((tile_ch // LANES, ts * bsz, LANES), f32),
            pltpu.VMEM((ts * bsz, 2 * ns), f32),
            pltpu.VMEM((2, bsz, ns), f32),
            pltpu.VMEM((tile_ch // LANES, ts * bsz, LANES), f32),
        ],
        compiler_params=_cparams(2), name="s5_ssm",
    )(h, bm, cm, a_re, a_im, dsk)


GATHER_PITCH = 24


def _swiglu(g, l):
    glu = jnp.minimum(g, SWIGLU_LIMIT)
    lin = jnp.clip(l, -SWIGLU_LIMIT, SWIGLU_LIMIT)
    return glu * jax.nn.sigmoid(SWIGLU_ALPHA * glu) * (lin + 1.0)


def _for_row_groups(nsub, cast, fn):
    n_big = nsub // MOE_GROUP
    big_rows = MOE_GROUP * MOE_SUB

    @pl.when(n_big > 0)
    def _():
        fn(0, big_rows, cast())

    @pl.when(n_big == 0)
    def _():
        fn(0, MOE_SUB, cast())

    def big(i, carry):
        fn(pl.multiple_of(i * big_rows, big_rows), big_rows, None)
        return carry

    def small(i, carry):
        fn(pl.multiple_of(i * MOE_SUB, MOE_SUB), MOE_SUB, None)
        return carry

    lax.fori_loop(1, n_big, big, 0)
    lax.fori_loop(jnp.maximum(n_big * MOE_GROUP, 1), nsub, small, 0)


def _moe_kernel(sbe_ref, sbn_ref, idx0_ref, idxn_ref, h_hbm, wg_ref, wl_ref, bgu_ref, wd_ref, bd_ref, o_ref,
                xbf, stg, sem, a_scr, wg_bf, wl_bf, wd_bf, *, n_chunks):
    s = pl.program_id(0)
    c = pl.program_id(1)
    n_sb = pl.num_programs(0)
    nsub = sbn_ref[s]
    slot = lax.rem(s, 2)
    nsub_next = jnp.where(s + 1 < n_sb, sbn_ref[jnp.minimum(s + 1, n_sb - 1)], 0)
    n_col = xbf.shape[2] // LANES
    stg_rows = MOE_SUB * GATHER_PITCH

    def issue(idx_ref, off, st_slot):
        def body(r2, carry):
            for prio in range(N_DMA_PRIORITIES):
                r = r2 * N_DMA_PRIORITIES + prio
                src = pl.multiple_of(idx_ref[0, 0, off + r] * n_col, SUBLANES)
                dst = pl.multiple_of(st_slot * stg_rows + r * GATHER_PITCH, SUBLANES)
                pltpu.make_async_copy(h_hbm.at[pl.ds(src, n_col)], stg.at[pl.ds(dst, n_col)],
                                      sem.at[st_slot]).start(priority=prio)
            return carry

        lax.fori_loop(0, MOE_SUB // N_DMA_PRIORITIES, body, 0, unroll=4)

    def finish(st_slot, x_slot, sub):
        base = pl.multiple_of(st_slot * stg_rows, SUBLANES)
        pltpu.make_async_copy(h_hbm.at[pl.ds(0, MOE_SUB * n_col)], stg.at[pl.ds(base, MOE_SUB * n_col)],
                              sem.at[st_slot]).wait()
        r0 = pl.multiple_of(sub * MOE_SUB, MOE_SUB)
        for j in range(n_col):
            piece = stg[pl.ds(base + j, MOE_SUB, stride=GATHER_PITCH), :]
            xbf[x_slot, pl.ds(r0, MOE_SUB), j * LANES:(j + 1) * LANES] = piece.astype(bf16)

    @pl.when(jnp.logical_and(s == 0, c == 0))
    def _():
        def first(j, carry):
            issue(idx0_ref, j * MOE_SUB, 0)
            finish(0, 0, j)
            return carry

        lax.fori_loop(0, nsub, first, 0)

    @pl.when(jnp.logical_and(c >= 1, c - 1 < nsub_next))
    def _():
        finish(lax.rem(c - 1, 2), 1 - slot, c - 1)

    @pl.when(c < nsub_next)
    def _():
        issue(idxn_ref, c * MOE_SUB, lax.rem(c, 2))

    @pl.when(jnp.logical_and(c < n_chunks, nsub > 0))
    def _():
        def cast():
            wg, wl = wg_ref[...].astype(bf16), wl_ref[...].astype(bf16)
            wg_bf[...] = wg
            wl_bf[...] = wl
            return wg, wl

        def up(r, rows, weights):
            wg, wl = weights if weights is not None else (wg_bf[...], wl_bf[...])
            xs = xbf[slot, pl.ds(r, rows), :]
            g = jnp.dot(xs, wg, preferred_element_type=f32) + bgu_ref[pl.ds(c, 1), :]
            l = jnp.dot(xs, wl, preferred_element_type=f32) + bgu_ref[pl.ds(n_chunks + c, 1), :]
            a_scr[c, pl.ds(r, rows), :] = _swiglu(g, l).astype(bf16)

        _for_row_groups(nsub, cast, up)

    @pl.when(jnp.logical_and(c >= n_chunks, nsub > 0))
    def _():
        def cast():
            wd = wd_ref[...].astype(bf16)
            wd_bf[...] = wd
            return wd

        def down(r, rows, weights):
            wd = weights if weights is not None else wd_bf[...]
            a = jnp.concatenate([a_scr[cc, pl.ds(r, rows), :] for cc in range(n_chunks)], axis=1)
            o_ref[pl.ds(r, rows), :] = jnp.dot(a, wd, preferred_element_type=f32) + bd_ref[pl.ds(c - n_chunks, 1), :]

        _for_row_groups(nsub, cast, down)

    @pl.when(c >= n_chunks)
    def _():
        def zero(i, carry):
            r = pl.multiple_of(i * MOE_SUB, MOE_SUB)
            o_ref[pl.ds(r, MOE_SUB), :] = jnp.zeros((MOE_SUB, o_ref.shape[1]), f32)
            return carry

        lax.fori_loop(nsub, MOE_SB_SUBS, zero, 0)


def _moe_experts(h_rows, n_col, row_tok, sb_expert, sb_nsub, w_gate_up, b_gate_up, w_down, b_down, layer):
    n_rows = row_tok.shape[0]
    d = n_col * LANES
    n_sb = n_rows // MOE_SB_ROWS
    d_exp = w_down.shape[2]
    n_chunks = d_exp // MOE_CHUNK
    n_out_chunks = d // MOE_DOWN_CHUNK
    last = n_chunks - 1
    last_out = n_out_chunks - 1

    def valid(s, n):
        return n[s] > 0

    def wg_map(s, c, e, n):
        return (layer, e[s], 0, jnp.where(valid(s, n), jnp.minimum(c, last), last))

    def wl_map(s, c, e, n):
        return (layer, e[s], 0, n_chunks + jnp.where(valid(s, n), jnp.minimum(c, last), last))

    def wd_map(s, c, e, n):
        return (layer, e[s], 0, jnp.where(valid(s, n), jnp.maximum(c - n_chunks, 0), last_out))

    def out_map(s, c, e, n):
        return (s, jnp.maximum(c - n_chunks, 0))

    def next_idx_map(s, c, e, n):
        return (jnp.minimum(s + 1, n_sb - 1), 0, 0)

    def bias_map(s, c, e, n):
        return (layer, e[s], 0, 0)

    grid_spec = pltpu.PrefetchScalarGridSpec(
        num_scalar_prefetch=2,
        grid=(n_sb, n_chunks + n_out_chunks),
        in_specs=[
            pl.BlockSpec((1, 1, MOE_SB_ROWS), lambda s, c, e, n: (0, 0, 0), memory_space=pltpu.SMEM),
            pl.BlockSpec((1, 1, MOE_SB_ROWS), next_idx_map, memory_space=pltpu.SMEM),
            pl.BlockSpec(memory_space=pl.ANY),
            pl.BlockSpec((None, None, d, MOE_CHUNK), wg_map),
            pl.BlockSpec((None, None, d, MOE_CHUNK), wl_map),
            pl.BlockSpec((None, None, 2 * n_chunks, MOE_CHUNK), bias_map),
            pl.BlockSpec((None, None, d_exp, MOE_DOWN_CHUNK), wd_map),
            pl.BlockSpec((None, None, n_out_chunks, MOE_DOWN_CHUNK), bias_map),
        ],
        out_specs=pl.BlockSpec((MOE_SB_ROWS, MOE_DOWN_CHUNK), out_map),
        scratch_shapes=[
            pltpu.VMEM((2, MOE_SB_ROWS, d), bf16),
            pltpu.VMEM((2 * MOE_SUB * GATHER_PITCH, LANES), f32),
            pltpu.SemaphoreType.DMA((2,)),
            pltpu.VMEM((n_chunks, MOE_SB_ROWS, MOE_CHUNK), bf16),
            pltpu.VMEM((d, MOE_CHUNK), bf16),
            pltpu.VMEM((d, MOE_CHUNK), bf16),
            pltpu.VMEM((d_exp, MOE_DOWN_CHUNK), bf16),
        ],
    )
    n_l, n_e = b_gate_up.shape[:2]
    return pl.pallas_call(
        functools.partial(_moe_kernel, n_chunks=n_chunks), grid_spec=grid_spec,
        out_shape=jax.ShapeDtypeStruct((n_rows, d), f32),
        compiler_params=_cparams(2, MOE_VMEM_LIMIT), name="moe_experts",
    )(sb_expert, sb_nsub, row_tok.reshape(n_sb, 1, MOE_SB_ROWS), row_tok.reshape(n_sb, 1, MOE_SB_ROWS),
      h_rows, w_gate_up, w_gate_up,
      b_gate_up.reshape(n_l, n_e, 2 * n_chunks, MOE_CHUNK),
      w_down, b_down.reshape(n_l, n_e, n_out_chunks, MOE_DOWN_CHUNK))


def _combine_kernel(dest_ref, dest_next_ref, y_hbm, gate_ref, x_ref, g2_ref, o_ref, ybuf, sem):
    i = pl.program_id(0)
    n = pl.num_programs(0)
    tm = x_ref.shape[0]
    slot = lax.rem(i, 2)

    def issue(dest_blk, sl):
        for k in range(TOP_K):
            def body(t2, carry, k=k):
                for prio in range(N_DMA_PRIORITIES):
                    t = t2 * N_DMA_PRIORITIES + prio
                    pltpu.make_async_copy(y_hbm.at[pl.ds(dest_blk[0, 0, k * tm + t], 1)],
                                          ybuf.at[sl * TOP_K + k, pl.ds(t, 1)], sem.at[sl]).start(priority=prio)
                return carry

            lax.fori_loop(0, tm // N_DMA_PRIORITIES, body, 0, unroll=4)

    @pl.when(i == 0)
    def _():
        issue(dest_ref, 0)

    @pl.when(i + 1 < n)
    def _():
        issue(dest_next_ref, 1 - slot)

    for k in range(TOP_K):
        pltpu.make_async_copy(y_hbm.at[pl.ds(0, tm)], ybuf.at[slot * TOP_K + k], sem.at[slot]).wait()
    gates = gate_ref[...]
    acc = gates[:, 0:1] * ybuf[slot * TOP_K]
    for k in range(1, TOP_K):
        acc = acc + gates[:, k:k + 1] * ybuf[slot * TOP_K + k]
    o_ref[...] = x_ref[...] + g2_ref[...] * acc


def _moe_combine(y_sorted, dest, gates, x2d, mod3, gate_chunk, seq):
    n_tok, d = x2d.shape
    tm = COMBINE_TM
    assert seq % tm == 0
    n_tiles = n_tok // tm
    dest_km = jnp.transpose(dest.reshape(TOP_K, n_tiles, tm), (1, 0, 2)).reshape(n_tiles, 1, TOP_K * tm)
    tiles_per_batch = seq // tm
    return pl.pallas_call(
        _combine_kernel,
        grid=(n_tiles,),
        in_specs=[
            pl.BlockSpec((1, 1, TOP_K * tm), lambda i: (i, 0, 0), memory_space=pltpu.SMEM),
            pl.BlockSpec((1, 1, TOP_K * tm), lambda i: (jnp.minimum(i + 1, n_tiles - 1), 0, 0),
                         memory_space=pltpu.SMEM),
            pl.BlockSpec(memory_space=pl.ANY),
            pl.BlockSpec((tm, LANES), lambda i: (i, 0)),
            pl.BlockSpec((tm, d), lambda i: (i, 0)),
            pl.BlockSpec((None, 1, d), lambda i: (i // tiles_per_batch, 0, gate_chunk)),
        ],
        out_specs=pl.BlockSpec((tm, d), lambda i: (i, 0)),
        out_shape=jax.ShapeDtypeStruct((n_tok, d), f32),
        scratch_shapes=[pltpu.VMEM((2 * TOP_K, tm, d), f32), pltpu.SemaphoreType.DMA((2,))],
        compiler_params=_cparams(1), name="moe_combine",
    )(dest_km, dest_km, y_sorted, gates, x2d, mod3)


def _routing_tables(idx_km):
    n_tok = idx_km.shape[1]
    n_assign = n_tok * TOP_K
    n_sb_max = (n_assign // MOE_SUB + N_EXPERTS + N_EXPERTS * (MOE_SB_SUBS - 1)) // MOE_SB_SUBS
    experts = jnp.arange(N_EXPERTS, dtype=i32)
    onehot = idx_km[:, :, None] == experts[None, None, :]
    per_tok = jnp.sum(onehot.astype(i32), axis=0)
    before = jnp.cumsum(per_tok, axis=0) - per_tok
    rank = jnp.sum(jnp.where(onehot, before[None], 0), axis=2)
    counts = before[-1] + per_tok[-1]
    n_sub_e = (counts + MOE_SUB - 1) // MOE_SUB
    n_sb_e = (n_sub_e + MOE_SB_SUBS - 1) // MOE_SB_SUBS
    sb_end = jnp.cumsum(n_sb_e)
    sb_start = sb_end - n_sb_e
    base = n_sub_e // jnp.maximum(n_sb_e, 1)
    rem = n_sub_e - base * n_sb_e
    tables = jnp.stack([sb_start, base, rem], axis=1).astype(bf16)
    looked = jnp.dot(onehot.astype(bf16).reshape(n_assign, N_EXPERTS), tables,
                     preferred_element_type=f32).astype(i32).reshape(TOP_K, n_tok, 3)
    a_start, a_base, a_rem = looked[..., 0], looked[..., 1], looked[..., 2]
    assert MOE_SUB & (MOE_SUB - 1) == 0
    q = lax.shift_right_logical(rank, int(math.log2(MOE_SUB)))
    max_sb_per_expert = -(-(-(-n_tok // MOE_SUB)) // MOE_SB_SUBS)
    sb_local = jnp.zeros_like(q)
    for j in range(1, max_sb_per_expert):
        sb_local = sb_local + (q >= j * a_base + jnp.minimum(j, a_rem)).astype(i32)
    sub_in = q - (sb_local * a_base + jnp.minimum(sb_local, a_rem))
    dest = ((a_start + sb_local) * MOE_SB_ROWS + sub_in * MOE_SUB + (rank & (MOE_SUB - 1))).astype(i32)
    dest = lax.optimization_barrier(dest)
    tok = jnp.broadcast_to(jnp.arange(n_tok, dtype=i32)[None, :], (TOP_K, n_tok))
    row_tok = jnp.zeros((n_sb_max * MOE_SB_ROWS,), i32).at[dest.reshape(-1)].set(
        tok.reshape(-1), unique_indices=True, mode="promise_in_bounds")
    s = jnp.arange(n_sb_max, dtype=i32)
    n_sb_total = sb_end[-1]
    sb_valid = s < n_sb_total
    sb_src = jnp.where(sb_valid, s, n_sb_total - 1).astype(i32)
    sb_e = jnp.minimum(jnp.searchsorted(sb_end, sb_src, side="right"), N_EXPERTS - 1).astype(i32)
    sb_local_s = sb_src - sb_start[sb_e]
    sb_nsub = jnp.where(sb_valid, base[sb_e] + (sb_local_s < rem[sb_e]).astype(i32), 0).astype(i32)
    return dest, row_tok, sb_e, sb_nsub


def kernel(x, c, rel_bias, norm_gain, ada_w, ada_b, attn_w_qkv, attn_b_qkv, attn_q_gain, attn_k_gain, attn_sinks, attn_w_o, attn_b_o, ssm_lam_re, ssm_lam_im, ssm_log_dt, ssm_b_re, ssm_b_im, ssm_c_re, ssm_c_im, ssm_d, ssm_w_glu_a, ssm_b_glu_a, ssm_w_glu_b, ssm_b_glu_b, moe_w_router, moe_b_router, moe_w_gate_up, moe_b_gate_up, moe_w_down, moe_b_down):
    bsz, seq, d = x.shape
    n_tok = bsz * seq
    depth = norm_gain.shape[0]
    tm, tn = DENSE_TM, DENSE_TN
    assert seq % tm == 0 and d % tn == 0
    SH1, SC1, G1, SH2, SC2, G2 = range(6)

    def gated(accs, extras):
        x_res, gate = extras
        return x_res + gate * accs[0]

    def gated_glu(accs, extras):
        x_res, gate = extras
        return x_res + gate * (accs[0] * jax.nn.sigmoid(accs[1]))

    for layer in range(depth):
        mod3 = _modulation(c, ada_w, ada_b, layer).reshape(bsz, 1, 6 * d)
        x2d = x.reshape(n_tok, d)
        res_extras = _gated_residual_extras(x2d, mod3, G1, d, tm, tn, seq)
        i = layer // 2
        if layer % 2 == 0:
            h = _norm(x, norm_gain[layer, 0], mod3, SC1, SH1, bf16)
            qkv = _dense(h.reshape(n_tok, d), [attn_w_qkv], [attn_b_qkv], i, lambda accs, extras: accs[0], [],
                         f32, tm, DENSE_TN_QKV, "qkv_proj")
            o = _attention(qkv.reshape(bsz, seq, -1), attn_q_gain[i], attn_k_gain[i], attn_sinks[i], rel_bias)
            x2d = _dense(o.reshape(n_tok, -1), [attn_w_o], [attn_b_o], i, gated,
                         _gated_residual_extras(x2d, mod3, G1, d, tm, DENSE_TN_WIDE, seq), f32, tm, DENSE_TN_WIDE,
                         "attn_out")
        else:
            h = _norm(x, norm_gain[layer, 0], mod3, SC1, SH1, f32)
            y = _ssm(h, ssm_lam_re[i], ssm_lam_im[i], ssm_log_dt[i], ssm_b_re[i], ssm_b_im[i],
                     ssm_c_re[i], ssm_c_im[i], ssm_d[i])
            x2d = _dense(y.reshape(n_tok, d), [ssm_w_glu_a, ssm_w_glu_b], [ssm_b_glu_a, ssm_b_glu_b], i,
                         gated_glu, res_extras, f32, tm, tn, "ssm_glu")
        x = x2d.reshape(bsz, seq, d)
        h, top_idx, gates = _norm(x, norm_gain[layer, 1], mod3, SC2, SH2, f32,
                                  router=(moe_w_router, moe_b_router, layer))
        idx_km = top_idx.reshape(n_tok, LANES)[:, :TOP_K].T
        dest, row_tok, sb_e, sb_nsub = _routing_tables(idx_km)
        y_sorted = _moe_experts(h, d // LANES, row_tok, sb_e, sb_nsub, moe_w_gate_up, moe_b_gate_up,
                                moe_w_down, moe_b_down, layer)
        x = _moe_combine(y_sorted, dest, gates.reshape(n_tok, LANES), x2d, mod3, G2, seq).reshape(bsz, seq, d)
    return x
```

```python
import functools
import math

import numpy as np
import jax
import jax.numpy as jnp
from jax import lax
from jax.experimental import pallas as pl
from jax.experimental.pallas import tpu as pltpu

f32 = jnp.float32
bf16 = jnp.bfloat16
i32 = jnp.int32

N_Q_HEADS = 32
N_KV_HEADS = 4
HEAD_DIM = 64
Q_PER_KV = N_Q_HEADS // N_KV_HEADS
WINDOW = 128
NUM_BUCKETS = 32
MAX_DISTANCE = 128
SSM_GROUP_CH = 16
N_EXPERTS = 32
TOP_K = 4
SWIGLU_ALPHA = 1.702
SWIGLU_LIMIT = 7.0
NORM_EPS = 1e-5

LANES = 128
SUBLANES = 8
VMEM_LIMIT = 56 * 1024 * 1024
MOE_VMEM_LIMIT = 60 * 1024 * 1024
N_DMA_PRIORITIES = 2

ADALN_TN = 1024
NORM_TS = 1024
DENSE_TM = 1024
DENSE_TN = 512
DENSE_TN_WIDE = 1024
DENSE_TN_QKV = 1280
COMBINE_TM = 256
ATTN_BLOCKS = 4
SSM_TILE_CH = 128
SSM_TIME_CHUNK = 512
MOE_SUB = 256
MOE_SB_SUBS = 8
MOE_SB_ROWS = MOE_SUB * MOE_SB_SUBS
MOE_GROUP = 4
MOE_CHUNK = 256
MOE_DOWN_CHUNK = 512


def _cparams(n_axes, vmem_limit=VMEM_LIMIT):
    return pltpu.CompilerParams(dimension_semantics=("arbitrary",) * n_axes, vmem_limit_bytes=vmem_limit)


def _mod_kernel(c_ref, w_ref, b_ref, o_ref):
    c = c_ref[...]
    cond = c * jax.nn.sigmoid(c)
    o_ref[...] = jnp.dot(cond.astype(bf16), w_ref[...].astype(bf16), preferred_element_type=f32) + b_ref[...]


def _modulation(c, ada_w, ada_b, layer):
    bsz, d = c.shape
    n = ada_w.shape[-1]
    tn = ADALN_TN
    assert n % tn == 0
    return pl.pallas_call(
        _mod_kernel,
        grid=(n // tn,),
        in_specs=[
            pl.BlockSpec((bsz, d), lambda j: (0, 0)),
            pl.BlockSpec((None, d, tn), lambda j: (layer, 0, j)),
            pl.BlockSpec((None, 1, tn), lambda j: (layer, 0, j)),
        ],
        out_specs=pl.BlockSpec((bsz, tn), lambda j: (0, j)),
        out_shape=jax.ShapeDtypeStruct((bsz, n), f32),
        compiler_params=_cparams(1),
        name="adaln_mod",
    )(c, ada_w, ada_b.reshape(ada_b.shape[0], 1, n))


def _norm_mod(x_ref, gain_ref, sc_ref, sh_ref):
    x = x_ref[0]
    ms = jnp.mean(x * x, axis=-1, keepdims=True)
    y = x * lax.rsqrt(ms + NORM_EPS) * gain_ref[...]
    return y * (1.0 + sc_ref[0]) + sh_ref[0]


def _norm_kernel(x_ref, gain_ref, sc_ref, sh_ref, h_ref):
    h_ref[0] = _norm_mod(x_ref, gain_ref, sc_ref, sh_ref).astype(h_ref.dtype)


def _split_bf16(v):
    hi = v.astype(bf16)
    lo = (v - hi.astype(f32)).astype(bf16)
    return hi, lo


def _norm_router_kernel(x_ref, gain_ref, sc_ref, sh_ref, wr_ref, br_ref, h_ref, idx_ref, gate_ref):
    h = _norm_mod(x_ref, gain_ref, sc_ref, sh_ref)
    ts, d = h.shape
    n_col = d // LANES
    for j in range(n_col):
        h_ref[pl.ds(j, ts, stride=n_col), :] = h[:, j * LANES:(j + 1) * LANES]
    h_hi, h_lo = _split_bf16(h)
    w_hi, w_lo = _split_bf16(wr_ref[...])
    dot = functools.partial(jnp.dot, preferred_element_type=f32)
    logits = dot(h_hi, w_hi) + (dot(h_hi, w_lo) + dot(h_lo, w_hi)) + br_ref[...]
    rows, n_exp = logits.shape
    col = lax.broadcasted_iota(i32, (rows, n_exp), 1)
    work = logits
    tops, idxs = [], []
    for _ in range(TOP_K):
        m = jnp.max(work, axis=-1, keepdims=True)
        idx = jnp.min(jnp.where(work == m, col, n_exp), axis=-1, keepdims=True)
        work = jnp.where(col == idx, -jnp.inf, work)
        tops.append(m)
        idxs.append(idx)
    es = [jnp.exp(t - tops[0]) for t in tops]
    denom = functools.reduce(lambda a, b: a + b, es)
    lane = lax.broadcasted_iota(i32, (rows, LANES), 1)
    idx_out = jnp.zeros((rows, LANES), i32)
    gate_out = jnp.zeros((rows, LANES), f32)
    for k in range(TOP_K):
        idx_out = jnp.where(lane == k, idxs[k], idx_out)
        gate_out = jnp.where(lane == k, es[k] / denom, gate_out)
    idx_ref[0] = idx_out
    gate_ref[0] = gate_out


def _norm(x, gain, mod3, sc_chunk, sh_chunk, out_dtype, router=None):
    bsz, seq, d = x.shape
    ts = NORM_TS
    assert seq % ts == 0 and d % LANES == 0
    in_specs = [
        pl.BlockSpec((1, ts, d), lambda b, s: (b, s, 0)),
        pl.BlockSpec((1, d), lambda b, s: (0, 0)),
        pl.BlockSpec((1, 1, d), lambda b, s: (b, 0, sc_chunk)),
        pl.BlockSpec((1, 1, d), lambda b, s: (b, 0, sh_chunk)),
    ]
    h_spec = pl.BlockSpec((1, ts, d), lambda b, s: (b, s, 0))
    args = [x, gain.reshape(1, d), mod3, mod3]
    if router is None:
        return pl.pallas_call(
            _norm_kernel, grid=(bsz, seq // ts), in_specs=in_specs, out_specs=h_spec,
            out_shape=jax.ShapeDtypeStruct((bsz, seq, d), out_dtype),
            compiler_params=_cparams(2), name="norm_mod",
        )(*args)
    w_router, b_router, layer = router
    n_exp = w_router.shape[-1]
    in_specs += [
        pl.BlockSpec((None, d, n_exp), lambda b, s: (layer, 0, 0)),
        pl.BlockSpec((None, 1, n_exp), lambda b, s: (layer, 0, 0)),
    ]
    lane_spec = pl.BlockSpec((1, ts, LANES), lambda b, s: (b, s, 0))
    n_col = d // LANES
    h3_spec = pl.BlockSpec((ts * n_col, LANES), lambda b, s: (b * (seq // ts) + s, 0))
    return pl.pallas_call(
        _norm_router_kernel, grid=(bsz, seq // ts), in_specs=in_specs,
        out_specs=[h3_spec, lane_spec, lane_spec],
        out_shape=[jax.ShapeDtypeStruct((bsz * seq * n_col, LANES), f32),
                   jax.ShapeDtypeStruct((bsz, seq, LANES), i32),
                   jax.ShapeDtypeStruct((bsz, seq, LANES), f32)],
        compiler_params=_cparams(2), name="norm_router",
    )(*args, w_router, b_router.reshape(b_router.shape[0], 1, n_exp))


def _dense_kernel(*refs, n_w, n_extra, epilogue):
    x_ref = refs[0]
    w_refs = refs[1:1 + n_w]
    b_refs = refs[1 + n_w:1 + 2 * n_w]
    e_refs = refs[1 + 2 * n_w:1 + 2 * n_w + n_extra]
    o_ref = refs[1 + 2 * n_w + n_extra]
    wbf_refs = refs[2 + 2 * n_w + n_extra:]

    @pl.when(pl.program_id(1) == 0)
    def _():
        for w_ref, wbf in zip(w_refs, wbf_refs):
            wbf[...] = w_ref[...].astype(bf16)

    x = x_ref[...]
    accs = [jnp.dot(x, wbf[...], preferred_element_type=f32) + b_ref[...] for wbf, b_ref in zip(wbf_refs, b_refs)]
    o_ref[...] = epilogue(accs, [e[...] for e in e_refs]).astype(o_ref.dtype)


def _dense(x, ws, bs, layer, epilogue, extras, out_dtype, tm, tn, name):
    m, k = x.shape
    n = ws[0].shape[-1]
    assert m % tm == 0 and n % tn == 0
    in_specs = [pl.BlockSpec((tm, k), lambda j, i: (i, 0))]
    in_specs += [pl.BlockSpec((None, k, tn), lambda j, i: (layer, 0, j)) for _ in ws]
    in_specs += [pl.BlockSpec((None, 1, tn), lambda j, i: (layer, 0, j)) for _ in bs]
    in_specs += [spec for _, spec in extras]
    args = [x] + list(ws) + [b.reshape(b.shape[0], 1, n) for b in bs] + [a for a, _ in extras]
    return pl.pallas_call(
        functools.partial(_dense_kernel, n_w=len(ws), n_extra=len(extras), epilogue=epilogue),
        grid=(n // tn, m // tm),
        in_specs=in_specs,
        out_specs=pl.BlockSpec((tm, tn), lambda j, i: (i, j)),
        out_shape=jax.ShapeDtypeStruct((m, n), out_dtype),
        scratch_shapes=[pltpu.VMEM((k, tn), bf16) for _ in ws],
        compiler_params=_cparams(2), name=name,
    )(*args)


def _gated_residual_extras(x2d, mod3, gate_chunk, d, tm, tn, seq):
    tiles_per_batch = seq // tm
    return [
        (x2d, pl.BlockSpec((tm, tn), lambda j, i: (i, j))),
        (mod3, pl.BlockSpec((None, 1, tn), lambda j, i: (i // tiles_per_batch, 0, gate_chunk * (d // tn) + j))),
    ]


def _attn_kernel(sinks_ref, q_ref, kp_ref, kc_ref, vp_ref, vc_ref, qg_ref, kg_ref, bias_ref, o_ref):
    n = pl.program_id(1)
    lane = lax.broadcasted_iota(i32, (1, LANES), 1)
    lo = lane < HEAD_DIM
    bq = kp_ref.shape[1]
    n_blk = q_ref.shape[1] // bq

    def halfnorm(v, gain):
        sq = v * v
        s_lo = jnp.sum(jnp.where(lo, sq, 0.0), axis=-1, keepdims=True)
        s_hi = jnp.sum(jnp.where(lo, 0.0, sq), axis=-1, keepdims=True)
        ms = jnp.where(lo, s_lo, s_hi) * (1.0 / HEAD_DIM)
        return v * lax.rsqrt(ms + NORM_EPS) * gain

    k_all = jnp.concatenate([kp_ref[0], kc_ref[0]], axis=0)
    v_all = jnp.concatenate([vp_ref[0], vc_ref[0]], axis=0)
    kcol = lax.broadcasted_iota(i32, (1, 2 * bq), 1)
    qgain = qg_ref[...]
    kgain = kg_ref[...]
    pairs = Q_PER_KV // 2
    for c in range(N_KV_HEADS // 2):
        kn_all = halfnorm(k_all[:, c * LANES:(c + 1) * LANES], kgain)
        kr_all = pltpu.roll(kn_all, HEAD_DIM, 1)
        vn_all = v_all[:, c * LANES:(c + 1) * LANES]
        vr_all = pltpu.roll(vn_all, HEAD_DIM, 1)
        for blk in range(n_blk):
            rows = slice(blk * bq, (blk + 1) * bq)
            keys = slice(blk * bq, (blk + 2) * bq)
            kn, kr, vn, vr = kn_all[keys], kr_all[keys], vn_all[keys], vr_all[keys]
            key_ok = jnp.logical_or(kcol >= bq, n > 0) if blk == 0 else None
            for half in range(2):
                g = 2 * c + half
                k_src, k_rot = (kn, kr) if half == 0 else (kr, kn)
                v_src, v_rot = (vn, vr) if half == 0 else (vr, vn)
                k_par = [jnp.where(lo, k_src, 0.0).astype(bf16), jnp.where(lo, 0.0, k_rot).astype(bf16)]
                v_par = [jnp.where(lo, v_src, 0.0).astype(bf16), jnp.where(lo, 0.0, v_rot).astype(bf16)]
                qs = [halfnorm(q_ref[0, rows, (g * pairs + p) * LANES:(g * pairs + p + 1) * LANES], qgain)
                      for p in range(pairs)]
                qg = (jnp.concatenate(qs, axis=0) * (1.0 / math.sqrt(HEAD_DIM))).astype(bf16)
                acc = None
                for par in range(2):
                    s = lax.dot_general(qg, k_par[par], (((1,), (1,)), ((), ())), preferred_element_type=f32)
                    s = s + bias_ref[g, par]
                    if key_ok is not None:
                        s = jnp.where(key_ok, s, -jnp.inf)
                    sink = jnp.concatenate(
                        [jnp.full((bq, 1), sinks_ref[g * Q_PER_KV + 2 * p + par], f32) for p in range(pairs)], axis=0)
                    m = jnp.maximum(jnp.max(s, axis=-1, keepdims=True), sink)
                    e = jnp.exp(s - m)
                    den = jnp.sum(e, axis=-1, keepdims=True) + jnp.exp(sink - m)
                    probs = (e * (1.0 / den)).astype(bf16)
                    o = jnp.dot(probs, v_par[par], preferred_element_type=f32)
                    acc = o if acc is None else acc + o
                for p in range(pairs):
                    col = (g * pairs + p) * LANES
                    o_ref[0, rows, col:col + LANES] = acc[p * bq:(p + 1) * bq].astype(o_ref.dtype)


def _t5_bucket(dist):
    nn = np.maximum(dist, 0)
    max_exact = NUM_BUCKETS // 2
    large = max_exact + (np.log(np.maximum(nn, 1) / max_exact) / np.log(MAX_DISTANCE / max_exact)
                         * (NUM_BUCKETS - max_exact)).astype(np.int32)
    large = np.minimum(large, NUM_BUCKETS - 1)
    return np.where(nn < max_exact, nn, large).astype(np.int32)


def _attention(qkv, q_gain, k_gain, sinks, rel_bias):
    bsz, seq, _ = qkv.shape
    bq = WINDOW
    q_dim = N_Q_HEADS * HEAD_DIM
    kv_dim = N_KV_HEADS * HEAD_DIM
    pairs = Q_PER_KV // 2
    ql = np.arange(bq)[:, None]
    kl = np.arange(2 * bq)[None, :]
    dist = ql + bq - kl
    in_window = (dist >= 0) & (dist < WINDOW)
    bucket = jnp.asarray(_t5_bucket(dist).reshape(1, -1))
    onehot_t = (bucket == jnp.arange(NUM_BUCKETS, dtype=i32)[:, None]).astype(f32)
    bias = jnp.dot(rel_bias.astype(f32).T, onehot_t, precision=lax.Precision.HIGHEST).reshape(N_Q_HEADS, bq, 2 * bq)
    bias = jnp.where(jnp.asarray(in_window)[None], bias, -jnp.inf)
    bias = bias.reshape(N_KV_HEADS, pairs, 2, bq, 2 * bq)
    bias = jnp.transpose(bias, (0, 2, 1, 3, 4)).reshape(N_KV_HEADS, 2, pairs * bq, 2 * bq)
    gain2 = lambda gn: jnp.concatenate([gn, gn]).reshape(1, LANES).astype(f32)
    k_blk = q_dim // kv_dim
    nb = ATTN_BLOCKS
    assert seq % (nb * bq) == 0
    grid_spec = pltpu.PrefetchScalarGridSpec(
        num_scalar_prefetch=1,
        grid=(bsz, seq // (nb * bq)),
        in_specs=[
            pl.BlockSpec((1, nb * bq, q_dim), lambda b, n, s: (b, n, 0)),
            pl.BlockSpec((1, bq, kv_dim), lambda b, n, s: (b, jnp.maximum(nb * n - 1, 0), k_blk)),
            pl.BlockSpec((1, nb * bq, kv_dim), lambda b, n, s: (b, n, k_blk)),
            pl.BlockSpec((1, bq, kv_dim), lambda b, n, s: (b, jnp.maximum(nb * n - 1, 0), k_blk + 1)),
            pl.BlockSpec((1, nb * bq, kv_dim), lambda b, n, s: (b, n, k_blk + 1)),
            pl.BlockSpec((1, LANES), lambda b, n, s: (0, 0)),
            pl.BlockSpec((1, LANES), lambda b, n, s: (0, 0)),
            pl.BlockSpec((N_KV_HEADS, 2, pairs * bq, 2 * bq), lambda b, n, s: (0, 0, 0, 0)),
        ],
        out_specs=pl.BlockSpec((1, nb * bq, q_dim), lambda b, n, s: (b, n, 0)),
    )
    return pl.pallas_call(
        _attn_kernel, grid_spec=grid_spec,
        out_shape=jax.ShapeDtypeStruct((bsz, seq, q_dim), bf16),
        compiler_params=_cparams(2), name="swa_attention",
    )(sinks.astype(f32), qkv, qkv, qkv, qkv, qkv, gain2(q_gain), gain2(k_gain), bias)


def _gelu_tanh(y):
    return 0.5 * y * (1.0 + jnp.tanh(math.sqrt(2.0 / math.pi) * (y + 0.044715 * (y * y * y))))


def _ssm_kernel(u_ref, bm_ref, cm_ref, are_ref, aim_ref, d_ref, y_ref, us, xs, st, ys):
    bsz, ts, _ = u_ref.shape
    n_state = are_ref.shape[-1]

    @pl.when(pl.program_id(1) == 0)
    def _():
        st[...] = jnp.zeros_like(st)

    n_slab = us.shape[0]
    for h in range(n_slab):
        for b in range(bsz):
            us[h, pl.ds(b, ts, stride=bsz), :] = u_ref[b, :, h * LANES:(h + 1) * LANES]
    u = jnp.concatenate([us[h] for h in range(n_slab)], axis=1)
    xs[...] = jnp.dot(u.astype(bf16), bm_ref[0], preferred_element_type=f32)
    a_re = jnp.broadcast_to(are_ref[0], (bsz, n_state))
    a_im = jnp.broadcast_to(aim_ref[0], (bsz, n_state))

    def step(t, carry):
        s_re, s_im = carry
        r0 = pl.multiple_of(t * bsz, bsz)
        x_re = xs[pl.ds(r0, bsz), 0:n_state]
        x_im = xs[pl.ds(r0, bsz), n_state:2 * n_state]
        n_re = a_re * s_re - a_im * s_im + x_re
        n_im = a_re * s_im + a_im * s_re + x_im
        xs[pl.ds(r0, bsz), 0:n_state] = n_re
        xs[pl.ds(r0, bsz), n_state:2 * n_state] = n_im
        return n_re, n_im

    s_re, s_im = lax.fori_loop(0, ts, step, (st[0], st[1]), unroll=8)
    st[0] = s_re
    st[1] = s_im
    y = (jnp.dot(xs[:, 0:n_state].astype(bf16), cm_ref[0, 0:n_state, :], preferred_element_type=f32)
         + jnp.dot(xs[:, n_state:2 * n_state].astype(bf16), cm_ref[0, n_state:2 * n_state, :],
                   preferred_element_type=f32)
         + d_ref[0] * u)
    y = _gelu_tanh(y)
    for h in range(n_slab):
        ys[h] = y[:, h * LANES:(h + 1) * LANES]
        for b in range(bsz):
            y_ref[b, :, h * LANES:(h + 1) * LANES] = ys[h, pl.ds(b, ts, stride=bsz), :].astype(y_ref.dtype)


def _ssm(h, lam_re, lam_im, log_dt, b_re, b_im, c_re, c_im, d_skip):
    bsz, seq, width = h.shape
    assert bsz == SUBLANES
    n_grp, n_st = lam_re.shape
    tile_ch = SSM_TILE_CH
    gpt = tile_ch // SSM_GROUP_CH
    n_tiles = width // tile_ch
    ns = gpt * n_st
    dt = jnp.exp(log_dt.astype(f32))[:, None]
    lr, li = lam_re.astype(f32), lam_im.astype(f32)
    mag = jnp.exp(lr * dt)
    ab_re, ab_im = mag * jnp.cos(li * dt), mag * jnp.sin(li * dt)
    den = lr * lr + li * li
    nr, ni = ab_re - 1.0, ab_im
    f_re = (nr * lr + ni * li) / den
    f_im = (ni * lr - nr * li) / den
    br, bi = b_re.astype(f32), b_im.astype(f32)
    bb_re = f_re[..., None] * br - f_im[..., None] * bi
    bb_im = f_re[..., None] * bi + f_im[..., None] * br
    eye = jnp.eye(gpt, dtype=f32)

    def blockdiag_in(bb):
        t = bb.reshape(n_tiles, gpt, n_st, SSM_GROUP_CH)
        return jnp.einsum("tgpc,gh->tgchp", t, eye).reshape(n_tiles, tile_ch, ns)

    def blockdiag_out(cc):
        t = cc.reshape(n_tiles, gpt, SSM_GROUP_CH, n_st)
        return jnp.einsum("tgcp,gh->tgphc", t, eye).reshape(n_tiles, ns, tile_ch)

    bm = jnp.concatenate([blockdiag_in(bb_re), blockdiag_in(bb_im)], axis=-1).astype(bf16)
    cm = jnp.concatenate([blockdiag_out(c_re.astype(f32)), -blockdiag_out(c_im.astype(f32))], axis=1).astype(bf16)
    a_re = ab_re.reshape(n_tiles, 1, ns)
    a_im = ab_im.reshape(n_tiles, 1, ns)
    dsk = d_skip.astype(f32).reshape(n_tiles, 1, tile_ch)
    ts = SSM_TIME_CHUNK
    return pl.pallas_call(
        _ssm_kernel,
        grid=(n_tiles, seq // ts),
        in_specs=[
            pl.BlockSpec((bsz, ts, tile_ch), lambda g, t: (0, t, g)),
            pl.BlockSpec((1, tile_ch, 2 * ns), lambda g, t: (g, 0, 0)),
            pl.BlockSpec((1, 2 * ns, tile_ch), lambda g, t: (g, 0, 0)),
            pl.BlockSpec((1, 1, ns), lambda g, t: (g, 0, 0)),
            pl.BlockSpec((1, 1, ns), lambda g, t: (g, 0, 0)),
            pl.BlockSpec((1, 1, tile_ch), lambda g, t: (g, 0, 0)),
        ],
        out_specs=pl.BlockSpec((bsz, ts, tile_ch), lambda g, t: (0, t, g)),
        out_shape=jax.ShapeDtypeStruct((bsz, seq, width), bf16),
        scratch_shapes=[
            pltpu.VMEM((tile_ch // LANES, ts * bsz, LANES), f32),
            pltpu.VMEM((ts * bsz, 2 * ns), f32),
            pltpu.VMEM((2, bsz, ns), f32),
            pltpu.VMEM((tile_ch // LANES, ts * bsz, LANES), f32),
        ],
        compiler_params=_cparams(2), name="s5_ssm",
    )(h, bm, cm, a_re, a_im, dsk)


GATHER_PITCH = 24


def _swiglu(g, l):
    glu = jnp.minimum(g, SWIGLU_LIMIT)
    lin = jnp.clip(l, -SWIGLU_LIMIT, SWIGLU_LIMIT)
    return glu * jax.nn.sigmoid(SWIGLU_ALPHA * glu) * (lin + 1.0)


def _for_row_groups(nsub, cast, fn):
    n_big = nsub // MOE_GROUP
    big_rows = MOE_GROUP * MOE_SUB

    @pl.when(n_big > 0)
    def _():
        fn(0, big_rows, cast())

    @pl.when(n_big == 0)
    def _():
        fn(0, MOE_SUB, cast())

    def big(i, carry):
        fn(pl.multiple_of(i * big_rows, big_rows), big_rows, None)
        return carry

    def small(i, carry):
        fn(pl.multiple_of(i * MOE_SUB, MOE_SUB), MOE_SUB, None)
        return carry

    lax.fori_loop(1, n_big, big, 0)
    lax.fori_loop(jnp.maximum(n_big * MOE_GROUP, 1), nsub, small, 0)


def _moe_kernel(sbe_ref, sbn_ref, idx0_ref, idxn_ref, h_hbm, wg_ref, wl_ref, bgu_ref, wd_ref, bd_ref, o_ref,
                xbf, stg, sem, a_scr, wg_bf, wl_bf, wd_bf, *, n_chunks):
    s = pl.program_id(0)
    c = pl.program_id(1)
    n_sb = pl.num_programs(0)
    nsub = sbn_ref[s]
    slot = lax.rem(s, 2)
    nsub_next = jnp.where(s + 1 < n_sb, sbn_ref[jnp.minimum(s + 1, n_sb - 1)], 0)
    n_col = xbf.shape[2] // LANES
    stg_rows = MOE_SUB * GATHER_PITCH

    def issue(idx_ref, off, st_slot):
        def body(r2, carry):
            for prio in range(N_DMA_PRIORITIES):
                r = r2 * N_DMA_PRIORITIES + prio
                src = pl.multiple_of(idx_ref[0, 0, off + r] * n_col, SUBLANES)
                dst = pl.multiple_of(st_slot * stg_rows + r * GATHER_PITCH, SUBLANES)
                pltpu.make_async_copy(h_hbm.at[pl.ds(src, n_col)], stg.at[pl.ds(dst, n_col)],
                                      sem.at[st_slot]).start(priority=prio)
            return carry

        lax.fori_loop(0, MOE_SUB // N_DMA_PRIORITIES, body, 0, unroll=4)

    def finish(st_slot, x_slot, sub):
        base = pl.multiple_of(st_slot * stg_rows, SUBLANES)
        pltpu.make_async_copy(h_hbm.at[pl.ds(0, MOE_SUB * n_col)], stg.at[pl.ds(base, MOE_SUB * n_col)],
                              sem.at[st_slot]).wait()
        r0 = pl.multiple_of(sub * MOE_SUB, MOE_SUB)
        for j in range(n_col):
            piece = stg[pl.ds(base + j, MOE_SUB, stride=GATHER_PITCH), :]
            xbf[x_slot, pl.ds(r0, MOE_SUB), j * LANES:(j + 1) * LANES] = piece.astype(bf16)

    @pl.when(jnp.logical_and(s == 0, c == 0))
    def _():
        def first(j, carry):
            issue(idx0_ref, j * MOE_SUB, 0)
            finish(0, 0, j)
            return carry

        lax.fori_loop(0, nsub, first, 0)

    @pl.when(jnp.logical_and(c >= 1, c - 1 < nsub_next))
    def _():
        finish(lax.rem(c - 1, 2), 1 - slot, c - 1)

    @pl.when(c < nsub_next)
    def _():
        issue(idxn_ref, c * MOE_SUB, lax.rem(c, 2))

    @pl.when(jnp.logical_and(c < n_chunks, nsub > 0))
    def _():
        def cast():
            wg, wl = wg_ref[...].astype(bf16), wl_ref[...].astype(bf16)
            wg_bf[...] = wg
            wl_bf[...] = wl
            return wg, wl

        def up(r, rows, weights):
            wg, wl = weights if weights is not None else (wg_bf[...], wl_bf[...])
            xs = xbf[slot, pl.ds(r, rows), :]
            g = jnp.dot(xs, wg, preferred_element_type=f32) + bgu_ref[pl.ds(c, 1), :]
            l = jnp.dot(xs, wl, preferred_element_type=f32) + bgu_ref[pl.ds(n_chunks + c, 1), :]
            a_scr[c, pl.ds(r, rows), :] = _swiglu(g, l).astype(bf16)

        _for_row_groups(nsub, cast, up)

    @pl.when(jnp.logical_and(c >= n_chunks, nsub > 0))
    def _():
        def cast():
            wd = wd_ref[...].astype(bf16)
            wd_bf[...] = wd
            return wd

        def down(r, rows, weights):
            wd = weights if weights is not None else wd_bf[...]
            a = jnp.concatenate([a_scr[cc, pl.ds(r, rows), :] for cc in range(n_chunks)], axis=1)
            o_ref[pl.ds(r, rows), :] = jnp.dot(a, wd, preferred_element_type=f32) + bd_ref[pl.ds(c - n_chunks, 1), :]

        _for_row_groups(nsub, cast, down)

    @pl.when(c >= n_chunks)
    def _():
        def zero(i, carry):
            r = pl.multiple_of(i * MOE_SUB, MOE_SUB)
            o_ref[pl.ds(r, MOE_SUB), :] = jnp.zeros((MOE_SUB, o_ref.shape[1]), f32)
            return carry

        lax.fori_loop(nsub, MOE_SB_SUBS, zero, 0)


def _moe_experts(h_rows, n_col, row_tok, sb_expert, sb_nsub, w_gate_up, b_gate_up, w_down, b_down, layer):
    n_rows = row_tok.shape[0]
    d = n_col * LANES
    n_sb = n_rows // MOE_SB_ROWS
    d_exp = w_down.shape[2]
    n_chunks = d_exp // MOE_CHUNK
    n_out_chunks = d // MOE_DOWN_CHUNK
    last = n_chunks - 1
    last_out = n_out_chunks - 1

    def valid(s, n):
        return n[s] > 0

    def wg_map(s, c, e, n):
        return (layer, e[s], 0, jnp.where(valid(s, n), jnp.minimum(c, last), last))

    def wl_map(s, c, e, n):
        return (layer, e[s], 0, n_chunks + jnp.where(valid(s, n), jnp.minimum(c, last), last))

    def wd_map(s, c, e, n):
        return (layer, e[s], 0, jnp.where(valid(s, n), jnp.maximum(c - n_chunks, 0), last_out))

    def out_map(s, c, e, n):
        return (s, jnp.maximum(c - n_chunks, 0))

    def next_idx_map(s, c, e, n):
        return (jnp.minimum(s + 1, n_sb - 1), 0, 0)

    def bias_map(s, c, e, n):
        return (layer, e[s], 0, 0)

    grid_spec = pltpu.PrefetchScalarGridSpec(
        num_scalar_prefetch=2,
        grid=(n_sb, n_chunks + n_out_chunks),
        in_specs=[
            pl.BlockSpec((1, 1, MOE_SB_ROWS), lambda s, c, e, n: (0, 0, 0), memory_space=pltpu.SMEM),
            pl.BlockSpec((1, 1, MOE_SB_ROWS), next_idx_map, memory_space=pltpu.SMEM),
            pl.BlockSpec(memory_space=pl.ANY),
            pl.BlockSpec((None, None, d, MOE_CHUNK), wg_map),
            pl.BlockSpec((None, None, d, MOE_CHUNK), wl_map),
            pl.BlockSpec((None, None, 2 * n_chunks, MOE_CHUNK), bias_map),
            pl.BlockSpec((None, None, d_exp, MOE_DOWN_CHUNK), wd_map),
            pl.BlockSpec((None, None, n_out_chunks, MOE_DOWN_CHUNK), bias_map),
        ],
        out_specs=pl.BlockSpec((MOE_SB_ROWS, MOE_DOWN_CHUNK), out_map),
        scratch_shapes=[
            pltpu.VMEM((2, MOE_SB_ROWS, d), bf16),
            pltpu.VMEM((2 * MOE_SUB * GATHER_PITCH, LANES), f32),
            pltpu.SemaphoreType.DMA((2,)),
            pltpu.VMEM((n_chunks, MOE_SB_ROWS, MOE_CHUNK), bf16),
            pltpu.VMEM((d, MOE_CHUNK), bf16),
            pltpu.VMEM((d, MOE_CHUNK), bf16),
            pltpu.VMEM((d_exp, MOE_DOWN_CHUNK), bf16),
        ],
    )
    n_l, n_e = b_gate_up.shape[:2]
    return pl.pallas_call(
        functools.partial(_moe_kernel, n_chunks=n_chunks), grid_spec=grid_spec,
        out_shape=jax.ShapeDtypeStruct((n_rows, d), f32),
        compiler_params=_cparams(2, MOE_VMEM_LIMIT), name="moe_experts",
    )(sb_expert, sb_nsub, row_tok.reshape(n_sb, 1, MOE_SB_ROWS), row_tok.reshape(n_sb, 1, MOE_SB_ROWS),
      h_rows, w_gate_up, w_gate_up,
      b_gate_up.reshape(n_l, n_e, 2 * n_chunks, MOE_CHUNK),
      w_down, b_down.reshape(n_l, n_e, n_out_chunks, MOE_DOWN_CHUNK))


def _combine_kernel(dest_ref, dest_next_ref, y_hbm, gate_ref, x_ref, g2_ref, o_ref, ybuf, sem):
    i = pl.program_id(0)
    n = pl.num_programs(0)
    tm = x_ref.shape[0]
    slot = lax.rem(i, 2)

    def issue(dest_blk, sl):
        for k in range(TOP_K):
            def body(t2, carry, k=k):
                for prio in range(N_DMA_PRIORITIES):
                    t = t2 * N_DMA_PRIORITIES + prio
                    pltpu.make_async_copy(y_hbm.at[pl.ds(dest_blk[0, 0, k * tm + t], 1)],
                                          ybuf.at[sl * TOP_K + k, pl.ds(t, 1)], sem.at[sl]).start(priority=prio)
                return carry

            lax.fori_loop(0, tm // N_DMA_PRIORITIES, body, 0, unroll=4)

    @pl.when(i == 0)
    def _():
        issue(dest_ref, 0)

    @pl.when(i + 1 < n)
    def _():
        issue(dest_next_ref, 1 - slot)

    for k in range(TOP_K):
        pltpu.make_async_copy(y_hbm.at[pl.ds(0, tm)], ybuf.at[slot * TOP_K + k], sem.at[slot]).wait()
    gates = gate_ref[...]
    acc = gates[:, 0:1] * ybuf[slot * TOP_K]
    for k in range(1, TOP_K):
        acc = acc + gates[:, k:k + 1] * ybuf[slot * TOP_K + k]
    o_ref[...] = x_ref[...] + g2_ref[...] * acc


def _moe_combine(y_sorted, dest, gates, x2d, mod3, gate_chunk, seq):
    n_tok, d = x2d.shape
    tm = COMBINE_TM
    assert seq % tm == 0
    n_tiles = n_tok // tm
    dest_km = jnp.transpose(dest.reshape(TOP_K, n_tiles, tm), (1, 0, 2)).reshape(n_tiles, 1, TOP_K * tm)
    tiles_per_batch = seq // tm
    return pl.pallas_call(
        _combine_kernel,
        grid=(n_tiles,),
        in_specs=[
            pl.BlockSpec((1, 1, TOP_K * tm), lambda i: (i, 0, 0), memory_space=pltpu.SMEM),
            pl.BlockSpec((1, 1, TOP_K * tm), lambda i: (jnp.minimum(i + 1, n_tiles - 1), 0, 0),
                         memory_space=pltpu.SMEM),
            pl.BlockSpec(memory_space=pl.ANY),
            pl.BlockSpec((tm, LANES), lambda i: (i, 0)),
            pl.BlockSpec((tm, d), lambda i: (i, 0)),
            pl.BlockSpec((None, 1, d), lambda i: (i // tiles_per_batch, 0, gate_chunk)),
        ],
        out_specs=pl.BlockSpec((tm, d), lambda i: (i, 0)),
        out_shape=jax.ShapeDtypeStruct((n_tok, d), f32),
        scratch_shapes=[pltpu.VMEM((2 * TOP_K, tm, d), f32), pltpu.SemaphoreType.DMA((2,))],
        compiler_params=_cparams(1), name="moe_combine",
    )(dest_km, dest_km, y_sorted, gates, x2d, mod3)


def _routing_tables(idx_km):
    n_tok = idx_km.shape[1]
    n_assign = n_tok * TOP_K
    n_sb_max = (n_assign // MOE_SUB + N_EXPERTS + N_EXPERTS * (MOE_SB_SUBS - 1)) // MOE_SB_SUBS
    experts = jnp.arange(N_EXPERTS, dtype=i32)
    onehot = idx_km[:, :, None] == experts[None, None, :]
    per_tok = jnp.sum(onehot.astype(i32), axis=0)
    before = jnp.cumsum(per_tok, axis=0) - per_tok
    rank = jnp.sum(jnp.where(onehot, before[None], 0), axis=2)
    counts = before[-1] + per_tok[-1]
    n_sub_e = (counts + MOE_SUB - 1) // MOE_SUB
    n_sb_e = (n_sub_e + MOE_SB_SUBS - 1) // MOE_SB_SUBS
    sb_end = jnp.cumsum(n_sb_e)
    sb_start = sb_end - n_sb_e
    base = n_sub_e // jnp.maximum(n_sb_e, 1)
    rem = n_sub_e - base * n_sb_e
    tables = jnp.stack([sb_start, base, rem], axis=1).astype(bf16)
    looked = jnp.dot(onehot.astype(bf16).reshape(n_assign, N_EXPERTS), tables,
                     preferred_element_type=f32).astype(i32).reshape(TOP_K, n_tok, 3)
    a_start, a_base, a_rem = looked[..., 0], looked[..., 1], looked[..., 2]
    assert MOE_SUB & (MOE_SUB - 1) == 0
    q = lax.shift_right_logical(rank, int(math.log2(MOE_SUB)))
    max_sb_per_expert = -(-(-(-n_tok // MOE_SUB)) // MOE_SB_SUBS)
    sb_local = jnp.zeros_like(q)
    for j in range(1, max_sb_per_expert):
        sb_local = sb_local + (q >= j * a_base + jnp.minimum(j, a_rem)).astype(i32)
    sub_in = q - (sb_local * a_base + jnp.minimum(sb_local, a_rem))
    dest = ((a_start + sb_local) * MOE_SB_ROWS + sub_in * MOE_SUB + (rank & (MOE_SUB - 1))).astype(i32)
    dest = lax.optimization_barrier(dest)
    tok = jnp.broadcast_to(jnp.arange(n_tok, dtype=i32)[None, :], (TOP_K, n_tok))
    row_tok = jnp.zeros((n_sb_max * MOE_SB_ROWS,), i32).at[dest.reshape(-1)].set(
        tok.reshape(-1), unique_indices=True, mode="promise_in_bounds")
    s = jnp.arange(n_sb_max, dtype=i32)
    n_sb_total = sb_end[-1]
    sb_valid = s < n_sb_total
    sb_src = jnp.where(sb_valid, s, n_sb_total - 1).astype(i32)
    sb_e = jnp.minimum(jnp.searchsorted(sb_end, sb_src, side="right"), N_EXPERTS - 1).astype(i32)
    sb_local_s = sb_src - sb_start[sb_e]
    sb_nsub = jnp.where(sb_valid, base[sb_e] + (sb_local_s < rem[sb_e]).astype(i32), 0).astype(i32)
    return dest, row_tok, sb_e, sb_nsub


def kernel(x, c, rel_bias, norm_gain, ada_w, ada_b, attn_w_qkv, attn_b_qkv, attn_q_gain, attn_k_gain, attn_sinks, attn_w_o, attn_b_o, ssm_lam_re, ssm_lam_im, ssm_log_dt, ssm_b_re, ssm_b_im, ssm_c_re, ssm_c_im, ssm_d, ssm_w_glu_a, ssm_b_glu_a, ssm_w_glu_b, ssm_b_glu_b, moe_w_router, moe_b_router, moe_w_gate_up, moe_b_gate_up, moe_w_down, moe_b_down):
    bsz, seq, d = x.shape
    n_tok = bsz * seq
    depth = norm_gain.shape[0]
    tm, tn = DENSE_TM, DENSE_TN
    assert seq % tm == 0 and d % tn == 0
    SH1, SC1, G1, SH2, SC2, G2 = range(6)

    def gated(accs, extras):
        x_res, gate = extras
        return x_res + gate * accs[0]

    def gated_glu(accs, extras):
        x_res, gate = extras
        return x_res + gate * (accs[0] * jax.nn.sigmoid(accs[1]))

    for layer in range(depth):
        mod3 = _modulation(c, ada_w, ada_b, layer).reshape(bsz, 1, 6 * d)
        x2d = x.reshape(n_tok, d)
        res_extras = _gated_residual_extras(x2d, mod3, G1, d, tm, tn, seq)
        i = layer // 2
        if layer % 2 == 0:
            h = _norm(x, norm_gain[layer, 0], mod3, SC1, SH1, bf16)
            qkv = _dense(h.reshape(n_tok, d), [attn_w_qkv], [attn_b_qkv], i, lambda accs, extras: accs[0], [],
                         f32, tm, DENSE_TN_QKV, "qkv_proj")
            o = _attention(qkv.reshape(bsz, seq, -1), attn_q_gain[i], attn_k_gain[i], attn_sinks[i], rel_bias)
            x2d = _dense(o.reshape(n_tok, -1), [attn_w_o], [attn_b_o], i, gated,
                         _gated_residual_extras(x2d, mod3, G1, d, tm, DENSE_TN_WIDE, seq), f32, tm, DENSE_TN_WIDE,
                         "attn_out")
        else:
            h = _norm(x, norm_gain[layer, 0], mod3, SC1, SH1, f32)
            y = _ssm(h, ssm_lam_re[i], ssm_lam_im[i], ssm_log_dt[i], ssm_b_re[i], ssm_b_im[i],
                     ssm_c_re[i], ssm_c_im[i], ssm_d[i])
            x2d = _dense(y.reshape(n_tok, d), [ssm_w_glu_a, ssm_w_glu_b], [ssm_b_glu_a, ssm_b_glu_b], i,
                         gated_glu, res_extras, f32, tm, tn, "ssm_glu")
        x = x2d.reshape(bsz, seq, d)
        h, top_idx, gates = _norm(x, norm_gain[layer, 1], mod3, SC2, SH2, f32,
                                  router=(moe_w_router, moe_b_router, layer))
        idx_km = top_idx.reshape(n_tok, LANES)[:, :TOP_K].T
        dest, row_tok, sb_e, sb_nsub = _routing_tables(idx_km)
        y_sorted = _moe_experts(h, d // LANES, row_tok, sb_e, sb_nsub, moe_w_gate_up, moe_b_gate_up,
                                moe_w_down, moe_b_down, layer)
        x = _moe_combine(y_sorted, dest, gates.reshape(n_tok, LANES), x2d, mod3, G2, seq).reshape(bsz, seq, d)
    return x
```
